```python
import math
import jax, jax.numpy as jnp
from jax import lax
import numpy as np

D_MODEL = 1024
BATCH = 2
SEQ = 8192
DEPTH = 2

HEAD_DIM = 64
D_MIX = D_MODEL
BLOCK = 128
NORM_EPS = 1e-5
A_HEADS = 4
A_QK_DIM = HEAD_DIM // 2
A_V_DIM = HEAD_DIM
A_WIDTH = A_HEADS * A_V_DIM
B_HEADS = 8
B_KV_HEADS = 2
B_GROUP = B_HEADS // B_KV_HEADS
B_WIDTH = B_HEADS * HEAD_DIM
WINDOW = 128
C_HEADS = 4
C_WIDTH = C_HEADS * HEAD_DIM
N_ALIBI_HEADS = A_HEADS + B_HEADS

SPLITS = (
    A_HEADS * 2 * A_QK_DIM, A_HEADS * 2 * A_QK_DIM, A_WIDTH, A_WIDTH,
    B_WIDTH, B_KV_HEADS * HEAD_DIM, B_KV_HEADS * HEAD_DIM, B_WIDTH,
    C_WIDTH, C_WIDTH, C_WIDTH, C_WIDTH,
)
D_IN = sum(SPLITS)
SPLIT_POINTS = tuple(int(v) for v in np.cumsum(SPLITS)[:-1])

kernel_name = "hymba_diff_swa_stickbreak_block"


def alibi_slopes():
    s = 2.0 ** (-8.0 * np.arange(1, N_ALIBI_HEADS + 1) / N_ALIBI_HEADS)
    s = s.astype(np.float32)
    return jnp.asarray(s[B_HEADS:]), jnp.asarray(s[:B_HEADS])


def rmsnorm(x, g):
    xf = x.astype(jnp.float32)
    y = xf * lax.rsqrt(jnp.mean(xf * xf, axis=-1, keepdims=True) + NORM_EPS)
    return (y * g.astype(jnp.float32)).astype(x.dtype)


def diff_attention(q, k, v, lam, slopes):
    b, s = q.shape[:2]
    nblk = s // BLOCK
    scale = A_QK_DIM ** -0.5
    kpos = jnp.arange(s)
    qb = q.reshape(b, nblk, BLOCK, A_HEADS, 2, A_QK_DIM).transpose(1, 0, 2, 3, 4, 5)

    def one_block(args):
        qblk, i = args
        qpos = i * BLOCK + jnp.arange(BLOCK)
        sc = jnp.einsum('bqhmd,bkhmd->bhmqk', qblk, k,
                        preferred_element_type=jnp.float32) * scale
        dist = (qpos[:, None] - kpos[None, :]).astype(jnp.float32)
        sc = sc - (slopes[:, None, None, None] * dist)[None]
        sc = jnp.where((dist >= 0)[None, None, None], sc, -jnp.inf)
        p = jax.nn.softmax(sc, axis=-1)
        w = p[:, :, 0] - lam * p[:, :, 1]
        return jnp.einsum('bhqk,bkhd->bqhd', w.astype(v.dtype), v)

    out = lax.map(one_block, (qb, jnp.arange(nblk)))
    return out.transpose(1, 0, 2, 3, 4).reshape(b, s, A_HEADS, A_V_DIM)


def window_attention(q, k, v, sinks, slopes):
    b, s = q.shape[:2]
    nblk = s // BLOCK
    qb = q.reshape(b, nblk, BLOCK, B_KV_HEADS, B_GROUP, HEAD_DIM)

    def banded(t):
        tb = t.reshape(b, nblk, BLOCK, B_KV_HEADS, HEAD_DIM)
        prev = jnp.pad(tb, ((0, 0), (1, 0), (0, 0), (0, 0), (0, 0)))[:, :-1]
        return jnp.concatenate([prev, tb], axis=2)

    kb, vb = banded(k), banded(v)
    sc = jnp.einsum('bnqhgd,bnkhd->bnhgqk', qb, kb,
                    preferred_element_type=jnp.float32) * HEAD_DIM ** -0.5
    qi = jnp.arange(BLOCK)
    kj = jnp.arange(2 * BLOCK) - BLOCK
    dist = qi[:, None] - kj[None, :]
    blk = jnp.arange(nblk)
    valid = ((dist >= 0) & (dist < WINDOW))[None] & \
        ((blk[:, None, None] * BLOCK + kj[None, None, :]) >= 0)
    sl = slopes.reshape(B_KV_HEADS, B_GROUP)
    sc = sc - sl[:, :, None, None] * dist.astype(jnp.float32)
    sc = jnp.where(valid[None, :, None, None], sc, -jnp.inf)
    sink_col = jnp.broadcast_to(
        sinks.reshape(B_KV_HEADS, B_GROUP)[None, None, :, :, None, None].astype(jnp.float32),
        sc.shape[:-1] + (1,))
    p = jax.nn.softmax(jnp.concatenate([sc, sink_col], axis=-1), axis=-1)[..., :-1]
    out = jnp.einsum('bnhgqk,bnkhd->bnqhgd', p.astype(v.dtype), vb)
    return out.reshape(b, s, B_HEADS, HEAD_DIM)


def stick_breaking_attention(q, k, v):
    b, s = q.shape[:2]
    nblk = s // BLOCK
    scale = HEAD_DIM ** -0.5
    kpos = jnp.arange(s)
    qb = q.reshape(b, nblk, BLOCK, C_HEADS, HEAD_DIM).transpose(1, 0, 2, 3, 4)

    def one_block(args):
        qblk, i = args
        qpos = i * BLOCK + jnp.arange(BLOCK)
        z = jnp.einsum('bqhd,bkhd->bhqk', qblk, k,
                       preferred_element_type=jnp.float32) * scale
        past = kpos[None, :] < qpos[:, None]
        log_beta = jax.nn.log_sigmoid(z)
        log_1m_beta = jnp.where(past, jax.nn.log_sigmoid(-z), 0.0)
        tail = lax.cumsum(log_1m_beta, axis=3, reverse=True) - log_1m_beta
        a = jnp.where(past, jnp.exp(log_beta + tail), 0.0)
        return jnp.einsum('bhqk,bkhd->bqhd', a.astype(v.dtype), v)

    out = lax.map(one_block, (qb, jnp.arange(nblk)))
    return out.transpose(1, 0, 2, 3, 4).reshape(b, s, C_HEADS, HEAD_DIM)


def hybrid_layer(x, norm_g, w_in, lq1, lk1, lq2, lk2, subln_g, sinks, w_out, layer_idx):
    b, s, _ = x.shape
    slopes_a, slopes_b = alibi_slopes()
    h = rmsnorm(x, norm_g)
    proj = jnp.einsum('bsd,de->bse', h, w_in)
    (aq, ak, av, ag, bq, bk, bv, bg, cq, ck, cv, cg) = jnp.split(proj, SPLIT_POINTS, axis=-1)

    lambda_init = 0.8 - 0.6 * math.exp(-0.3 * layer_idx)
    lam = (jnp.exp(jnp.sum(lq1.astype(jnp.float32) * lk1.astype(jnp.float32)))
           - jnp.exp(jnp.sum(lq2.astype(jnp.float32) * lk2.astype(jnp.float32)))
           + lambda_init)
    ya = diff_attention(aq.reshape(b, s, A_HEADS, 2, A_QK_DIM),
                        ak.reshape(b, s, A_HEADS, 2, A_QK_DIM),
                        av.reshape(b, s, A_HEADS, A_V_DIM), lam, slopes_a)
    ya = rmsnorm(ya, subln_g) * (1.0 - lambda_init)
    ya = ya.reshape(b, s, A_WIDTH) * jax.nn.silu(ag)

    yb = window_attention(bq.reshape(b, s, B_HEADS, HEAD_DIM),
                          bk.reshape(b, s, B_KV_HEADS, HEAD_DIM),
                          bv.reshape(b, s, B_KV_HEADS, HEAD_DIM), sinks, slopes_b)
    yb = yb.reshape(b, s, B_WIDTH) * jax.nn.silu(bg)

    yc = stick_breaking_attention(cq.reshape(b, s, C_HEADS, HEAD_DIM),
                                  ck.reshape(b, s, C_HEADS, HEAD_DIM),
                                  cv.reshape(b, s, C_HEADS, HEAD_DIM))
    yc = yc.reshape(b, s, C_WIDTH) * jax.nn.silu(cg)

    y = jnp.concatenate([ya, yb, yc], axis=-1)
    return x + jnp.einsum('bse,ed->bsd', y, w_out)


def setup_inputs(seed: int = 0) -> dict:
    key = jax.random.key(seed)
    ks = jax.random.split(key, 12)
    f32 = jnp.float32
    return {
        "x": jax.random.normal(ks[0], (BATCH, SEQ, D_MODEL), f32),
        "norm_g": 1.0 + 0.02 * jax.random.normal(ks[1], (DEPTH, D_MODEL), f32),
        "w_in": jax.random.normal(ks[2], (DEPTH, D_MODEL, D_IN), f32) * D_MODEL ** -0.5,
        "lambda_q1": 0.1 * jax.random.normal(ks[3], (DEPTH, A_QK_DIM), f32),
        "lambda_k1": 0.1 * jax.random.normal(ks[4], (DEPTH, A_QK_DIM), f32),
        "lambda_q2": 0.1 * jax.random.normal(ks[5], (DEPTH, A_QK_DIM), f32),
        "lambda_k2": 0.1 * jax.random.normal(ks[6], (DEPTH, A_QK_DIM), f32),
        "subln_g": 1.0 + 0.02 * jax.random.normal(ks[7], (DEPTH, A_V_DIM), f32),
        "sinks": 0.5 * jax.random.normal(ks[8], (DEPTH, B_HEADS), f32),
        "w_out": jax.random.normal(ks[9], (DEPTH, D_MIX, D_MODEL), f32) * D_MIX ** -0.5,
        "final_g": 1.0 + 0.02 * jax.random.normal(ks[10], (D_MODEL,), f32),
    }


def reference(x, norm_g, w_in, lambda_q1, lambda_k1, lambda_q2, lambda_k2, subln_g, sinks, w_out, final_g):
    for l in range(DEPTH):
        x = hybrid_layer(x, norm_g[l], w_in[l], lambda_q1[l], lambda_k1[l], lambda_q2[l],
                         lambda_k2[l], subln_g[l], sinks[l], w_out[l], l)
    return rmsnorm(x, final_g)
```

```python
import functools
import math

import numpy as np
import jax
import jax.numpy as jnp
from jax import lax
from jax.experimental import pallas as pl
from jax.experimental.pallas import tpu as pltpu

F32 = jnp.float32
BF16 = jnp.bfloat16

HEAD_DIM = 64
NORM_EPS = 1e-5
A_HEADS = 4
A_QK_DIM = HEAD_DIM // 2
B_HEADS = 8
B_KV_HEADS = 2
B_GROUP = B_HEADS // B_KV_HEADS
C_HEADS = 4
WINDOW = 128
A_W = A_HEADS * HEAD_DIM
B_W = B_HEADS * HEAD_DIM
B_KV_W = B_KV_HEADS * HEAD_DIM
C_W = C_HEADS * HEAD_DIM
QKV_W = 3 * A_W + B_W + 2 * B_KV_W + 3 * C_W
GATE_W = A_W + B_W + C_W
MASK_VALUE = -1e30
VMEM_LIMIT_BYTES = 48 * 1024 * 1024

ROW_TILE = 256
A_TILE = 256
C_TILE = 256
B_TILE = WINDOW


def _alibi_slopes():
    n = A_HEADS + B_HEADS
    s = (2.0 ** (-8.0 * np.arange(1, n + 1) / n)).astype(np.float32)
    return s[B_HEADS:], s[:B_HEADS]


def _split3_bf16(v):
    out = []
    r = jnp.asarray(v, F32)
    for _ in range(3):
        p = r.astype(BF16)
        out.append(p)
        r = r - p.astype(F32)
    return out


def _dot_nt(a, b):
    return lax.dot_general(a, b, (((1,), (1,)), ((), ())), preferred_element_type=F32)


def _inproj_kernel(x_ref, g_ref, w_ref, cs_ref, qkv_ref, gate_ref):
    x = x_ref[...]
    ms = jnp.mean(x * x, axis=-1, keepdims=True)
    h = (x * lax.rsqrt(ms + NORM_EPS) * g_ref[...]).astype(BF16)
    qkv = jnp.dot(h, w_ref[:, :QKV_W], preferred_element_type=F32)
    qkv_ref[...] = (qkv * cs_ref[...]).astype(BF16)
    gate_ref[...] = jnp.dot(h, w_ref[:, QKV_W:], preferred_element_type=F32)


def _inproj(x2, g, w, cs):
    m, d = x2.shape
    n = w.shape[1]
    return pl.pallas_call(
        _inproj_kernel,
        grid=(m // ROW_TILE,),
        in_specs=[
            pl.BlockSpec((ROW_TILE, d), lambda i: (i, 0)),
            pl.BlockSpec((1, d), lambda i: (0, 0)),
            pl.BlockSpec((d, n), lambda i: (0, 0)),
            pl.BlockSpec((1, QKV_W), lambda i: (0, 0)),
        ],
        out_specs=[
            pl.BlockSpec((ROW_TILE, QKV_W), lambda i: (i, 0)),
            pl.BlockSpec((ROW_TILE, GATE_W), lambda i: (i, 0)),
        ],
        out_shape=[
            jax.ShapeDtypeStruct((m, QKV_W), BF16),
            jax.ShapeDtypeStruct((m, GATE_W), F32),
        ],
        compiler_params=pltpu.CompilerParams(
            dimension_semantics=("parallel",), vmem_limit_bytes=VMEM_LIMIT_BYTES),
        name="inproj",
    )(x2, g, w, cs)


def _attn_a_kernel(lq1_ref, lk1_ref, lq2_ref, lk2_ref, sg_ref, q_ref, k_ref, v_ref, o_ref,
                   m_sc, l_sc, acc_sc, *, lambda_init):
    t = A_TILE
    qi = pl.program_id(2)
    m_sc[...] = jnp.full(m_sc.shape, MASK_VALUE, F32)
    l_sc[...] = jnp.zeros(l_sc.shape, F32)
    acc_sc[...] = jnp.zeros(acc_sc.shape, F32)

    def tile(kj, masked):
        ks = pl.ds(pl.multiple_of(kj * t, t), t)
        v = v_ref[ks, :]
        for mp in range(2):
            s = _dot_nt(q_ref[mp], k_ref[mp, ks, :])
            if masked:
                row = lax.broadcasted_iota(jnp.int32, (t, t), 0)
                col = lax.broadcasted_iota(jnp.int32, (t, t), 1)
                s = jnp.where(row >= col, s, MASK_VALUE)
            m_old = m_sc[mp]
            m_new = jnp.maximum(m_old, jnp.max(s, axis=-1, keepdims=True))
            alpha = jnp.exp(m_old - m_new)
            p = jnp.exp(s - m_new)
            l_sc[mp] = alpha * l_sc[mp] + jnp.sum(p, axis=-1, keepdims=True)
            acc_sc[mp] = alpha * acc_sc[mp] + jnp.dot(p.astype(BF16), v,
                                                      preferred_element_type=F32)
            m_sc[mp] = m_new

    def body(kj, carry):
        tile(kj, False)
        return carry

    lax.fori_loop(0, qi, body, 0)
    tile(qi, True)

    lam = (jnp.exp(jnp.sum(lq1_ref[...] * lk1_ref[...], axis=-1, keepdims=True))
           - jnp.exp(jnp.sum(lq2_ref[...] * lk2_ref[...], axis=-1, keepdims=True))
           + lambda_init)
    o = acc_sc[0] / l_sc[0] - lam * (acc_sc[1] / l_sc[1])
    ms = jnp.mean(o * o, axis=-1, keepdims=True)
    o_ref[...] = o * lax.rsqrt(ms + NORM_EPS) * sg_ref[...] * (1.0 - lambda_init)


def _attn_a(q, k, v, lq1, lk1, lq2, lk2, sg, lambda_init):
    b, nh, _, s, dq = q.shape
    t = A_TILE
    small = lambda shape: pl.BlockSpec(shape, lambda bi, hi, qi: (0, 0))
    return pl.pallas_call(
        functools.partial(_attn_a_kernel, lambda_init=lambda_init),
        grid=(b, nh, s // t),
        in_specs=[
            small((1, A_QK_DIM)), small((1, A_QK_DIM)), small((1, A_QK_DIM)), small((1, A_QK_DIM)),
            small((1, HEAD_DIM)),
            pl.BlockSpec((None, None, 2, t, dq), lambda bi, hi, qi: (bi, hi, 0, qi, 0)),
            pl.BlockSpec((None, None, 2, s, dq), lambda bi, hi, qi: (bi, hi, 0, 0, 0)),
            pl.BlockSpec((None, None, s, HEAD_DIM), lambda bi, hi, qi: (bi, hi, 0, 0)),
        ],
        out_specs=pl.BlockSpec((None, None, t, HEAD_DIM), lambda bi, hi, qi: (bi, hi, qi, 0)),
        out_shape=jax.ShapeDtypeStruct((b, nh, s, HEAD_DIM), F32),
        scratch_shapes=[
            pltpu.VMEM((2, t, 1), F32),
            pltpu.VMEM((2, t, 1), F32),
            pltpu.VMEM((2, t, HEAD_DIM), F32),
        ],
        compiler_params=pltpu.CompilerParams(
            dimension_semantics=("parallel", "parallel", "arbitrary"),
            vmem_limit_bytes=VMEM_LIMIT_BYTES),
        name="attn_a",
    )(lq1, lk1, lq2, lk2, sg, q, k, v)


def _attn_b_kernel(sink_ref, q_ref, kp_ref, kc_ref, vp_ref, vc_ref, o_ref, *, slopes):
    t = B_TILE
    qi = pl.program_id(1)
    row = lax.broadcasted_iota(jnp.int32, (t, 2 * t), 0)
    col = lax.broadcasted_iota(jnp.int32, (t, 2 * t), 1) - t
    dist = row - col
    valid = (dist >= 0) & (dist < WINDOW) & (qi * t + col >= 0)
    distf = dist.astype(F32)
    for g in range(B_KV_HEADS):
        k = jnp.concatenate([kp_ref[g], kc_ref[g]], axis=0)
        v = jnp.concatenate([vp_ref[g], vc_ref[g]], axis=0)
        for j in range(B_GROUP):
            h = g * B_GROUP + j
            sink = sink_ref[h]
            s = _dot_nt(q_ref[h], k) - slopes[h] * distf
            s = jnp.where(valid, s, MASK_VALUE)
            m = jnp.maximum(jnp.max(s, axis=-1, keepdims=True), sink)
            p = jnp.exp(s - m)
            den = jnp.sum(p, axis=-1, keepdims=True) + jnp.exp(sink - m)
            o_ref[h] = jnp.dot(p.astype(BF16), v, preferred_element_type=F32) / den


def _attn_b(q, k, v, sinks, slopes):
    b, nh, s, d = q.shape
    t = B_TILE
    prev = lambda bi, qi: (bi, 0, jnp.maximum(qi - 1, 0), 0)
    cur = lambda bi, qi: (bi, 0, qi, 0)
    return pl.pallas_call(
        functools.partial(_attn_b_kernel, slopes=slopes),
        grid=(b, s // t),
        in_specs=[
            pl.BlockSpec(memory_space=pltpu.SMEM),
            pl.BlockSpec((None, nh, t, d), cur),
            pl.BlockSpec((None, B_KV_HEADS, t, d), prev),
            pl.BlockSpec((None, B_KV_HEADS, t, d), cur),
            pl.BlockSpec((None, B_KV_HEADS, t, d), prev),
            pl.BlockSpec((None, B_KV_HEADS, t, d), cur),
        ],
        out_specs=pl.BlockSpec((None, nh, t, d), cur),
        out_shape=jax.ShapeDtypeStruct((b, nh, s, d), F32),
        compiler_params=pltpu.CompilerParams(
            dimension_semantics=("parallel", "arbitrary"),
            vmem_limit_bytes=VMEM_LIMIT_BYTES),
        name="attn_b",
    )(sinks, q, k, k, v, v)


def _attn_c_kernel(q_ref, k_ref, v_ref, u_ref, o_ref, carry_sc, acc_sc):
    t = C_TILE
    qi = pl.program_id(2)
    carry_sc[...] = jnp.zeros(carry_sc.shape, F32)
    acc_sc[...] = jnp.zeros(acc_sc.shape, F32)
    q = q_ref[...]
    u = u_ref[...]

    def tile(kj, masked):
        ks = pl.ds(pl.multiple_of(kj * t, t), t)
        z = _dot_nt(q, k_ref[ks, :])
        sp = jnp.maximum(z, 0.0) + jnp.log(1.0 + jnp.exp(-jnp.abs(z)))
        log_beta = z - sp
        l1m = -sp
        if masked:
            row = lax.broadcasted_iota(jnp.int32, (t, t), 0)
            col = lax.broadcasted_iota(jnp.int32, (t, t), 1)
            past = col < row
            l1m = jnp.where(past, l1m, 0.0)
        hi = l1m.astype(BF16)
        lo = (l1m - hi.astype(F32)).astype(BF16)
        tail = (jnp.dot(hi, u, preferred_element_type=F32)
                + jnp.dot(lo, u, preferred_element_type=F32))
        a = jnp.exp(log_beta + tail + carry_sc[...])
        if masked:
            a = jnp.where(past, a, 0.0)
        acc_sc[...] += jnp.dot(a.astype(BF16), v_ref[ks, :], preferred_element_type=F32)
        carry_sc[...] += jnp.sum(l1m, axis=-1, keepdims=True)

    tile(qi, True)

    def body(i, carry):
        tile(qi - 1 - i, False)
        return carry

    lax.fori_loop(0, qi, body, 0)
    o_ref[...] = acc_sc[...]


def _attn_c(q, k, v, u):
    b, nh, s, d = q.shape
    t = C_TILE
    return pl.pallas_call(
        _attn_c_kernel,
        grid=(b, nh, s // t),
        in_specs=[
            pl.BlockSpec((None, None, t, d), lambda bi, hi, qi: (bi, hi, qi, 0)),
            pl.BlockSpec((None, None, s, d), lambda bi, hi, qi: (bi, hi, 0, 0)),
            pl.BlockSpec((None, None, s, d), lambda bi, hi, qi: (bi, hi, 0, 0)),
            pl.BlockSpec((t, t), lambda bi, hi, qi: (0, 0)),
        ],
        out_specs=pl.BlockSpec((None, None, t, d), lambda bi, hi, qi: (bi, hi, qi, 0)),
        out_shape=jax.ShapeDtypeStruct((b, nh, s, d), F32),
        scratch_shapes=[pltpu.VMEM((t, 1), F32), pltpu.VMEM((t, d), F32)],
        compiler_params=pltpu.CompilerParams(
            dimension_semantics=("parallel", "parallel", "arbitrary"),
            vmem_limit_bytes=VMEM_LIMIT_BYTES),
        name="attn_c",
    )(q, k, v, u)


def _outproj_kernel(y_ref, gate_ref, x_ref, w_ref, fg_ref, o_ref, *, final_norm):
    g = gate_ref[...]
    yg = (y_ref[...] * (g * jax.nn.sigmoid(g))).astype(BF16)
    r = x_ref[...] + jnp.dot(yg, w_ref[...], preferred_element_type=F32)
    if final_norm:
        ms = jnp.mean(r * r, axis=-1, keepdims=True)
        r = r * lax.rsqrt(ms + NORM_EPS) * fg_ref[...]
    o_ref[...] = r


def _outproj(y, gate, x2, w, fg, final_norm):
    m, d = x2.shape
    row = lambda i: (i, 0)
    fixed = lambda i: (0, 0)
    return pl.pallas_call(
        functools.partial(_outproj_kernel, final_norm=final_norm),
        grid=(m // ROW_TILE,),
        in_specs=[
            pl.BlockSpec((ROW_TILE, GATE_W), row),
            pl.BlockSpec((ROW_TILE, GATE_W), row),
            pl.BlockSpec((ROW_TILE, d), row),
            pl.BlockSpec((GATE_W, d), fixed),
            pl.BlockSpec((1, d), fixed),
        ],
        out_specs=pl.BlockSpec((ROW_TILE, d), row),
        out_shape=jax.ShapeDtypeStruct((m, d), F32),
        compiler_params=pltpu.CompilerParams(
            dimension_semantics=("parallel",), vmem_limit_bytes=VMEM_LIMIT_BYTES),
        name="outproj",
    )(y, gate, x2, w, fg)


def _permute_w_in(w):
    splits = (A_W, A_W, A_W, A_W, B_W, B_KV_W, B_KV_W, B_W, C_W, C_W, C_W, C_W)
    pts = np.cumsum(splits)[:-1]
    aq, ak, av, ag, bq, bk, bv, bg, cq, ck, cv, cg = jnp.split(w, pts, axis=-1)
    return jnp.concatenate([aq, ak, av, bq, bk, bv, cq, ck, cv, ag, bg, cg], axis=-1)


def _q_scale_row():
    cs = np.ones((1, QKV_W), np.float32)
    cs[0, :A_W] = A_QK_DIM ** -0.5
    off_b = 3 * A_W
    cs[0, off_b:off_b + B_W] = HEAD_DIM ** -0.5
    off_c = off_b + B_W + 2 * B_KV_W
    cs[0, off_c:off_c + C_W] = HEAD_DIM ** -0.5
    return jnp.asarray(cs)


def _heads_first(t, b, s, nh, d):
    return t.reshape(b, s, nh, d).transpose(0, 2, 1, 3)


def kernel(x, norm_g, w_in, lambda_q1, lambda_k1, lambda_q2, lambda_k2, subln_g, sinks, w_out, final_g):
    b, s, d = x.shape
    depth = w_in.shape[0]
    m = b * s
    slopes_a, slopes_b = _alibi_slopes()

    pos = jnp.arange(s, dtype=jnp.int32)
    p_hi = (pos // 256).astype(BF16)
    p_lo = (pos % 256).astype(BF16)
    kfeat = jnp.stack([p_hi] * 3 + [p_lo] * 3, axis=-1)
    kfeat = jnp.pad(kfeat, ((0, 0), (0, A_QK_DIM - 6)))
    qfeat = jnp.stack(
        [jnp.stack(_split3_bf16(256.0 * sl) + _split3_bf16(sl)) for sl in slopes_a])
    qfeat = jnp.pad(qfeat, ((0, 0), (0, A_QK_DIM - 6)))
    kfeat_full = jnp.broadcast_to(kfeat[None, None, None], (b, A_HEADS, 2, s, A_QK_DIM))
    qfeat_full = jnp.broadcast_to(qfeat[None, :, None, None], (b, A_HEADS, 2, s, A_QK_DIM))

    tri = (lax.broadcasted_iota(jnp.int32, (C_TILE, C_TILE), 0)
           > lax.broadcasted_iota(jnp.int32, (C_TILE, C_TILE), 1)).astype(BF16)
    cs = _q_scale_row()

    x2 = x.reshape(m, d)
    for l in range(depth):
        lambda_init = 0.8 - 0.6 * math.exp(-0.3 * l)
        w = _permute_w_in(w_in[l]).astype(BF16)
        qkv, gate = _inproj(x2, norm_g[l].reshape(1, d), w, cs)

        o = 0
        aq = qkv[:, o:o + A_W]; o += A_W
        ak = qkv[:, o:o + A_W]; o += A_W
        av = qkv[:, o:o + A_W]; o += A_W
        bq = qkv[:, o:o + B_W]; o += B_W
        bk = qkv[:, o:o + B_KV_W]; o += B_KV_W
        bv = qkv[:, o:o + B_KV_W]; o += B_KV_W
        cq = qkv[:, o:o + C_W]; o += C_W
        ck = qkv[:, o:o + C_W]; o += C_W
        cv = qkv[:, o:o + C_W]; o += C_W

        aq5 = aq.reshape(b, s, A_HEADS, 2, A_QK_DIM).transpose(0, 2, 3, 1, 4)
        ak5 = ak.reshape(b, s, A_HEADS, 2, A_QK_DIM).transpose(0, 2, 3, 1, 4)
        aq5 = jnp.concatenate([aq5, qfeat_full], axis=-1)
        ak5 = jnp.concatenate([ak5, kfeat_full], axis=-1)
        ya = _attn_a(aq5, ak5, _heads_first(av, b, s, A_HEADS, HEAD_DIM),
                     lambda_q1[l].reshape(1, -1), lambda_k1[l].reshape(1, -1),
                     lambda_q2[l].reshape(1, -1), lambda_k2[l].reshape(1, -1),
                     subln_g[l].reshape(1, -1), lambda_init)
        yb = _attn_b(_heads_first(bq, b, s, B_HEADS, HEAD_DIM),
                     _heads_first(bk, b, s, B_KV_HEADS, HEAD_DIM),
                     _heads_first(bv, b, s, B_KV_HEADS, HEAD_DIM),
                     sinks[l], tuple(float(v) for v in slopes_b))
        yc = _attn_c(_heads_first(cq, b, s, C_HEADS, HEAD_DIM),
                     _heads_first(ck, b, s, C_HEADS, HEAD_DIM),
                     _heads_first(cv, b, s, C_HEADS, HEAD_DIM), tri)

        y = jnp.concatenate([
            ya.transpose(0, 2, 1, 3).reshape(m, A_W),
            yb.transpose(0, 2, 1, 3).reshape(m, B_W),
            yc.transpose(0, 2, 1, 3).reshape(m, C_W)], axis=-1)
        x2 = _outproj(y, gate, x2, w_out[l].astype(BF16), final_g.reshape(1, d),
                      final_norm=(l == depth - 1))
    return x2.reshape(b, s, d)
```

```python
import functools
import math

import numpy as np
import jax
import jax.numpy as jnp
from jax import lax
from jax.experimental import pallas as pl
from jax.experimental.pallas import tpu as pltpu

F32 = jnp.float32
BF16 = jnp.bfloat16

HEAD_DIM = 64
NORM_EPS = 1e-5
A_HEADS = 4
A_QK_DIM = HEAD_DIM // 2
B_HEADS = 8
B_KV_HEADS = 2
B_GROUP = B_HEADS // B_KV_HEADS
C_HEADS = 4
WINDOW = 128
A_W = A_HEADS * HEAD_DIM
B_W = B_HEADS * HEAD_DIM
B_KV_W = B_KV_HEADS * HEAD_DIM
C_W = C_HEADS * HEAD_DIM
QKV_W = 3 * A_W + B_W + 2 * B_KV_W + 3 * C_W
GATE_W = A_W + B_W + C_W
MASK_VALUE = -1e30
LOG2_E = math.log2(math.e)
C_ZERO_WEIGHT_LOG = -105.0
VMEM_LIMIT_BYTES = 48 * 1024 * 1024

ROW_TILE = 256
A_TILE = 256
C_TILE = 256
B_TILE = WINDOW


def _alibi_slopes():
    n = A_HEADS + B_HEADS
    s = (2.0 ** (-8.0 * np.arange(1, n + 1) / n)).astype(np.float32)
    return s[B_HEADS:], s[:B_HEADS]


def _split3_bf16(v):
    out = []
    r = jnp.asarray(v, F32)
    for _ in range(3):
        p = r.astype(BF16)
        out.append(p)
        r = r - p.astype(F32)
    return out


def _dot_nt(a, b):
    return lax.dot_general(a, b, (((1,), (1,)), ((), ())), preferred_element_type=F32)


def _inproj_kernel(x_ref, g_ref, w_ref, cs_ref, qkv_ref, gate_ref):
    x = x_ref[...]
    ms = jnp.mean(x * x, axis=-1, keepdims=True)
    h = (x * lax.rsqrt(ms + NORM_EPS) * g_ref[...]).astype(BF16)
    qkv = jnp.dot(h, w_ref[:, :QKV_W], preferred_element_type=F32)
    qkv_ref[...] = (qkv * cs_ref[...]).astype(BF16)
    gate_ref[...] = jnp.dot(h, w_ref[:, QKV_W:], preferred_element_type=F32)


def _inproj(x2, g, w, cs):
    m, d = x2.shape
    n = w.shape[1]
    return pl.pallas_call(
        _inproj_kernel,
        grid=(m // ROW_TILE,),
        in_specs=[
            pl.BlockSpec((ROW_TILE, d), lambda i: (i, 0)),
            pl.BlockSpec((1, d), lambda i: (0, 0)),
            pl.BlockSpec((d, n), lambda i: (0, 0)),
            pl.BlockSpec((1, QKV_W), lambda i: (0, 0)),
        ],
        out_specs=[
            pl.BlockSpec((ROW_TILE, QKV_W), lambda i: (i, 0)),
            pl.BlockSpec((ROW_TILE, GATE_W), lambda i: (i, 0)),
        ],
        out_shape=[
            jax.ShapeDtypeStruct((m, QKV_W), BF16),
            jax.ShapeDtypeStruct((m, GATE_W), F32),
        ],
        compiler_params=pltpu.CompilerParams(
            dimension_semantics=("parallel",), vmem_limit_bytes=VMEM_LIMIT_BYTES),
        name="inproj",
    )(x2, g, w, cs)


def _attn_a_kernel(lq1_ref, lk1_ref, lq2_ref, lk2_ref, sg_ref, q_ref, k_ref, vt_ref, o_ref,
                   m_sc, l_sc, acc_sc, s_sc, p_sc, alpha_sc, *, lambda_init):
    t = A_TILE
    qi = pl.program_id(2)
    m_sc[...] = jnp.full(m_sc.shape, MASK_VALUE, F32)
    l_sc[...] = jnp.zeros(l_sc.shape, F32)
    acc_sc[...] = jnp.zeros(acc_sc.shape, F32)

    def scores(mp, kj):
        return _dot_nt(k_ref[mp, kj], q_ref[mp])

    def softmax_step(mp, s):
        m_old = m_sc[mp]
        m_new = jnp.maximum(m_old, jnp.max(s, axis=0, keepdims=True))
        alpha = jnp.exp2(m_old - m_new)
        p = jnp.exp2(s - m_new)
        l_sc[mp] = alpha * l_sc[mp] + jnp.sum(p, axis=0, keepdims=True)
        m_sc[mp] = m_new
        return alpha, p.astype(BF16)

    def value_step(mp, kj):
        acc_sc[mp] = alpha_sc[mp] * acc_sc[mp] + jnp.dot(vt_ref[kj], p_sc[mp],
                                                         preferred_element_type=F32)

    krow = lax.broadcasted_iota(jnp.int32, (t, t), 0)
    qcol = lax.broadcasted_iota(jnp.int32, (t, t), 1)
    for mp in range(2):
        s_diag = jnp.where(krow <= qcol, scores(mp, qi), MASK_VALUE)
        s_sc[mp] = scores(mp, 0)
        alpha, p = softmax_step(mp, s_diag)
        alpha_sc[mp] = alpha
        p_sc[mp] = p

    def body(j, carry):
        pending = jnp.where(j == 0, qi, j - 1)
        nxt = jnp.minimum(j + 1, qi - 1)
        for mp in range(2):
            value_step(mp, pending)
            alpha, p = softmax_step(mp, s_sc[mp])
            s_sc[mp] = scores(mp, nxt)
            alpha_sc[mp] = alpha
            p_sc[mp] = p
        return carry

    lax.fori_loop(0, qi, body, 0)
    last = jnp.where(qi == 0, 0, qi - 1)
    for mp in range(2):
        value_step(mp, last)

    lam =(jnp.exp(jnp.sum(lq1_ref[...] * lk1_ref[...], axis=-1, keepdims=True))
           - jnp.exp(jnp.sum(lq2_ref[...] * lk2_ref[...], axis=-1, keepdims=True))
           + lambda_init)
    o = acc_sc[0] / l_sc[0] - lam * (acc_sc[1] / l_sc[1])
    ms = jnp.mean(o * o, axis=0, keepdims=True)
    o_ref[...] = o * lax.rsqrt(ms + NORM_EPS) * sg_ref[...] * (1.0 - lambda_init)


def _attn_a(q, k, vt, lq1, lk1, lq2, lk2, sg, lambda_init):
    b, nh, _, s, dq = q.shape
    t = A_TILE
    nt = s // t
    small = lambda shape: pl.BlockSpec(shape, lambda bi, hi, qi: (0, 0))
    return pl.pallas_call(
        functools.partial(_attn_a_kernel, lambda_init=lambda_init),
        grid=(b, nh, nt),
        in_specs=[
            small((1, A_QK_DIM)), small((1, A_QK_DIM)), small((1, A_QK_DIM)), small((1, A_QK_DIM)),
            small((HEAD_DIM, 1)),
            pl.BlockSpec((None, None, 2, t, dq), lambda bi, hi, qi: (bi, hi, 0, qi, 0)),
            pl.BlockSpec((None, None, 2, nt, t, dq), lambda bi, hi, qi: (bi, hi, 0, 0, 0, 0)),
            pl.BlockSpec((None, None, nt, HEAD_DIM, t), lambda bi, hi, qi: (bi, hi, 0, 0, 0)),
        ],
        out_specs=pl.BlockSpec((None, None, HEAD_DIM, t), lambda bi, hi, qi: (bi, hi, 0, qi)),
        out_shape=jax.ShapeDtypeStruct((b, nh, HEAD_DIM, s), F32),
        scratch_shapes=[
            pltpu.VMEM((2, 1, t), F32),
            pltpu.VMEM((2, 1, t), F32),
            pltpu.VMEM((2, HEAD_DIM, t), F32),
            pltpu.VMEM((2, t, t), F32),
            pltpu.VMEM((2, t, t), BF16),
            pltpu.VMEM((2, 1, t), F32),
        ],
        compiler_params=pltpu.CompilerParams(
            dimension_semantics=("parallel", "parallel", "arbitrary"),
            vmem_limit_bytes=VMEM_LIMIT_BYTES),
        name="attn_a",
    )(lq1, lk1, lq2, lk2, sg, q, k, vt)


def _attn_b_kernel(sink_ref, q_ref, kp_ref, kc_ref, vp_ref, vc_ref, o_ref, *, slopes):
    t = B_TILE
    qi = pl.program_id(1)
    row = lax.broadcasted_iota(jnp.int32, (t, 2 * t), 0)
    col = lax.broadcasted_iota(jnp.int32, (t, 2 * t), 1) - t
    dist = row - col
    valid = (dist >= 0) & (dist < WINDOW) & (qi * t + col >= 0)
    distf = dist.astype(F32)
    for g in range(B_KV_HEADS):
        k = jnp.concatenate([kp_ref[g], kc_ref[g]], axis=0)
        v = jnp.concatenate([vp_ref[g], vc_ref[g]], axis=0)
        for j in range(B_GROUP):
            h = g * B_GROUP + j
            sink = sink_ref[h]
            s = _dot_nt(q_ref[h], k) - slopes[h] * distf
            s = jnp.where(valid, s, MASK_VALUE)
            m = jnp.maximum(jnp.max(s, axis=-1, keepdims=True), sink)
            p = jnp.exp(s - m)
            den = jnp.sum(p, axis=-1, keepdims=True) + jnp.exp(sink - m)
            o_ref[h] = jnp.dot(p.astype(BF16), v, preferred_element_type=F32) / den


def _attn_b(q, k, v, sinks, slopes):
    b, nh, s, d = q.shape
    t = B_TILE
    prev = lambda bi, qi: (bi, 0, jnp.maximum(qi - 1, 0), 0)
    cur = lambda bi, qi: (bi, 0, qi, 0)
    return pl.pallas_call(
        functools.partial(_attn_b_kernel, slopes=slopes),
        grid=(b, s // t),
        in_specs=[
            pl.BlockSpec(memory_space=pltpu.SMEM),
            pl.BlockSpec((None, nh, t, d), cur),
            pl.BlockSpec((None, B_KV_HEADS, t, d), prev),
            pl.BlockSpec((None, B_KV_HEADS, t, d), cur),
            pl.BlockSpec((None, B_KV_HEADS, t, d), prev),
            pl.BlockSpec((None, B_KV_HEADS, t, d), cur),
        ],
        out_specs=pl.BlockSpec((None, nh, t, d), cur),
        out_shape=jax.ShapeDtypeStruct((b, nh, s, d), F32),
        compiler_params=pltpu.CompilerParams(
            dimension_semantics=("parallel", "arbitrary"),
            vmem_limit_bytes=VMEM_LIMIT_BYTES),
        name="attn_b",
    )(sinks, q, k, k, v, v)


def _attn_c_kernel(q_ref, k_ref, vt_ref, u_ref, o_ref, carry_sc, acc_sc):
    t = C_TILE
    qi = pl.program_id(2)
    carry_sc[...] = jnp.zeros(carry_sc.shape, F32)
    acc_sc[...] = jnp.zeros(acc_sc.shape, F32)
    q = q_ref[...]
    u = u_ref[...]

    def tile(kj, masked):
        z = _dot_nt(k_ref[kj], q)
        sp = jnp.maximum(z, 0.0) + jnp.log(1.0 + jnp.exp(-jnp.abs(z)))
        log_beta = z - sp
        l1m = -sp
        if masked:
            krow = lax.broadcasted_iota(jnp.int32, (t, t), 0)
            qcol = lax.broadcasted_iota(jnp.int32, (t, t), 1)
            past = krow < qcol
            l1m = jnp.where(past, l1m, 0.0)
        hi = l1m.astype(BF16)
        lo = (l1m - hi.astype(F32)).astype(BF16)
        tail = (jnp.dot(u, hi, preferred_element_type=F32)
                + jnp.dot(u, lo, preferred_element_type=F32))
        a = jnp.exp(log_beta + tail + carry_sc[...])
        if masked:
            a = jnp.where(past, a, 0.0)
        acc_sc[...] += jnp.dot(vt_ref[kj], a.astype(BF16), preferred_element_type=F32)
        carry = carry_sc[...] + jnp.sum(l1m, axis=0, keepdims=True)
        carry_sc[...] = carry
        return jnp.max(carry) > C_ZERO_WEIGHT_LOG

    go0 = tile(qi, True)

    def cond(state):
        i, go = state
        return jnp.logical_and(i < qi, go)

    def body(state):
        i, _ = state
        return i + 1, tile(qi - 1 - i, False)

    lax.while_loop(cond, body, (jnp.int32(0), go0))
    o_ref[...] = acc_sc[...]


def _attn_c(q, k, vt, u):
    b, nh, s, d = q.shape
    t = C_TILE
    nt = s // t
    return pl.pallas_call(
        _attn_c_kernel,
        grid=(b, nh, nt),
        in_specs=[
            pl.BlockSpec((None, None, t, d), lambda bi, hi, qi: (bi, hi, qi, 0)),
            pl.BlockSpec((None, None, nt, t, d), lambda bi, hi, qi: (bi, hi, 0, 0, 0)),
            pl.BlockSpec((None, None, nt, d, t), lambda bi, hi, qi: (bi, hi, 0, 0, 0)),
            pl.BlockSpec((t, t), lambda bi, hi, qi: (0, 0)),
        ],
        out_specs=pl.BlockSpec((None, None, d, t), lambda bi, hi, qi: (bi, hi, 0, qi)),
        out_shape=jax.ShapeDtypeStruct((b, nh, d, s), F32),
        scratch_shapes=[pltpu.VMEM((1, t), F32), pltpu.VMEM((d, t), F32)],
        compiler_params=pltpu.CompilerParams(
            dimension_semantics=("parallel", "parallel", "arbitrary"),
            vmem_limit_bytes=VMEM_LIMIT_BYTES),
        name="attn_c",
    )(q, k, vt, u)


def _outproj_kernel(y_ref, gate_ref, x_ref, w_ref, fg_ref, o_ref, *, final_norm):
    g = gate_ref[...]
    yg = (y_ref[...] * (g * jax.nn.sigmoid(g))).astype(BF16)
    r = x_ref[...] + jnp.dot(yg, w_ref[...], preferred_element_type=F32)
    if final_norm:
        ms = jnp.mean(r * r, axis=-1, keepdims=True)
        r = r * lax.rsqrt(ms + NORM_EPS) * fg_ref[...]
    o_ref[...] = r


def _outproj(y, gate, x2, w, fg, final_norm):
    m, d = x2.shape
    row = lambda i: (i, 0)
    fixed = lambda i: (0, 0)
    return pl.pallas_call(
        functools.partial(_outproj_kernel, final_norm=final_norm),
        grid=(m // ROW_TILE,),
        in_specs=[
            pl.BlockSpec((ROW_TILE, GATE_W), row),
            pl.BlockSpec((ROW_TILE, GATE_W), row),
            pl.BlockSpec((ROW_TILE, d), row),
            pl.BlockSpec((GATE_W, d), fixed),
            pl.BlockSpec((1, d), fixed),
        ],
        out_specs=pl.BlockSpec((ROW_TILE, d), row),
        out_shape=jax.ShapeDtypeStruct((m, d), F32),
        compiler_params=pltpu.CompilerParams(
            dimension_semantics=("parallel",), vmem_limit_bytes=VMEM_LIMIT_BYTES),
        name="outproj",
    )(y, gate, x2, w, fg)


def _permute_w_in(w):
    splits = (A_W, A_W, A_W, A_W, B_W, B_KV_W, B_KV_W, B_W, C_W, C_W, C_W, C_W)
    pts = np.cumsum(splits)[:-1]
    aq, ak, av, ag, bq, bk, bv, bg, cq, ck, cv, cg = jnp.split(w, pts, axis=-1)
    return jnp.concatenate([aq, ak, av, bq, bk, bv, cq, ck, cv, ag, bg, cg], axis=-1)


def _q_scale_row():
    cs = np.ones((1, QKV_W), np.float32)
    cs[0, :A_W] = A_QK_DIM ** -0.5 * LOG2_E
    off_b = 3 * A_W
    cs[0, off_b:off_b + B_W] = HEAD_DIM ** -0.5
    off_c = off_b + B_W + 2 * B_KV_W
    cs[0, off_c:off_c + C_W] = HEAD_DIM ** -0.5
    return jnp.asarray(cs)


def _heads_first(t, b, s, nh, d):
    return t.reshape(b, s, nh, d).transpose(0, 2, 1, 3)


def _value_tiles_t(v, b, s, nh, d, t):
    return v.reshape(b, s // t, t, nh, d).transpose(0, 3, 1, 4, 2)


def kernel(x, norm_g, w_in, lambda_q1, lambda_k1, lambda_q2, lambda_k2, subln_g, sinks, w_out, final_g):
    b, s, d = x.shape
    depth = w_in.shape[0]
    m = b * s
    slopes_a, slopes_b = _alibi_slopes()
    ta, tc = A_TILE, C_TILE

    pos = jnp.arange(s, dtype=jnp.int32)
    p_hi = (pos // 256).astype(BF16)
    p_lo = (pos % 256).astype(BF16)
    kfeat = jnp.stack([p_hi] * 3 + [p_lo] * 3, axis=-1)
    kfeat = jnp.pad(kfeat, ((0, 0), (0, A_QK_DIM - 6)))
    qfeat = jnp.stack(
        [jnp.stack(_split3_bf16(256.0 * LOG2_E * sl) + _split3_bf16(LOG2_E * sl))
         for sl in slopes_a])
    qfeat = jnp.pad(qfeat, ((0, 0), (0, A_QK_DIM - 6)))
    kfeat_full = jnp.broadcast_to(kfeat[None, None, None], (b, A_HEADS, 2, s, A_QK_DIM))
    qfeat_full = jnp.broadcast_to(qfeat[None, :, None, None], (b, A_HEADS, 2, s, A_QK_DIM))

    tri = (lax.broadcasted_iota(jnp.int32, (tc, tc), 1)
           > lax.broadcasted_iota(jnp.int32, (tc, tc), 0)).astype(BF16)
    cs = _q_scale_row()

    x2 = x.reshape(m, d)
    for l in range(depth):
        lambda_init = 0.8 - 0.6 * math.exp(-0.3 * l)
        w = _permute_w_in(w_in[l]).astype(BF16)
        qkv, gate = _inproj(x2, norm_g[l].reshape(1, d), w, cs)

        o = 0
        aq = qkv[:, o:o + A_W]; o += A_W
        ak = qkv[:, o:o + A_W]; o += A_W
        av = qkv[:, o:o + A_W]; o += A_W
        bq = qkv[:, o:o + B_W]; o += B_W
        bk = qkv[:, o:o + B_KV_W]; o += B_KV_W
        bv = qkv[:, o:o + B_KV_W]; o += B_KV_W
        cq = qkv[:, o:o + C_W]; o += C_W
        ck = qkv[:, o:o + C_W]; o += C_W
        cv = qkv[:, o:o + C_W]; o += C_W

        aq5 = aq.reshape(b, s, A_HEADS, 2, A_QK_DIM).transpose(0, 2, 3, 1, 4)
        ak5 = ak.reshape(b, s, A_HEADS, 2, A_QK_DIM).transpose(0, 2, 3, 1, 4)
        aq5 = jnp.concatenate([aq5, qfeat_full], axis=-1)
        ak5 = jnp.concatenate([ak5, kfeat_full], axis=-1)
        ak6 = ak5.reshape(b, A_HEADS, 2, s // ta, ta, HEAD_DIM)
        ya = _attn_a(aq5, ak6, _value_tiles_t(av, b, s, A_HEADS, HEAD_DIM, ta),
                     lambda_q1[l].reshape(1, -1), lambda_k1[l].reshape(1, -1),
                     lambda_q2[l].reshape(1, -1), lambda_k2[l].reshape(1, -1),
                     subln_g[l].reshape(-1, 1), lambda_init)
        yb = _attn_b(_heads_first(bq, b, s, B_HEADS, HEAD_DIM),
                     _heads_first(bk, b, s, B_KV_HEADS, HEAD_DIM),
                     _heads_first(bv, b, s, B_KV_HEADS, HEAD_DIM),
                     sinks[l], tuple(float(v) for v in slopes_b))
        ck5 = _heads_first(ck, b, s, C_HEADS, HEAD_DIM).reshape(b, C_HEADS, s // tc, tc, HEAD_DIM)
        yc = _attn_c(_heads_first(cq, b, s, C_HEADS, HEAD_DIM), ck5,
                     _value_tiles_t(cv, b, s, C_HEADS, HEAD_DIM, tc), tri)

        y = jnp.concatenate([
            ya.transpose(0, 3, 1, 2).reshape(m, A_W),
            yb.transpose(0, 2, 1, 3).reshape(m, B_W),
            yc.transpose(0, 3, 1, 2).reshape(m, C_W)], axis=-1)
        x2 = _outproj(y, gate, x2, w_out[l].astype(BF16), final_g.reshape(1, d),
                      final_norm=(l == depth - 1))
    return x2.reshape(b, s, d)
```

```python
import functools
import math

import numpy as np
import jax
import jax.numpy as jnp
from jax import lax
from jax.experimental import pallas as pl
from jax.experimental.pallas import tpu as pltpu

F32 = jnp.float32
BF16 = jnp.bfloat16

HEAD_DIM = 64
NORM_EPS = 1e-5
A_HEADS = 4
A_QK_DIM = HEAD_DIM // 2
B_HEADS = 8
B_KV_HEADS = 2
B_GROUP = B_HEADS // B_KV_HEADS
C_HEADS = 4
WINDOW = 128
A_W = A_HEADS * HEAD_DIM
B_W = B_HEADS * HEAD_DIM
B_KV_W = B_KV_HEADS * HEAD_DIM
C_W = C_HEADS * HEAD_DIM
GATE_W = A_W + B_W + C_W
LANES = 128
MASK_VALUE = -1e30
LOG2_E = math.log2(math.e)
C_ZERO_WEIGHT_LOG = -105.0
VMEM_LIMIT_BYTES = 48 * 1024 * 1024

ROW_TILE = 256
A_TILE = 256
C_TILE = 256
B_TILE = WINDOW

OFF_AQ = 0
OFF_AK = OFF_AQ + A_W
OFF_BQ = OFF_AK + A_W
OFF_BK = OFF_BQ + B_W
OFF_CQ = OFF_BK + B_KV_W
OFF_CK = OFF_CQ + C_W
OFF_GATE = OFF_CK + C_W
RM_W = OFF_GATE + GATE_W
VT_W = A_W + B_KV_W + C_W
GATE_BLOCK_A = B_W // A_W
GATE_BLOCK_C = (B_W + A_W) // C_W


def _alibi_slopes():
    n = A_HEADS + B_HEADS
    s = (2.0 ** (-8.0 * np.arange(1, n + 1) / n)).astype(np.float32)
    return s[B_HEADS:], s[:B_HEADS]


def _split3_bf16(v):
    out = []
    r = np.float32(v)
    for _ in range(3):
        p = np.asarray(r, np.float32).astype(BF16).astype(np.float32)
        out.append(float(p))
        r = np.float32(r - p)
    return out


def _dot_nt(a, b):
    return lax.dot_general(a, b, (((1,), (1,)), ((), ())), preferred_element_type=F32)


def _silu(g):
    return g * jax.nn.sigmoid(g)


def _inproj_kernel(x_ref, g_ref, wrm_ref, wvt_ref, cs_ref,
                   aq_ref, ak_ref, bq_ref, bk_ref, cq_ref, ck_ref, gate_ref,
                   avt_ref, bvt_ref, cvt_ref):
    x = x_ref[...]
    ms = jnp.mean(x * x, axis=-1, keepdims=True)
    h = (x * lax.rsqrt(ms + NORM_EPS) * g_ref[...]).astype(BF16)
    qk = jnp.dot(h, wrm_ref[:, :OFF_GATE], preferred_element_type=F32) * cs_ref[...]
    aq_ref[...] = qk[:, OFF_AQ:OFF_AK].astype(BF16)
    ak_ref[...] = qk[:, OFF_AK:OFF_BQ].astype(BF16)
    bq_ref[...] = qk[:, OFF_BQ:OFF_BK].astype(BF16)
    bk_ref[...] = qk[:, OFF_BK:OFF_CQ].astype(BF16)
    cq_ref[...] = qk[:, OFF_CQ:OFF_CK].astype(BF16)
    ck_ref[...] = qk[:, OFF_CK:OFF_GATE].astype(BF16)
    gate_ref[...] = jnp.dot(h, wrm_ref[:, OFF_GATE:], preferred_element_type=F32)
    vt = _dot_nt(wvt_ref[...], h).astype(BF16)
    for c in range(ROW_TILE // A_TILE):
        avt_ref[c] = vt[:A_W, c * A_TILE:(c + 1) * A_TILE]
    for c in range(ROW_TILE // B_TILE):
        bvt_ref[c] = vt[A_W:A_W + B_KV_W, c * B_TILE:(c + 1) * B_TILE]
    for c in range(ROW_TILE // C_TILE):
        cvt_ref[c] = vt[A_W + B_KV_W:, c * C_TILE:(c + 1) * C_TILE]


def _inproj(x2, g, wrm, wvt, cs):
    m, d = x2.shape
    tm = ROW_TILE
    row = lambda i: (i, 0)
    fixed = lambda i: (0, 0)
    tiles = lambda i: (i, 0, 0)
    widths = (A_W, A_W, B_W, B_KV_W, C_W, C_W)
    return pl.pallas_call(
        _inproj_kernel,
        grid=(m // tm,),
        in_specs=[
            pl.BlockSpec((tm, d), row),
            pl.BlockSpec((1, d), fixed),
            pl.BlockSpec((d, RM_W), fixed),
            pl.BlockSpec((VT_W, d), fixed),
            pl.BlockSpec((1, OFF_GATE), fixed),
        ],
        out_specs=[pl.BlockSpec((tm, w), row) for w in widths] + [
            pl.BlockSpec((tm, GATE_W), row),
            pl.BlockSpec((tm // A_TILE, A_W, A_TILE), tiles),
            pl.BlockSpec((tm // B_TILE, B_KV_W, B_TILE), tiles),
            pl.BlockSpec((tm // C_TILE, C_W, C_TILE), tiles),
        ],
        out_shape=[jax.ShapeDtypeStruct((m, w), BF16) for w in widths] + [
            jax.ShapeDtypeStruct((m, GATE_W), F32),
            jax.ShapeDtypeStruct((m // A_TILE, A_W, A_TILE), BF16),
            jax.ShapeDtypeStruct((m // B_TILE, B_KV_W, B_TILE), BF16),
            jax.ShapeDtypeStruct((m // C_TILE, C_W, C_TILE), BF16),
        ],
        compiler_params=pltpu.CompilerParams(
            dimension_semantics=("parallel",), vmem_limit_bytes=VMEM_LIMIT_BYTES),
        name="inproj",
    )(x2, g, wrm, wvt, cs)


def _attn_a_kernel(lq1_ref, lk1_ref, lq2_ref, lk2_ref, sg_ref, qfeat_ref,
                   q_ref, k_ref, kfeat_ref, vt_ref, gate_ref, o_ref,
                   qz_sc, m_sc, l_sc, alpha_sc, acc_sc, s_sc, smax_sc, p_sc, *, lambda_init):
    t = A_TILE
    qi = pl.program_id(1)
    m_sc[...] = jnp.full(m_sc.shape, MASK_VALUE, F32)
    l_sc[...] = jnp.zeros(l_sc.shape, F32)
    acc_sc[...] = jnp.zeros(acc_sc.shape, F32)

    lane = lax.broadcasted_iota(jnp.int32, (t, LANES), 1)
    for pr in range(2):
        qp = q_ref[:, pr * LANES:(pr + 1) * LANES]
        for c in range(4):
            hh, mp = divmod(c, 2)
            lo = hh * HEAD_DIM + mp * A_QK_DIM
            real = jnp.where((lane >= lo) & (lane < lo + A_QK_DIM), qp, jnp.zeros_like(qp))
            h = 2 * pr + hh
            feat = jnp.broadcast_to(qfeat_ref[h:h + 1, :], (t, LANES)).astype(BF16)
            qz_sc[pr, c * t:(c + 1) * t, :] = jnp.concatenate([real, feat], axis=1)

    def scores(pr, kj):
        ks = pl.ds(pl.multiple_of(kj * t, t), t)
        lhs = jnp.concatenate([k_ref[ks, pr * LANES:(pr + 1) * LANES], kfeat_ref[ks, :]], axis=1)
        return _dot_nt(lhs, qz_sc[pr])

    def softmax_step(pr, s, smax):
        m_old = m_sc[pr]
        m_new = jnp.maximum(m_old, smax)
        alpha = jnp.exp2(m_old - m_new)
        p = jnp.exp2(s - m_new)
        l_sc[pr] = alpha * l_sc[pr] + jnp.sum(p, axis=0, keepdims=True)
        m_sc[pr] = m_new
        return alpha, p.astype(BF16)

    def value_step(pr, kj):
        for hh in range(2):
            h = 2 * pr + hh
            cols = slice(hh * 2 * t, (hh + 1) * 2 * t)
            pv = jnp.dot(vt_ref[kj, h * HEAD_DIM:(h + 1) * HEAD_DIM, :], p_sc[pr, :, cols],
                         preferred_element_type=F32)
            acc_sc[h] = alpha_sc[pr, :, cols] * acc_sc[h] + pv

    krow = lax.broadcasted_iota(jnp.int32, (t, 4 * t), 0)
    qcol = lax.broadcasted_iota(jnp.int32, (t, 4 * t), 1) & (t - 1)
    for pr in range(2):
        s_diag = jnp.where(krow <= qcol, scores(pr, qi), MASK_VALUE)
        s0 = scores(pr, 0)
        s_sc[pr] = s0
        smax_sc[pr] = jnp.max(s0, axis=0, keepdims=True)
        alpha, p = softmax_step(pr, s_diag, jnp.max(s_diag, axis=0, keepdims=True))
        alpha_sc[pr] = alpha
        p_sc[pr] = p

    def body(j, carry):
        pending = jnp.where(j == 0, qi, j - 1)
        nxt = jnp.minimum(j + 1, qi - 1)
        for pr in range(2):
            value_step(pr, pending)
            alpha, p = softmax_step(pr, s_sc[pr], smax_sc[pr])
            sn = scores(pr, nxt)
            s_sc[pr] = sn
            smax_sc[pr] = jnp.max(sn, axis=0, keepdims=True)
            alpha_sc[pr] = alpha
            p_sc[pr] = p
        return carry

    lax.fori_loop(0, qi, body, 0)
    last = jnp.where(qi == 0, 0, qi - 1)
    for pr in range(2):
        value_step(pr, last)

    lam = (jnp.exp(jnp.sum(lq1_ref[...] * lk1_ref[...], axis=-1, keepdims=True))
           - jnp.exp(jnp.sum(lq2_ref[...] * lk2_ref[...], axis=-1, keepdims=True))
           + lambda_init)
    outs = []
    for h in range(A_HEADS):
        pr, hh = divmod(h, 2)
        acc = acc_sc[h]
        l = l_sc[pr, :, hh * 2 * t:(hh + 1) * 2 * t]
        o = acc[:, :t] / l[:, :t] - lam * (acc[:, t:] / l[:, t:])
        ms = jnp.mean(o * o, axis=0, keepdims=True)
        outs.append(o * lax.rsqrt(ms + NORM_EPS))
    o_all = jnp.concatenate(outs, axis=0) * sg_ref[...] * (1.0 - lambda_init)
    o_ref[...] = (o_all.T * _silu(gate_ref[...])).astype(BF16)


def _attn_a(aq, ak, kfeat, avt, gate, qfeat, lq1, lk1, lq2, lk2, sg4, lambda_init, b, s):
    t = A_TILE
    nt = s // t
    small = lambda shape: pl.BlockSpec(shape, lambda bi, qi: (0, 0))
    return pl.pallas_call(
        functools.partial(_attn_a_kernel, lambda_init=lambda_init),
        grid=(b, nt),
        in_specs=[
            small((1, A_QK_DIM)), small((1, A_QK_DIM)), small((1, A_QK_DIM)), small((1, A_QK_DIM)),
            small((A_W, 1)),
            small((A_HEADS, LANES)),
            pl.BlockSpec((t, A_W), lambda bi, qi: (bi * nt + qi, 0)),
            pl.BlockSpec((s, A_W), lambda bi, qi: (bi, 0)),
            pl.BlockSpec((s, LANES), lambda bi, qi: (0, 0)),
            pl.BlockSpec((nt, A_W, t), lambda bi, qi: (bi, 0, 0)),
            pl.BlockSpec((t, A_W), lambda bi, qi: (bi * nt + qi, GATE_BLOCK_A)),
        ],
        out_specs=pl.BlockSpec((t, A_W), lambda bi, qi: (bi * nt + qi, 0)),
        out_shape=jax.ShapeDtypeStruct((b * s, A_W), BF16),
        scratch_shapes=[
            pltpu.VMEM((2, 4 * t, 2 * LANES), BF16),
            pltpu.VMEM((2, 1, 4 * t), F32),
            pltpu.VMEM((2, 1, 4 * t), F32),
            pltpu.VMEM((2, 1, 4 * t), F32),
            pltpu.VMEM((A_HEADS, HEAD_DIM, 2 * t), F32),
            pltpu.VMEM((2, t, 4 * t), F32),
            pltpu.VMEM((2, 1, 4 * t), F32),
            pltpu.VMEM((2, t, 4 * t), BF16),
        ],
        compiler_params=pltpu.CompilerParams(
            dimension_semantics=("parallel", "arbitrary"),
            vmem_limit_bytes=VMEM_LIMIT_BYTES),
        name="attn_a",
    )(lq1, lk1, lq2, lk2, sg4, qfeat, aq, ak, kfeat, avt, gate)


def _attn_b_kernel(sink_ref, q_ref, kp_ref, kc_ref, vtp_ref, vtc_ref, gate_ref, o_ref, *, slopes):
    t = B_TILE
    qi = pl.program_id(1)
    k_all = jnp.concatenate([kp_ref[...], kc_ref[...]], axis=0)
    vt_all = jnp.concatenate([vtp_ref[...], vtc_ref[...]], axis=1)
    koff = lax.broadcasted_iota(jnp.int32, (2 * t, t), 0) - t
    qoff = lax.broadcasted_iota(jnp.int32, (2 * t, t), 1)
    dist = qoff - koff
    valid = (dist >= 0) & (dist < WINDOW) & (qi * t + koff >= 0)
    distf = dist.astype(F32)
    lane = lax.broadcasted_iota(jnp.int32, (t, LANES), 1)
    outs = [None] * B_HEADS
    for j in range(B_GROUP):
        qp = q_ref[:, j * LANES:(j + 1) * LANES]
        for g in range(B_KV_HEADS):
            h = g * B_GROUP + j
            own = (lane < HEAD_DIM) if g == 0 else (lane >= HEAD_DIM)
            qz = jnp.where(own, qp, jnp.zeros_like(qp))
            sink = sink_ref[h]
            s = _dot_nt(k_all, qz) - slopes[h] * distf
            s = jnp.where(valid, s, MASK_VALUE)
            m = jnp.maximum(jnp.max(s, axis=0, keepdims=True), sink)
            p = jnp.exp(s - m)
            den = jnp.sum(p, axis=0, keepdims=True) + jnp.exp(sink - m)
            outs[h] = jnp.dot(vt_all[g * HEAD_DIM:(g + 1) * HEAD_DIM, :], p.astype(BF16),
                              preferred_element_type=F32) / den
    o_all = jnp.concatenate(outs, axis=0)
    o_ref[...] = (o_all.T * _silu(gate_ref[...])).astype(BF16)


def _attn_b(bq, bk, bvt, gate, sinks, slopes, b, s):
    t = B_TILE
    nt = s // t
    cur = lambda bi, qi: (bi * nt + qi, 0)
    prev = lambda bi, qi: (bi * nt + jnp.maximum(qi - 1, 0), 0)
    cur3 = lambda bi, qi: (bi * nt + qi, 0, 0)
    prev3 = lambda bi, qi: (bi * nt + jnp.maximum(qi - 1, 0), 0, 0)
    return pl.pallas_call(
        functools.partial(_attn_b_kernel, slopes=slopes),
        grid=(b, nt),
        in_specs=[
            pl.BlockSpec(memory_space=pltpu.SMEM),
            pl.BlockSpec((t, B_W), cur),
            pl.BlockSpec((t, B_KV_W), prev),
            pl.BlockSpec((t, B_KV_W), cur),
            pl.BlockSpec((None, B_KV_W, t), prev3),
            pl.BlockSpec((None, B_KV_W, t), cur3),
            pl.BlockSpec((t, B_W), cur),
        ],
        out_specs=pl.BlockSpec((t, B_W), cur),
        out_shape=jax.ShapeDtypeStruct((b * s, B_W), BF16),
        compiler_params=pltpu.CompilerParams(
            dimension_semantics=("parallel", "arbitrary"),
            vmem_limit_bytes=VMEM_LIMIT_BYTES),
        name="attn_b",
    )(sinks, bq, bk, bk, bvt, bvt, gate)


def _attn_c_kernel(q_ref, k_ref, vt_ref, u_ref, gate_ref, o_ref, qz_sc, carry_sc, acc_sc):
    t = C_TILE
    qi = pl.program_id(1)
    carry_sc[...] = jnp.zeros(carry_sc.shape, F32)
    acc_sc[...] = jnp.zeros(acc_sc.shape, F32)
    q = q_ref[...]
    lane = lax.broadcasted_iota(jnp.int32, (t, C_W), 1)
    for h in range(C_HEADS):
        own = (lane >= h * HEAD_DIM) & (lane < (h + 1) * HEAD_DIM)
        qz_sc[h * t:(h + 1) * t, :] = jnp.where(own, q, jnp.zeros_like(q))
    u = u_ref[...]

    def tile(kj, masked):
        ks = pl.ds(pl.multiple_of(kj * t, t), t)
        z = _dot_nt(k_ref[ks, :], qz_sc[...])
        sp = jnp.maximum(z, 0.0) + jnp.log(1.0 + jnp.exp(-jnp.abs(z)))
        log_beta = z - sp
        l1m = -sp
        if masked:
            krow = lax.broadcasted_iota(jnp.int32, (t, C_HEADS * t), 0)
            qcol = lax.broadcasted_iota(jnp.int32, (t, C_HEADS * t), 1) & (t - 1)
            past = krow < qcol
            l1m = jnp.where(past, l1m, 0.0)
        hi = l1m.astype(BF16)
        lo = (l1m - hi.astype(F32)).astype(BF16)
        tail = (jnp.dot(u, hi, preferred_element_type=F32)
                + jnp.dot(u, lo, preferred_element_type=F32))
        a = jnp.exp(log_beta + tail + carry_sc[...])
        if masked:
            a = jnp.where(past, a, 0.0)
        a = a.astype(BF16)
        for h in range(C_HEADS):
            rows = slice(h * HEAD_DIM, (h + 1) * HEAD_DIM)
            acc_sc[rows, :] += jnp.dot(vt_ref[kj, rows, :], a[:, h * t:(h + 1) * t],
                                       preferred_element_type=F32)
        carry = carry_sc[...] + jnp.sum(l1m, axis=0, keepdims=True)
        carry_sc[...] = carry
        return jnp.max(carry) > C_ZERO_WEIGHT_LOG

    go0 = tile(qi, True)

    def cond(state):
        i, go = state
        return jnp.logical_and(i < qi, go)

    def body(state):
        i, _ = state
        return i + 1, tile(qi - 1 - i, False)

    lax.while_loop(cond, body, (jnp.int32(0), go0))
    o_ref[...] = (acc_sc[...].T * _silu(gate_ref[...])).astype(BF16)


def _attn_c(cq, ck, cvt, u, gate, b, s):
    t = C_TILE
    nt = s // t
    return pl.pallas_call(
        _attn_c_kernel,
        grid=(b, nt),
        in_specs=[
            pl.BlockSpec((t, C_W), lambda bi, qi: (bi * nt + qi, 0)),
            pl.BlockSpec((s, C_W), lambda bi, qi: (bi, 0)),
            pl.BlockSpec((nt, C_W, t), lambda bi, qi: (bi, 0, 0)),
            pl.BlockSpec((t, t), lambda bi, qi: (0, 0)),
            pl.BlockSpec((t, C_W), lambda bi, qi: (bi * nt + qi, GATE_BLOCK_C)),
        ],
        out_specs=pl.BlockSpec((t, C_W), lambda bi, qi: (bi * nt + qi, 0)),
        out_shape=jax.ShapeDtypeStruct((b * s, C_W), BF16),
        scratch_shapes=[
            pltpu.VMEM((C_HEADS * t, C_W), BF16),
            pltpu.VMEM((1, C_HEADS * t), F32),
            pltpu.VMEM((C_W, t), F32),
        ],
        compiler_params=pltpu.CompilerParams(
            dimension_semantics=("parallel", "arbitrary"),
            vmem_limit_bytes=VMEM_LIMIT_BYTES),
        name="attn_c",
    )(cq, ck, cvt, u, gate)


def _outproj_kernel(ya_ref, yb_ref, yc_ref, x_ref, w_ref, fg_ref, o_ref, *, final_norm):
    y = jnp.concatenate([ya_ref[...], yb_ref[...], yc_ref[...]], axis=1)
    r = x_ref[...] + jnp.dot(y, w_ref[...], preferred_element_type=F32)
    if final_norm:
        ms = jnp.mean(r * r, axis=-1, keepdims=True)
        r = r * lax.rsqrt(ms + NORM_EPS) * fg_ref[...]
    o_ref[...] = r


def _outproj(ya, yb, yc, x2, w, fg, final_norm):
    m, d = x2.shape
    row = lambda i: (i, 0)
    fixed = lambda i: (0, 0)
    return pl.pallas_call(
        functools.partial(_outproj_kernel, final_norm=final_norm),
        grid=(m // ROW_TILE,),
        in_specs=[
            pl.BlockSpec((ROW_TILE, A_W), row),
            pl.BlockSpec((ROW_TILE, B_W), row),
            pl.BlockSpec((ROW_TILE, C_W), row),
            pl.BlockSpec((ROW_TILE, d), row),
            pl.BlockSpec((GATE_W, d), fixed),
            pl.BlockSpec((1, d), fixed),
        ],
        out_specs=pl.BlockSpec((ROW_TILE, d), row),
        out_shape=jax.ShapeDtypeStruct((m, d), F32),
        compiler_params=pltpu.CompilerParams(
            dimension_semantics=("parallel",), vmem_limit_bytes=VMEM_LIMIT_BYTES),
        name="outproj",
    )(ya, yb, yc, x2, w, fg)


def _split_w_in(w):
    splits = (A_W, A_W, A_W, A_W, B_W, B_KV_W, B_KV_W, B_W, C_W, C_W, C_W, C_W)
    pts = np.cumsum(splits)[:-1]
    aq, ak, av, ag, bq, bk, bv, bg, cq, ck, cv, cg = jnp.split(w, pts, axis=-1)
    d = w.shape[0]
    bq = bq.reshape(d, B_KV_HEADS, B_GROUP, HEAD_DIM).transpose(0, 2, 1, 3).reshape(d, B_W)
    wrm = jnp.concatenate([aq, ak, bq, bk, cq, ck, bg, ag, cg], axis=-1)
    wvt = jnp.concatenate([av, bv, cv], axis=-1).T
    return wrm.astype(BF16), wvt.astype(BF16)


def _q_scale_row():
    cs = np.ones((1, OFF_GATE), np.float32)
    cs[0, OFF_AQ:OFF_AK] = A_QK_DIM ** -0.5 * LOG2_E
    cs[0, OFF_BQ:OFF_BK] = HEAD_DIM ** -0.5
    cs[0, OFF_CQ:OFF_CK] = HEAD_DIM ** -0.5
    return jnp.asarray(cs)


def kernel(x, norm_g, w_in, lambda_q1, lambda_k1, lambda_q2, lambda_k2, subln_g, sinks, w_out, final_g):
    b, s, d = x.shape
    depth = w_in.shape[0]
    m = b * s
    slopes_a, slopes_b = _alibi_slopes()

    pos = jnp.arange(s, dtype=jnp.int32)
    p_hi = (pos // 256).astype(BF16)
    p_lo = (pos % 256).astype(BF16)
    kfeat = jnp.stack([p_hi] * 3 + [p_lo] * 3, axis=-1)
    kfeat = jnp.pad(kfeat, ((0, 0), (0, LANES - 6)))
    qfeat_np = np.zeros((A_HEADS, LANES), np.float32)
    for h, sl in enumerate(slopes_a):
        sl2 = np.float32(np.float32(sl) * np.float32(LOG2_E))
        qfeat_np[h, :6] = _split3_bf16(np.float32(256.0) * sl2) + _split3_bf16(sl2)
    qfeat = jnp.asarray(qfeat_np)

    tri = (lax.broadcasted_iota(jnp.int32, (C_TILE, C_TILE), 1)
           > lax.broadcasted_iota(jnp.int32, (C_TILE, C_TILE), 0)).astype(BF16)
    cs = _q_scale_row()

    x2 = x.reshape(m, d)
    for l in range(depth):
        lambda_init = 0.8 - 0.6 * math.exp(-0.3 * l)
        wrm, wvt = _split_w_in(w_in[l])
        aq, ak, bq, bk, cq, ck, gate, avt, bvt, cvt = _inproj(
            x2, norm_g[l].reshape(1, d), wrm, wvt, cs)
        ya = _attn_a(aq, ak, kfeat, avt, gate, qfeat,
                     lambda_q1[l].reshape(1, -1), lambda_k1[l].reshape(1, -1),
                     lambda_q2[l].reshape(1, -1), lambda_k2[l].reshape(1, -1),
                     jnp.tile(subln_g[l], A_HEADS).reshape(-1, 1), lambda_init, b, s)
        yb = _attn_b(bq, bk, bvt, gate, sinks[l], tuple(float(v) for v in slopes_b), b, s)
        yc = _attn_c(cq, ck, cvt, tri, gate, b, s)
        x2 = _outproj(ya, yb, yc, x2, w_out[l].astype(BF16), final_g.reshape(1, d),
                      final_norm=(l == depth - 1))
    return x2.reshape(b, s, d)
```

```python
import functools
import math

import numpy as np
import jax
import jax.numpy as jnp
from jax import lax
from jax.experimental import pallas as pl
from jax.experimental.pallas import tpu as pltpu

F32 = jnp.float32
BF16 = jnp.bfloat16

HEAD_DIM = 64
NORM_EPS = 1e-5
A_HEADS = 4
A_QK_DIM = HEAD_DIM // 2
B_HEADS = 8
B_KV_HEADS = 2
B_GROUP = B_HEADS // B_KV_HEADS
C_HEADS = 4
WINDOW = 128
A_W = A_HEADS * HEAD_DIM
B_W = B_HEADS * HEAD_DIM
B_KV_W = B_KV_HEADS * HEAD_DIM
C_W = C_HEADS * HEAD_DIM
GATE_W = A_W + B_W + C_W
LANES = 128
MASK_VALUE = -1e30
LOG2_E = math.log2(math.e)
C_ZERO_WEIGHT_LOG2 = -105.0 * LOG2_E
VMEM_LIMIT_BYTES = 48 * 1024 * 1024

ROW_TILE = 256
A_TILE = 256
A_KEYS = 2 * A_TILE
C_TILE = 256
B_TILE = WINDOW

OFF_AK = 0
OFF_BQ = OFF_AK + A_W
OFF_BK = OFF_BQ + B_W
OFF_CQ = OFF_BK + B_KV_W
OFF_CK = OFF_CQ + C_W
OFF_GATE = OFF_CK + C_W
RM_W = OFF_GATE + GATE_W
TOFF_AV = 0
TOFF_BV = TOFF_AV + A_W
TOFF_CV = TOFF_BV + B_KV_W
TOFF_AQ = TOFF_CV + C_W
T_W = TOFF_AQ + A_W
GATE_BLOCK_A = B_W // A_W
GATE_BLOCK_C = (B_W + A_W) // C_W


def _alibi_slopes():
    n = A_HEADS + B_HEADS
    s = (2.0 ** (-8.0 * np.arange(1, n + 1) / n)).astype(np.float32)
    return s[B_HEADS:], s[:B_HEADS]


def _split3_bf16(v):
    out = []
    r = np.float32(v)
    for _ in range(3):
        p = np.asarray(r, np.float32).astype(BF16).astype(np.float32)
        out.append(float(p))
        r = np.float32(r - p)
    return out


def _dot_nt(a, b):
    return lax.dot_general(a, b, (((1,), (1,)), ((), ())), preferred_element_type=F32)


def _silu(g):
    return g * jax.nn.sigmoid(g)


def _inproj_kernel(x_ref, g_ref, wrm_ref, wt_ref, cs_ref, ts_ref,
                   ak_ref, bq_ref, bk_ref, cq_ref, ck_ref, gate_ref,
                   avt_ref, bvt_ref, cvt_ref, aqt_ref):
    x = x_ref[...]
    ms = jnp.mean(x * x, axis=-1, keepdims=True)
    h = (x * lax.rsqrt(ms + NORM_EPS) * g_ref[...]).astype(BF16)
    qk = jnp.dot(h, wrm_ref[:, :OFF_GATE], preferred_element_type=F32) * cs_ref[...]
    ak_ref[...] = qk[:, OFF_AK:OFF_BQ].astype(BF16)
    bq_ref[...] = qk[:, OFF_BQ:OFF_BK].astype(BF16)
    bk_ref[...] = qk[:, OFF_BK:OFF_CQ].astype(BF16)
    cq_ref[...] = qk[:, OFF_CQ:OFF_CK].astype(BF16)
    ck_ref[...] = qk[:, OFF_CK:OFF_GATE].astype(BF16)
    gate_ref[...] = jnp.dot(h, wrm_ref[:, OFF_GATE:], preferred_element_type=F32)
    tr = (_dot_nt(wt_ref[...], h) * ts_ref[...]).astype(BF16)
    for c in range(ROW_TILE // A_TILE):
        avt_ref[c] = tr[TOFF_AV:TOFF_BV, c * A_TILE:(c + 1) * A_TILE]
        aqt_ref[c] = tr[TOFF_AQ:, c * A_TILE:(c + 1) * A_TILE]
    for c in range(ROW_TILE // B_TILE):
        bvt_ref[c] = tr[TOFF_BV:TOFF_CV, c * B_TILE:(c + 1) * B_TILE]
    for c in range(ROW_TILE // C_TILE):
        cvt_ref[c] = tr[TOFF_CV:TOFF_AQ, c * C_TILE:(c + 1) * C_TILE]


def _inproj(x2, g, wrm, wt, cs, ts):
    m, d = x2.shape
    tm = ROW_TILE
    row = lambda i: (i, 0)
    fixed = lambda i: (0, 0)
    tiles = lambda i: (i, 0, 0)
    widths = (A_W, B_W, B_KV_W, C_W, C_W)
    return pl.pallas_call(
        _inproj_kernel,
        grid=(m // tm,),
        in_specs=[
            pl.BlockSpec((tm, d), row),
            pl.BlockSpec((1, d), fixed),
            pl.BlockSpec((d, RM_W), fixed),
            pl.BlockSpec((T_W, d), fixed),
            pl.BlockSpec((1, OFF_GATE), fixed),
            pl.BlockSpec((T_W, 1), fixed),
        ],
        out_specs=[pl.BlockSpec((tm, w), row) for w in widths] + [
            pl.BlockSpec((tm, GATE_W), row),
            pl.BlockSpec((tm // A_TILE, A_W, A_TILE), tiles),
            pl.BlockSpec((tm // B_TILE, B_KV_W, B_TILE), tiles),
            pl.BlockSpec((tm // C_TILE, C_W, C_TILE), tiles),
            pl.BlockSpec((tm // A_TILE, A_W, A_TILE), tiles),
        ],
        out_shape=[jax.ShapeDtypeStruct((m, w), BF16) for w in widths] + [
            jax.ShapeDtypeStruct((m, GATE_W), F32),
            jax.ShapeDtypeStruct((m // A_TILE, A_W, A_TILE), BF16),
            jax.ShapeDtypeStruct((m // B_TILE, B_KV_W, B_TILE), BF16),
            jax.ShapeDtypeStruct((m // C_TILE, C_W, C_TILE), BF16),
            jax.ShapeDtypeStruct((m // A_TILE, A_W, A_TILE), BF16),
        ],
        compiler_params=pltpu.CompilerParams(
            dimension_semantics=("parallel",), vmem_limit_bytes=VMEM_LIMIT_BYTES),
        name="inproj",
    )(x2, g, wrm, wt, cs, ts)


def _attn_a_kernel(lq1_ref, lk1_ref, lq2_ref, lk2_ref, sg_ref, qfeat_ref,
                   qt_ref, k_ref, kfeat_ref, vt_ref, gate_ref, o_ref,
                   qz_sc, m_sc, l_sc, alpha_sc, acc_sc, s_sc, smax_sc, p_sc, *, lambda_init):
    t = A_TILE
    tk = A_KEYS
    qi = pl.program_id(1)
    n_full = qi // 2
    m_sc[...] = jnp.full(m_sc.shape, MASK_VALUE, F32)
    l_sc[...] = jnp.zeros(l_sc.shape, F32)
    acc_sc[...] = jnp.zeros(acc_sc.shape, F32)

    sub = lax.broadcasted_iota(jnp.int32, (LANES, t), 0)
    for pr in range(2):
        qp = qt_ref[pr * LANES:(pr + 1) * LANES, :]
        for c in range(4):
            hh, mp = divmod(c, 2)
            lo = hh * HEAD_DIM + mp * A_QK_DIM
            real = jnp.where((sub >= lo) & (sub < lo + A_QK_DIM), qp, jnp.zeros_like(qp))
            h = 2 * pr + hh
            feat = jnp.broadcast_to(qfeat_ref[:, h:h + 1], (LANES, t)).astype(BF16)
            qz_sc[pr, :, c * t:(c + 1) * t] = jnp.concatenate([real, feat], axis=0)

    def scores(pr, step):
        ks = pl.ds(pl.multiple_of(step * tk, tk), tk)
        lhs = jnp.concatenate([k_ref[ks, pr * LANES:(pr + 1) * LANES], kfeat_ref[ks, :]], axis=1)
        return jnp.dot(lhs, qz_sc[pr], preferred_element_type=F32)

    def softmax_step(pr, s, smax):
        m_old = m_sc[pr]
        m_new = jnp.maximum(m_old, smax)
        alpha = jnp.exp2(m_old - m_new)
        p = jnp.exp2(s - m_new)
        l_sc[pr] = alpha * l_sc[pr] + jnp.sum(p, axis=0, keepdims=True)
        m_sc[pr] = m_new
        return alpha, p.astype(BF16)

    def value_step(pr, step):
        for hh in range(2):
            h = 2 * pr + hh
            rows = slice(h * HEAD_DIM, (h + 1) * HEAD_DIM)
            cols = slice(hh * 2 * t, (hh + 1) * 2 * t)
            vt = jnp.concatenate([vt_ref[2 * step, rows, :], vt_ref[2 * step + 1, rows, :]], axis=1)
            pv = jnp.dot(vt, p_sc[pr, :, cols], preferred_element_type=F32)
            acc_sc[h] = alpha_sc[pr, :, cols] * acc_sc[h] + pv

    def causal_mask():
        kmq = (lax.broadcasted_iota(jnp.int32, (tk, 4 * t), 0)
               - (lax.broadcasted_iota(jnp.int32, (tk, 4 * t), 1) & (t - 1)))
        return kmq <= qi * t - n_full * tk

    def stage_scores(pr, s):
        s_sc[pr] = s
        smax_sc[pr] = jnp.max(s, axis=0, keepdims=True)

    def trip(i, masked_next, has_next):
        for pr in range(2):
            value_step(pr, jnp.maximum(i - 1, 0))
            alpha, p = softmax_step(pr, s_sc[pr], smax_sc[pr])
            if has_next:
                sn = scores(pr, i + 1)
                if masked_next:
                    sn = jnp.where(causal_mask(), sn, MASK_VALUE)
                stage_scores(pr, sn)
            alpha_sc[pr] = alpha
            p_sc[pr] = p

    p_sc[...] = jnp.zeros(p_sc.shape, BF16)
    alpha_sc[...] = jnp.ones(alpha_sc.shape, F32)
    keep = jnp.logical_or(causal_mask(), n_full > 0)
    for pr in range(2):
        stage_scores(pr, jnp.where(keep, scores(pr, 0), MASK_VALUE))

    def body(i, carry):
        trip(i, False, True)
        return carry

    lax.fori_loop(0, n_full - 1, body, 0)

    @pl.when(n_full > 0)
    def _():
        trip(n_full - 1, True, True)

    trip(n_full, False, False)
    for pr in range(2):
        value_step(pr, n_full)

    lam = (jnp.exp(jnp.sum(lq1_ref[...] * lk1_ref[...], axis=-1, keepdims=True))
           - jnp.exp(jnp.sum(lq2_ref[...] * lk2_ref[...], axis=-1, keepdims=True))
           + lambda_init)
    outs = []
    for h in range(A_HEADS):
        pr, hh = divmod(h, 2)
        acc = acc_sc[h]
        l = l_sc[pr, :, hh * 2 * t:(hh + 1) * 2 * t]
        o = acc[:, :t] / l[:, :t] - lam * (acc[:, t:] / l[:, t:])
        ms = jnp.mean(o * o, axis=0, keepdims=True)
        outs.append(o * lax.rsqrt(ms + NORM_EPS))
    o_all = jnp.concatenate(outs, axis=0) * sg_ref[...] * (1.0 - lambda_init)
    o_ref[...] = (o_all.T * _silu(gate_ref[...])).astype(BF16)


def _attn_a(aqt, ak, kfeat, avt, gate, qfeat, lq1, lk1, lq2, lk2, sg4, lambda_init, b, s):
    t = A_TILE
    tk = A_KEYS
    nt = s // t
    small = lambda shape: pl.BlockSpec(shape, lambda bi, qi: (0, 0))
    return pl.pallas_call(
        functools.partial(_attn_a_kernel, lambda_init=lambda_init),
        grid=(b, nt),
        in_specs=[
            small((1, A_QK_DIM)), small((1, A_QK_DIM)), small((1, A_QK_DIM)), small((1, A_QK_DIM)),
            small((A_W, 1)),
            small((LANES, A_HEADS)),
            pl.BlockSpec((None, A_W, t), lambda bi, qi: (bi * nt + qi, 0, 0)),
            pl.BlockSpec((s, A_W), lambda bi, qi: (bi, 0)),
            pl.BlockSpec((s, LANES), lambda bi, qi: (0, 0)),
            pl.BlockSpec((nt, A_W, t), lambda bi, qi: (bi, 0, 0)),
            pl.BlockSpec((t, A_W), lambda bi, qi: (bi * nt + qi, GATE_BLOCK_A)),
        ],
        out_specs=pl.BlockSpec((t, A_W), lambda bi, qi: (bi * nt + qi, 0)),
        out_shape=jax.ShapeDtypeStruct((b * s, A_W), BF16),
        scratch_shapes=[
            pltpu.VMEM((2, 2 * LANES, 4 * t), BF16),
            pltpu.VMEM((2, 1, 4 * t), F32),
            pltpu.VMEM((2, 1, 4 * t), F32),
            pltpu.VMEM((2, 1, 4 * t), F32),
            pltpu.VMEM((A_HEADS, HEAD_DIM, 2 * t), F32),
            pltpu.VMEM((2, tk, 4 * t), F32),
            pltpu.VMEM((2, 1, 4 * t), F32),
            pltpu.VMEM((2, tk, 4 * t), BF16),
        ],
        compiler_params=pltpu.CompilerParams(
            dimension_semantics=("parallel", "arbitrary"),
            vmem_limit_bytes=VMEM_LIMIT_BYTES),
        name="attn_a",
    )(lq1, lk1, lq2, lk2, sg4, qfeat, aqt, ak, kfeat, avt, gate)


def _attn_b_kernel(q_ref, kp_ref, kc_ref, vtp_ref, vtc_ref, gate_ref, bias_ref, sink_ref, o_ref):
    t = B_TILE
    qi = pl.program_id(1)
    k_all = jnp.concatenate([kp_ref[...], kc_ref[...]], axis=0)
    vt_all = jnp.concatenate([vtp_ref[...], vtc_ref[...]], axis=1)
    lane = lax.broadcasted_iota(jnp.int32, (t, LANES), 1)
    pieces = [None] * B_HEADS
    for j in range(B_GROUP):
        qp = q_ref[:, j * LANES:(j + 1) * LANES]
        zero = jnp.zeros_like(qp)
        pieces[j] = jnp.where(lane < HEAD_DIM, qp, zero)
        pieces[B_GROUP + j] = jnp.where(lane >= HEAD_DIM, qp, zero)
    qz = jnp.concatenate(pieces, axis=0)
    s = _dot_nt(k_all, qz) + bias_ref[jnp.minimum(qi, 1)]
    sink = sink_ref[...]
    m = jnp.maximum(jnp.max(s, axis=0, keepdims=True), sink)
    p = jnp.exp(s - m)
    inv = 1.0 / (jnp.sum(p, axis=0, keepdims=True) + jnp.exp(sink - m))
    pb = p.astype(BF16)
    outs = []
    for g in range(B_KV_HEADS):
        cols = slice(g * B_GROUP * t, (g + 1) * B_GROUP * t)
        og = jnp.dot(vt_all[g * HEAD_DIM:(g + 1) * HEAD_DIM, :], pb[:, cols],
                     preferred_element_type=F32) * inv[:, cols]
        outs.extend(og[:, j * t:(j + 1) * t] for j in range(B_GROUP))
    o_all = jnp.concatenate(outs, axis=0)
    o_ref[...] = (o_all.T * _silu(gate_ref[...])).astype(BF16)


def _attn_b(bq, bk, bvt, gate, bias, sink_row, b, s):
    t = B_TILE
    nt = s // t
    cur = lambda bi, qi: (bi * nt + qi, 0)
    prev = lambda bi, qi: (bi * nt + jnp.maximum(qi - 1, 0), 0)
    cur3 = lambda bi, qi: (bi * nt + qi, 0, 0)
    prev3 = lambda bi, qi: (bi * nt + jnp.maximum(qi - 1, 0), 0, 0)
    return pl.pallas_call(
        _attn_b_kernel,
        grid=(b, nt),
        in_specs=[
            pl.BlockSpec((t, B_W), cur),
            pl.BlockSpec((t, B_KV_W), prev),
            pl.BlockSpec((t, B_KV_W), cur),
            pl.BlockSpec((None, B_KV_W, t), prev3),
            pl.BlockSpec((None, B_KV_W, t), cur3),
            pl.BlockSpec((t, B_W), cur),
            pl.BlockSpec((2, 2 * t, B_HEADS * t), lambda bi, qi: (0, 0, 0)),
            pl.BlockSpec((1, B_HEADS * t), lambda bi, qi: (0, 0)),
        ],
        out_specs=pl.BlockSpec((t, B_W), cur),
        out_shape=jax.ShapeDtypeStruct((b * s, B_W), BF16),
        compiler_params=pltpu.CompilerParams(
            dimension_semantics=("parallel", "arbitrary"),
            vmem_limit_bytes=VMEM_LIMIT_BYTES),
        name="attn_b",
    )(bq, bk, bk, bvt, bvt, gate, bias, sink_row)


def _b_bias(slopes):
    t = B_TILE
    koff = np.arange(2 * t)[:, None] - t
    qoff = np.arange(t)[None, :]
    dist = qoff - koff
    band = (dist >= 0) & (dist < WINDOW)
    out = np.empty((2, 2 * t, B_HEADS * t), np.float32)
    for h in range(B_HEADS):
        term = (-np.float32(slopes[h]) * dist.astype(np.float32)).astype(np.float32)
        out[1, :, h * t:(h + 1) * t] = np.where(band, term, np.float32(MASK_VALUE))
        out[0, :, h * t:(h + 1) * t] = np.where(band & (koff >= 0), term, np.float32(MASK_VALUE))
    return jnp.asarray(out)


def _attn_c_kernel(q_ref, k_ref, vt_ref, u_ref, gate_ref, o_ref, qz_sc, carry_sc, acc_sc):
    t = C_TILE
    qi = pl.program_id(1)
    carry_sc[...] = jnp.zeros(carry_sc.shape, F32)
    acc_sc[...] = jnp.zeros(acc_sc.shape, F32)
    q = q_ref[...]
    lane = lax.broadcasted_iota(jnp.int32, (t, C_W), 1)
    for h in range(C_HEADS):
        own = (lane >= h * HEAD_DIM) & (lane < (h + 1) * HEAD_DIM)
        qz_sc[h * t:(h + 1) * t, :] = jnp.where(own, q, jnp.zeros_like(q))
    u2 = u_ref[...]

    def tile(kj, masked):
        ks = pl.ds(pl.multiple_of(kj * t, t), t)
        z = _dot_nt(k_ref[ks, :], qz_sc[...])
        lse = jnp.log2(1.0 + jnp.exp2(jnp.minimum(z, -z)))
        log_beta = jnp.minimum(z, 0.0) - lse
        l1m = log_beta - z
        if masked:
            krow = lax.broadcasted_iota(jnp.int32, (t, C_HEADS * t), 0)
            qcol = lax.broadcasted_iota(jnp.int32, (t, C_HEADS * t), 1) & (t - 1)
            past = krow < qcol
            l1m = jnp.where(past, l1m, 0.0)
        hi = l1m.astype(BF16)
        lo = (l1m - hi.astype(F32)).astype(BF16)
        tail = jnp.dot(u2, jnp.concatenate([hi, lo], axis=0),
                       preferred_element_type=F32)
        a = jnp.exp2(log_beta + tail + carry_sc[...])
        if masked:
            a = jnp.where(past, a, 0.0)
        a = a.astype(BF16)
        for h in range(C_HEADS):
            rows = slice(h * HEAD_DIM, (h + 1) * HEAD_DIM)
            acc_sc[rows, :] += jnp.dot(vt_ref[kj, rows, :], a[:, h * t:(h + 1) * t],
                                       preferred_element_type=F32)
        carry = carry_sc[...] + jnp.sum(l1m, axis=0, keepdims=True)
        carry_sc[...] = carry
        return jnp.max(carry) > C_ZERO_WEIGHT_LOG2

    go0 = tile(qi, True)

    def cond(state):
        i, go = state
        return jnp.logical_and(i < qi, go)

    def body(state):
        i, _ = state
        return i + 1, tile(qi - 1 - i, False)

    lax.while_loop(cond, body, (jnp.int32(0), go0))
    o_ref[...] = (acc_sc[...].T * _silu(gate_ref[...])).astype(BF16)


def _attn_c(cq, ck, cvt, u2, gate, b, s):
    t = C_TILE
    nt = s // t
    return pl.pallas_call(
        _attn_c_kernel,
        grid=(b, nt),
        in_specs=[
            pl.BlockSpec((t, C_W), lambda bi, qi: (bi * nt + qi, 0)),
            pl.BlockSpec((s, C_W), lambda bi, qi: (bi, 0)),
            pl.BlockSpec((nt, C_W, t), lambda bi, qi: (bi, 0, 0)),
            pl.BlockSpec((t, 2 * t), lambda bi, qi: (0, 0)),
            pl.BlockSpec((t, C_W), lambda bi, qi: (bi * nt + qi, GATE_BLOCK_C)),
        ],
        out_specs=pl.BlockSpec((t, C_W), lambda bi, qi: (bi * nt + qi, 0)),
        out_shape=jax.ShapeDtypeStruct((b * s, C_W), BF16),
        scratch_shapes=[
            pltpu.VMEM((C_HEADS * t, C_W), BF16),
            pltpu.VMEM((1, C_HEADS * t), F32),
            pltpu.VMEM((C_W, t), F32),
        ],
        compiler_params=pltpu.CompilerParams(
            dimension_semantics=("parallel", "arbitrary"),
            vmem_limit_bytes=VMEM_LIMIT_BYTES),
        name="attn_c",
    )(cq, ck, cvt, u2, gate)


def _outproj_kernel(ya_ref, yb_ref, yc_ref, x_ref, w_ref, fg_ref, o_ref, *, final_norm):
    y = jnp.concatenate([ya_ref[...], yb_ref[...], yc_ref[...]], axis=1)
    r = x_ref[...] + jnp.dot(y, w_ref[...], preferred_element_type=F32)
    if final_norm:
        ms = jnp.mean(r * r, axis=-1, keepdims=True)
        r = r * lax.rsqrt(ms + NORM_EPS) * fg_ref[...]
    o_ref[...] = r


def _outproj(ya, yb, yc, x2, w, fg, final_norm):
    m, d = x2.shape
    row = lambda i: (i, 0)
    fixed = lambda i: (0, 0)
    return pl.pallas_call(
        functools.partial(_outproj_kernel, final_norm=final_norm),
        grid=(m // ROW_TILE,),
        in_specs=[
            pl.BlockSpec((ROW_TILE, A_W), row),
            pl.BlockSpec((ROW_TILE, B_W), row),
            pl.BlockSpec((ROW_TILE, C_W), row),
            pl.BlockSpec((ROW_TILE, d), row),
            pl.BlockSpec((GATE_W, d), fixed),
            pl.BlockSpec((1, d), fixed),
        ],
        out_specs=pl.BlockSpec((ROW_TILE, d), row),
        out_shape=jax.ShapeDtypeStruct((m, d), F32),
        compiler_params=pltpu.CompilerParams(
            dimension_semantics=("parallel",), vmem_limit_bytes=VMEM_LIMIT_BYTES),
        name="outproj",
    )(ya, yb, yc, x2, w, fg)


def _split_w_in(w):
    splits = (A_W, A_W, A_W, A_W, B_W, B_KV_W, B_KV_W, B_W, C_W, C_W, C_W, C_W)
    pts = np.cumsum(splits)[:-1]
    aq, ak, av, ag, bq, bk, bv, bg, cq, ck, cv, cg = jnp.split(w, pts, axis=-1)
    d = w.shape[0]
    bq = bq.reshape(d, B_KV_HEADS, B_GROUP, HEAD_DIM).transpose(0, 2, 1, 3).reshape(d, B_W)
    wrm = jnp.concatenate([ak, bq, bk, cq, ck, bg, ag, cg], axis=-1)
    wt = jnp.concatenate([av, bv, cv, aq], axis=-1).T
    return wrm.astype(BF16), wt.astype(BF16)


def _q_scales():
    cs = np.ones((1, OFF_GATE), np.float32)
    cs[0, OFF_BQ:OFF_BK] = HEAD_DIM ** -0.5
    cs[0, OFF_CQ:OFF_CK] = HEAD_DIM ** -0.5 * LOG2_E
    ts = np.ones((T_W, 1), np.float32)
    ts[TOFF_AQ:, 0] = A_QK_DIM ** -0.5 * LOG2_E
    return jnp.asarray(cs), jnp.asarray(ts)


def kernel(x, norm_g, w_in, lambda_q1, lambda_k1, lambda_q2, lambda_k2, subln_g, sinks, w_out, final_g):
    b, s, d = x.shape
    depth = w_in.shape[0]
    m = b * s
    slopes_a, slopes_b = _alibi_slopes()

    pos = jnp.arange(s, dtype=jnp.int32)
    p_hi = (pos // 256).astype(BF16)
    p_lo = (pos % 256).astype(BF16)
    kfeat = jnp.stack([p_hi] * 3 + [p_lo] * 3, axis=-1)
    kfeat = jnp.pad(kfeat, ((0, 0), (0, LANES - 6)))
    qfeat_np = np.zeros((LANES, A_HEADS), np.float32)
    for h, sl in enumerate(slopes_a):
        sl2 = np.float32(np.float32(sl) * np.float32(LOG2_E))
        qfeat_np[:6, h] = _split3_bf16(np.float32(256.0) * sl2) + _split3_bf16(sl2)
    qfeat = jnp.asarray(qfeat_np)

    tri = (lax.broadcasted_iota(jnp.int32, (C_TILE, C_TILE), 1)
           > lax.broadcasted_iota(jnp.int32, (C_TILE, C_TILE), 0)).astype(BF16)
    tri2 = jnp.concatenate([tri, tri], axis=1)
    b_bias = _b_bias(slopes_b)
    cs, ts = _q_scales()

    x2 = x.reshape(m, d)
    for l in range(depth):
        lambda_init = 0.8 - 0.6 * math.exp(-0.3 * l)
        wrm, wt = _split_w_in(w_in[l])
        ak, bq, bk, cq, ck, gate, avt, bvt, cvt, aqt = _inproj(
            x2, norm_g[l].reshape(1, d), wrm, wt, cs, ts)
        ya = _attn_a(aqt, ak, kfeat, avt, gate, qfeat,
                     lambda_q1[l].reshape(1, -1), lambda_k1[l].reshape(1, -1),
                     lambda_q2[l].reshape(1, -1), lambda_k2[l].reshape(1, -1),
                     jnp.tile(subln_g[l], A_HEADS).reshape(-1, 1), lambda_init, b, s)
        sink_row = jnp.repeat(sinks[l], B_TILE).reshape(1, -1)
        yb = _attn_b(bq, bk, bvt, gate, b_bias, sink_row, b, s)
        yc = _attn_c(cq, ck, cvt, tri2, gate, b, s)
        x2 = _outproj(ya, yb, yc, x2, w_out[l].astype(BF16), final_g.reshape(1, d),
                      final_norm=(l == depth - 1))
    return x2.reshape(b, s, d)
```

```python
import functools
import math

import numpy as np
import jax
import jax.numpy as jnp
from jax import lax
from jax.experimental import pallas as pl
from jax.experimental.pallas import tpu as pltpu

F32 = jnp.float32
BF16 = jnp.bfloat16

HEAD_DIM = 64
NORM_EPS = 1e-5
A_HEADS = 4
A_QK_DIM = HEAD_DIM // 2
B_HEADS = 8
B_KV_HEADS = 2
B_GROUP = B_HEADS // B_KV_HEADS
C_HEADS = 4
WINDOW = 128
A_W = A_HEADS * HEAD_DIM
B_W = B_HEADS * HEAD_DIM
B_KV_W = B_KV_HEADS * HEAD_DIM
C_W = C_HEADS * HEAD_DIM
GATE_W = A_W + B_W + C_W
LANES = 128
MASK_VALUE = -1e30
LOG2_E = math.log2(math.e)
C_ZERO_WEIGHT_LOG2 = -105.0 * LOG2_E
VMEM_LIMIT_BYTES = 48 * 1024 * 1024

ROW_TILE = 512
A_TILE = 256
A_KEYS = 2 * A_TILE
C_TILE = 256
B_TILE = WINDOW
B_BLOCKS = 2

OFF_AK = 0
OFF_BQ = OFF_AK + A_W
OFF_BK = OFF_BQ + B_W
OFF_CQ = OFF_BK + B_KV_W
OFF_CK = OFF_CQ + C_W
OFF_GATE = OFF_CK + C_W
RM_W = OFF_GATE + GATE_W
TOFF_AV = 0
TOFF_BV = TOFF_AV + A_W
TOFF_CV = TOFF_BV + B_KV_W
TOFF_AQ = TOFF_CV + C_W
T_W = TOFF_AQ + A_W
GATE_BLOCK_A = B_W // A_W
GATE_BLOCK_C = (B_W + A_W) // C_W


def _alibi_slopes():
    n = A_HEADS + B_HEADS
    s = (2.0 ** (-8.0 * np.arange(1, n + 1) / n)).astype(np.float32)
    return s[B_HEADS:], s[:B_HEADS]


def _split3_bf16(v):
    out = []
    r = np.float32(v)
    for _ in range(3):
        p = np.asarray(r, np.float32).astype(BF16).astype(np.float32)
        out.append(float(p))
        r = np.float32(r - p)
    return out


def _dot_nt(a, b):
    return lax.dot_general(a, b, (((1,), (1,)), ((), ())), preferred_element_type=F32)


def _silu(g):
    return g * jax.nn.sigmoid(g)


def _inproj_kernel(x_ref, g_ref, wrm_ref, wt_ref, cs_ref, ts_ref,
                   ak_ref, bq_ref, bk_ref, cq_ref, ck_ref, gate_ref,
                   avt_ref, bvt_ref, cvt_ref, aqt_ref):
    x = x_ref[...]
    ms = jnp.mean(x * x, axis=-1, keepdims=True)
    h = (x * lax.rsqrt(ms + NORM_EPS) * g_ref[...]).astype(BF16)
    qk = jnp.dot(h, wrm_ref[:, :OFF_GATE], preferred_element_type=F32) * cs_ref[...]
    ak_ref[...] = qk[:, OFF_AK:OFF_BQ].astype(BF16)
    bq_ref[...] = qk[:, OFF_BQ:OFF_BK].astype(BF16)
    bk_ref[...] = qk[:, OFF_BK:OFF_CQ].astype(BF16)
    cq_ref[...] = qk[:, OFF_CQ:OFF_CK].astype(BF16)
    ck_ref[...] = qk[:, OFF_CK:OFF_GATE].astype(BF16)
    gate_ref[...] = jnp.dot(h, wrm_ref[:, OFF_GATE:], preferred_element_type=F32)
    tr = (_dot_nt(wt_ref[...], h) * ts_ref[...]).astype(BF16)
    for c in range(ROW_TILE // A_TILE):
        avt_ref[c] = tr[TOFF_AV:TOFF_BV, c * A_TILE:(c + 1) * A_TILE]
        aqt_ref[c] = tr[TOFF_AQ:, c * A_TILE:(c + 1) * A_TILE]
    for c in range(ROW_TILE // B_TILE):
        bvt_ref[c] = tr[TOFF_BV:TOFF_CV, c * B_TILE:(c + 1) * B_TILE]
    for c in range(ROW_TILE // C_TILE):
        cvt_ref[c] = tr[TOFF_CV:TOFF_AQ, c * C_TILE:(c + 1) * C_TILE]


def _inproj(x2, g, wrm, wt, cs, ts):
    m, d = x2.shape
    tm = ROW_TILE
    row = lambda i: (i, 0)
    fixed = lambda i: (0, 0)
    tiles = lambda i: (i, 0, 0)
    widths = (A_W, B_W, B_KV_W, C_W, C_W)
    return pl.pallas_call(
        _inproj_kernel,
        grid=(m // tm,),
        in_specs=[
            pl.BlockSpec((tm, d), row),
            pl.BlockSpec((1, d), fixed),
            pl.BlockSpec((d, RM_W), fixed),
            pl.BlockSpec((T_W, d), fixed),
            pl.BlockSpec((1, OFF_GATE), fixed),
            pl.BlockSpec((T_W, 1), fixed),
        ],
        out_specs=[pl.BlockSpec((tm, w), row) for w in widths] + [
            pl.BlockSpec((tm, GATE_W), row),
            pl.BlockSpec((tm // A_TILE, A_W, A_TILE), tiles),
            pl.BlockSpec((tm // B_TILE, B_KV_W, B_TILE), tiles),
            pl.BlockSpec((tm // C_TILE, C_W, C_TILE), tiles),
            pl.BlockSpec((tm // A_TILE, A_W, A_TILE), tiles),
        ],
        out_shape=[jax.ShapeDtypeStruct((m, w), BF16) for w in widths] + [
            jax.ShapeDtypeStruct((m, GATE_W), F32),
            jax.ShapeDtypeStruct((m // A_TILE, A_W, A_TILE), BF16),
            jax.ShapeDtypeStruct((m // B_TILE, B_KV_W, B_TILE), BF16),
            jax.ShapeDtypeStruct((m // C_TILE, C_W, C_TILE), BF16),
            jax.ShapeDtypeStruct((m // A_TILE, A_W, A_TILE), BF16),
        ],
        compiler_params=pltpu.CompilerParams(
            dimension_semantics=("parallel",), vmem_limit_bytes=VMEM_LIMIT_BYTES),
        name="inproj",
    )(x2, g, wrm, wt, cs, ts)


def _attn_a_kernel(lq1_ref, lk1_ref, lq2_ref, lk2_ref, sg_ref, qfeat_ref,
                   qt_ref, k_ref, kfeat_ref, vt_ref, gate_ref, o_ref,
                   qz_sc, m_sc, l_sc, alpha_sc, acc_sc, s_sc, smax_sc, p_sc, *, lambda_init):
    t = A_TILE
    tk = A_KEYS
    qi = pl.program_id(1)
    n_full = qi // 2
    m_sc[...] = jnp.full(m_sc.shape, MASK_VALUE, F32)
    l_sc[...] = jnp.zeros(l_sc.shape, F32)
    acc_sc[...] = jnp.zeros(acc_sc.shape, F32)

    sub = lax.broadcasted_iota(jnp.int32, (LANES, t), 0)
    for pr in range(2):
        qp = qt_ref[pr * LANES:(pr + 1) * LANES, :]
        for c in range(4):
            hh, mp = divmod(c, 2)
            lo = hh * HEAD_DIM + mp * A_QK_DIM
            real = jnp.where((sub >= lo) & (sub < lo + A_QK_DIM), qp, jnp.zeros_like(qp))
            h = 2 * pr + hh
            feat = jnp.broadcast_to(qfeat_ref[:, h:h + 1], (LANES, t)).astype(BF16)
            qz_sc[pr, :, c * t:(c + 1) * t] = jnp.concatenate([real, feat], axis=0)

    def scores(pr, step):
        ks = pl.ds(pl.multiple_of(step * tk, tk), tk)
        lhs = jnp.concatenate([k_ref[ks, pr * LANES:(pr + 1) * LANES], kfeat_ref[ks, :]], axis=1)
        return jnp.dot(lhs, qz_sc[pr], preferred_element_type=F32)

    def softmax_step(pr, s, smax):
        m_old = m_sc[pr]
        m_new = jnp.maximum(m_old, smax)
        alpha = jnp.exp2(m_old - m_new)
        p = jnp.exp2(s - m_new)
        l_sc[pr] = alpha * l_sc[pr] + jnp.sum(p, axis=0, keepdims=True)
        m_sc[pr] = m_new
        return alpha, p.astype(BF16)

    def value_step(pr, step):
        for hh in range(2):
            h = 2 * pr + hh
            rows = slice(h * HEAD_DIM, (h + 1) * HEAD_DIM)
            cols = slice(hh * 2 * t, (hh + 1) * 2 * t)
            vt = jnp.concatenate([vt_ref[2 * step, rows, :], vt_ref[2 * step + 1, rows, :]], axis=1)
            pv = jnp.dot(vt, p_sc[pr, :, cols], preferred_element_type=F32)
            acc_sc[h] = alpha_sc[pr, :, cols] * acc_sc[h] + pv

    def causal_mask():
        kmq = (lax.broadcasted_iota(jnp.int32, (tk, 4 * t), 0)
               - (lax.broadcasted_iota(jnp.int32, (tk, 4 * t), 1) & (t - 1)))
        return kmq <= qi * t - n_full * tk

    def stage_scores(pr, s):
        s_sc[pr] = s
        smax_sc[pr] = jnp.max(s, axis=0, keepdims=True)

    def trip(i, masked_next, has_next):
        for pr in range(2):
            value_step(pr, jnp.maximum(i - 1, 0))
            alpha, p = softmax_step(pr, s_sc[pr], smax_sc[pr])
            if has_next:
                sn = scores(pr, i + 1)
                if masked_next:
                    sn = jnp.where(causal_mask(), sn, MASK_VALUE)
                stage_scores(pr, sn)
            alpha_sc[pr] = alpha
            p_sc[pr] = p

    p_sc[...] = jnp.zeros(p_sc.shape, BF16)
    alpha_sc[...] = jnp.ones(alpha_sc.shape, F32)
    keep = jnp.logical_or(causal_mask(), n_full > 0)
    for pr in range(2):
        stage_scores(pr, jnp.where(keep, scores(pr, 0), MASK_VALUE))

    def body(i, carry):
        trip(i, False, True)
        return carry

    lax.fori_loop(0, n_full - 1, body, 0)

    @pl.when(n_full > 0)
    def _():
        trip(n_full - 1, True, True)

    trip(n_full, False, False)
    for pr in range(2):
        value_step(pr, n_full)

    lam = (jnp.exp(jnp.sum(lq1_ref[...] * lk1_ref[...], axis=-1, keepdims=True))
           - jnp.exp(jnp.sum(lq2_ref[...] * lk2_ref[...], axis=-1, keepdims=True))
           + lambda_init)
    outs = []
    for h in range(A_HEADS):
        pr, hh = divmod(h, 2)
        acc = acc_sc[h]
        l = l_sc[pr, :, hh * 2 * t:(hh + 1) * 2 * t]
        o = acc[:, :t] / l[:, :t] - lam * (acc[:, t:] / l[:, t:])
        ms = jnp.mean(o * o, axis=0, keepdims=True)
        outs.append(o * lax.rsqrt(ms + NORM_EPS))
    o_all = jnp.concatenate(outs, axis=0) * sg_ref[...] * (1.0 - lambda_init)
    o_ref[...] = (o_all.T * _silu(gate_ref[...])).astype(BF16)


def _attn_a(aqt, ak, kfeat, avt, gate, qfeat, lq1, lk1, lq2, lk2, sg4, lambda_init, b, s):
    t = A_TILE
    tk = A_KEYS
    nt = s // t
    small = lambda shape: pl.BlockSpec(shape, lambda bi, qi: (0, 0))
    return pl.pallas_call(
        functools.partial(_attn_a_kernel, lambda_init=lambda_init),
        grid=(b, nt),
        in_specs=[
            small((1, A_QK_DIM)), small((1, A_QK_DIM)), small((1, A_QK_DIM)), small((1, A_QK_DIM)),
            small((A_W, 1)),
            small((LANES, A_HEADS)),
            pl.BlockSpec((None, A_W, t), lambda bi, qi: (bi * nt + qi, 0, 0)),
            pl.BlockSpec((s, A_W), lambda bi, qi: (bi, 0)),
            pl.BlockSpec((s, LANES), lambda bi, qi: (0, 0)),
            pl.BlockSpec((nt, A_W, t), lambda bi, qi: (bi, 0, 0)),
            pl.BlockSpec((t, A_W), lambda bi, qi: (bi * nt + qi, GATE_BLOCK_A)),
        ],
        out_specs=pl.BlockSpec((t, A_W), lambda bi, qi: (bi * nt + qi, 0)),
        out_shape=jax.ShapeDtypeStruct((b * s, A_W), BF16),
        scratch_shapes=[
            pltpu.VMEM((2, 2 * LANES, 4 * t), BF16),
            pltpu.VMEM((2, 1, 4 * t), F32),
            pltpu.VMEM((2, 1, 4 * t), F32),
            pltpu.VMEM((2, 1, 4 * t), F32),
            pltpu.VMEM((A_HEADS, HEAD_DIM, 2 * t), F32),
            pltpu.VMEM((2, tk, 4 * t), F32),
            pltpu.VMEM((2, 1, 4 * t), F32),
            pltpu.VMEM((2, tk, 4 * t), BF16),
        ],
        compiler_params=pltpu.CompilerParams(
            dimension_semantics=("parallel", "arbitrary"),
            vmem_limit_bytes=VMEM_LIMIT_BYTES),
        name="attn_a",
    )(lq1, lk1, lq2, lk2, sg4, qfeat, aqt, ak, kfeat, avt, gate)


def _attn_b_kernel(q_ref, kp_ref, kc_ref, vtp_ref, vtc_ref, gate_ref, bias_ref, sink_ref, o_ref):
    t = B_TILE
    qi = pl.program_id(1)
    lane = lax.broadcasted_iota(jnp.int32, (t, LANES), 1)
    sink = sink_ref[...]
    k_blocks = [kp_ref[...]] + [kc_ref[c * t:(c + 1) * t, :] for c in range(B_BLOCKS)]
    vt_blocks = [vtp_ref[...]] + [vtc_ref[c] for c in range(B_BLOCKS)]
    for c in range(B_BLOCKS):
        rows = slice(c * t, (c + 1) * t)
        k_all = jnp.concatenate(k_blocks[c:c + 2], axis=0)
        vt_all = jnp.concatenate(vt_blocks[c:c + 2], axis=1)
        pieces = [None] * B_HEADS
        for j in range(B_GROUP):
            qp = q_ref[rows, j * LANES:(j + 1) * LANES]
            zero = jnp.zeros_like(qp)
            pieces[j] = jnp.where(lane < HEAD_DIM, qp, zero)
            pieces[B_GROUP + j] = jnp.where(lane >= HEAD_DIM, qp, zero)
        qz = jnp.concatenate(pieces, axis=0)
        bias = bias_ref[jnp.minimum(qi, 1)] if c == 0 else bias_ref[1]
        s = _dot_nt(k_all, qz) + bias
        m = jnp.maximum(jnp.max(s, axis=0, keepdims=True), sink)
        p = jnp.exp(s - m)
        inv = 1.0 / (jnp.sum(p, axis=0, keepdims=True) + jnp.exp(sink - m))
        pb = p.astype(BF16)
        outs = []
        for g in range(B_KV_HEADS):
            cols = slice(g * B_GROUP * t, (g + 1) * B_GROUP * t)
            og = jnp.dot(vt_all[g * HEAD_DIM:(g + 1) * HEAD_DIM, :], pb[:, cols],
                         preferred_element_type=F32) * inv[:, cols]
            outs.extend(og[:, j * t:(j + 1) * t] for j in range(B_GROUP))
        o_all = jnp.concatenate(outs, axis=0)
        o_ref[rows, :] = (o_all.T * _silu(gate_ref[rows, :])).astype(BF16)


def _attn_b(bq, bk, bvt, gate, bias, sink_row, b, s):
    t = B_TILE
    tq = B_BLOCKS * t
    nt = s // tq
    cur = lambda bi, qi: (bi * nt + qi, 0)
    prev = lambda bi, qi: (jnp.maximum((bi * nt + qi) * B_BLOCKS - 1, 0), 0)
    cur3 = lambda bi, qi: (bi * nt + qi, 0, 0)
    prev3 = lambda bi, qi: (jnp.maximum((bi * nt + qi) * B_BLOCKS - 1, 0), 0, 0)
    return pl.pallas_call(
        _attn_b_kernel,
        grid=(b, nt),
        in_specs=[
            pl.BlockSpec((tq, B_W), cur),
            pl.BlockSpec((t, B_KV_W), prev),
            pl.BlockSpec((tq, B_KV_W), cur),
            pl.BlockSpec((None, B_KV_W, t), prev3),
            pl.BlockSpec((B_BLOCKS, B_KV_W, t), cur3),
            pl.BlockSpec((tq, B_W), cur),
            pl.BlockSpec((2, 2 * t, B_HEADS * t), lambda bi, qi: (0, 0, 0)),
            pl.BlockSpec((1, B_HEADS * t), lambda bi, qi: (0, 0)),
        ],
        out_specs=pl.BlockSpec((tq, B_W), cur),
        out_shape=jax.ShapeDtypeStruct((b * s, B_W), BF16),
        compiler_params=pltpu.CompilerParams(
            dimension_semantics=("parallel", "arbitrary"),
            vmem_limit_bytes=VMEM_LIMIT_BYTES),
        name="attn_b",
    )(bq, bk, bk, bvt, bvt, gate, bias, sink_row)


def _b_bias(slopes):
    t = B_TILE
    koff = np.arange(2 * t)[:, None] - t
    qoff = np.arange(t)[None, :]
    dist = qoff - koff
    band = (dist >= 0) & (dist < WINDOW)
    out = np.empty((2, 2 * t, B_HEADS * t), np.float32)
    for h in range(B_HEADS):
        term = (-np.float32(slopes[h]) * dist.astype(np.float32)).astype(np.float32)
        out[1, :, h * t:(h + 1) * t] = np.where(band, term, np.float32(MASK_VALUE))
        out[0, :, h * t:(h + 1) * t] = np.where(band & (koff >= 0), term, np.float32(MASK_VALUE))
    return jnp.asarray(out)


def _attn_c_kernel(q_ref, k_ref, vt_ref, u_ref, gate_ref, o_ref, qz_sc, carry_sc, acc_sc):
    t = C_TILE
    qi = pl.program_id(1)
    carry_sc[...] = jnp.zeros(carry_sc.shape, F32)
    acc_sc[...] = jnp.zeros(acc_sc.shape, F32)
    q = q_ref[...]
    lane = lax.broadcasted_iota(jnp.int32, (t, C_W), 1)
    for h in range(C_HEADS):
        own = (lane >= h * HEAD_DIM) & (lane < (h + 1) * HEAD_DIM)
        qz_sc[h * t:(h + 1) * t, :] = jnp.where(own, q, jnp.zeros_like(q))
    u2 = u_ref[...]

    def tile(kj, masked):
        ks = pl.ds(pl.multiple_of(kj * t, t), t)
        z = _dot_nt(k_ref[ks, :], qz_sc[...])
        lse = jnp.log2(1.0 + jnp.exp2(jnp.minimum(z, -z)))
        log_beta = jnp.minimum(z, 0.0) - lse
        l1m = log_beta - z
        if masked:
            krow = lax.broadcasted_iota(jnp.int32, (t, C_HEADS * t), 0)
            qcol = lax.broadcasted_iota(jnp.int32, (t, C_HEADS * t), 1) & (t - 1)
            past = krow < qcol
            l1m = jnp.where(past, l1m, 0.0)
        hi = l1m.astype(BF16)
        lo = (l1m - hi.astype(F32)).astype(BF16)
        tail = jnp.dot(u2, jnp.concatenate([hi, lo], axis=0),
                       preferred_element_type=F32)
        a = jnp.exp2(log_beta + tail + carry_sc[...])
        if masked:
            a = jnp.where(past, a, 0.0)
        a = a.astype(BF16)
        for h in range(C_HEADS):
            rows = slice(h * HEAD_DIM, (h + 1) * HEAD_DIM)
            acc_sc[rows, :] += jnp.dot(vt_ref[kj, rows, :], a[:, h * t:(h + 1) * t],
                                       preferred_element_type=F32)
        carry = carry_sc[...] + jnp.sum(l1m, axis=0, keepdims=True)
        carry_sc[...] = carry
        return jnp.max(carry) > C_ZERO_WEIGHT_LOG2

    go0 = tile(qi, True)

    def cond(state):
        i, go = state
        return jnp.logical_and(i < qi, go)

    def body(state):
        i, _ = state
        return i + 1, tile(qi - 1 - i, False)

    lax.while_loop(cond, body, (jnp.int32(0), go0))
    o_ref[...] = (acc_sc[...].T * _silu(gate_ref[...])).astype(BF16)


def _attn_c(cq, ck, cvt, u2, gate, b, s):
    t = C_TILE
    nt = s // t
    return pl.pallas_call(
        _attn_c_kernel,
        grid=(b, nt),
        in_specs=[
            pl.BlockSpec((t, C_W), lambda bi, qi: (bi * nt + qi, 0)),
            pl.BlockSpec((s, C_W), lambda bi, qi: (bi, 0)),
            pl.BlockSpec((nt, C_W, t), lambda bi, qi: (bi, 0, 0)),
            pl.BlockSpec((t, 2 * t), lambda bi, qi: (0, 0)),
            pl.BlockSpec((t, C_W), lambda bi, qi: (bi * nt + qi, GATE_BLOCK_C)),
        ],
        out_specs=pl.BlockSpec((t, C_W), lambda bi, qi: (bi * nt + qi, 0)),
        out_shape=jax.ShapeDtypeStruct((b * s, C_W), BF16),
        scratch_shapes=[
            pltpu.VMEM((C_HEADS * t, C_W), BF16),
            pltpu.VMEM((1, C_HEADS * t), F32),
            pltpu.VMEM((C_W, t), F32),
        ],
        compiler_params=pltpu.CompilerParams(
            dimension_semantics=("parallel", "arbitrary"),
            vmem_limit_bytes=VMEM_LIMIT_BYTES),
        name="attn_c",
    )(cq, ck, cvt, u2, gate)


def _outproj_kernel(ya_ref, yb_ref, yc_ref, x_ref, w_ref, fg_ref, o_ref, *, final_norm):
    y = jnp.concatenate([ya_ref[...], yb_ref[...], yc_ref[...]], axis=1)
    r = x_ref[...] + jnp.dot(y, w_ref[...], preferred_element_type=F32)
    if final_norm:
        ms = jnp.mean(r * r, axis=-1, keepdims=True)
        r = r * lax.rsqrt(ms + NORM_EPS) * fg_ref[...]
    o_ref[...] = r


def _outproj(ya, yb, yc, x2, w, fg, final_norm):
    m, d = x2.shape
    row = lambda i: (i, 0)
    fixed = lambda i: (0, 0)
    return pl.pallas_call(
        functools.partial(_outproj_kernel, final_norm=final_norm),
        grid=(m // ROW_TILE,),
        in_specs=[
            pl.BlockSpec((ROW_TILE, A_W), row),
            pl.BlockSpec((ROW_TILE, B_W), row),
            pl.BlockSpec((ROW_TILE, C_W), row),
            pl.BlockSpec((ROW_TILE, d), row),
            pl.BlockSpec((GATE_W, d), fixed),
            pl.BlockSpec((1, d), fixed),
        ],
        out_specs=pl.BlockSpec((ROW_TILE, d), row),
        out_shape=jax.ShapeDtypeStruct((m, d), F32),
        compiler_params=pltpu.CompilerParams(
            dimension_semantics=("parallel",), vmem_limit_bytes=VMEM_LIMIT_BYTES),
        name="outproj",
    )(ya, yb, yc, x2, w, fg)


def _split_w_in(w):
    splits = (A_W, A_W, A_W, A_W, B_W, B_KV_W, B_KV_W, B_W, C_W, C_W, C_W, C_W)
    pts = np.cumsum(splits)[:-1]
    aq, ak, av, ag, bq, bk, bv, bg, cq, ck, cv, cg = jnp.split(w, pts, axis=-1)
    d = w.shape[0]
    bq = bq.reshape(d, B_KV_HEADS, B_GROUP, HEAD_DIM).transpose(0, 2, 1, 3).reshape(d, B_W)
    wrm = jnp.concatenate([ak, bq, bk, cq, ck, bg, ag, cg], axis=-1)
    wt = jnp.concatenate([av, bv, cv, aq], axis=-1).T
    return wrm.astype(BF16), wt.astype(BF16)


def _q_scales():
    cs = np.ones((1, OFF_GATE), np.float32)
    cs[0, OFF_BQ:OFF_BK] = HEAD_DIM ** -0.5
    cs[0, OFF_CQ:OFF_CK] = HEAD_DIM ** -0.5 * LOG2_E
    ts = np.ones((T_W, 1), np.float32)
    ts[TOFF_AQ:, 0] = A_QK_DIM ** -0.5 * LOG2_E
    return jnp.asarray(cs), jnp.asarray(ts)


def kernel(x, norm_g, w_in, lambda_q1, lambda_k1, lambda_q2, lambda_k2, subln_g, sinks, w_out, final_g):
    b, s, d = x.shape
    depth = w_in.shape[0]
    m = b * s
    slopes_a, slopes_b = _alibi_slopes()

    pos = jnp.arange(s, dtype=jnp.int32)
    p_hi = (pos // 256).astype(BF16)
    p_lo = (pos % 256).astype(BF16)
    kfeat = jnp.stack([p_hi] * 3 + [p_lo] * 3, axis=-1)
    kfeat = jnp.pad(kfeat, ((0, 0), (0, LANES - 6)))
    qfeat_np = np.zeros((LANES, A_HEADS), np.float32)
    for h, sl in enumerate(slopes_a):
        sl2 = np.float32(np.float32(sl) * np.float32(LOG2_E))
        qfeat_np[:6, h] = _split3_bf16(np.float32(256.0) * sl2) + _split3_bf16(sl2)
    qfeat = jnp.asarray(qfeat_np)

    tri = (lax.broadcasted_iota(jnp.int32, (C_TILE, C_TILE), 1)
           > lax.broadcasted_iota(jnp.int32, (C_TILE, C_TILE), 0)).astype(BF16)
    tri2 = jnp.concatenate([tri, tri], axis=1)
    b_bias = _b_bias(slopes_b)
    cs, ts = _q_scales()

    x2 = x.reshape(m, d)
    for l in range(depth):
        lambda_init = 0.8 - 0.6 * math.exp(-0.3 * l)
        wrm, wt = _split_w_in(w_in[l])
        ak, bq, bk, cq, ck, gate, avt, bvt, cvt, aqt = _inproj(
            x2, norm_g[l].reshape(1, d), wrm, wt, cs, ts)
        ya = _attn_a(aqt, ak, kfeat, avt, gate, qfeat,
                     lambda_q1[l].reshape(1, -1), lambda_k1[l].reshape(1, -1),
                     lambda_q2[l].reshape(1, -1), lambda_k2[l].reshape(1, -1),
                     jnp.tile(subln_g[l], A_HEADS).reshape(-1, 1), lambda_init, b, s)
        sink_row = jnp.repeat(sinks[l], B_TILE).reshape(1, -1)
        yb = _attn_b(bq, bk, bvt, gate, b_bias, sink_row, b, s)
        yc = _attn_c(cq, ck, cvt, tri2, gate, b, s)
        x2 = _outproj(ya, yb, yc, x2, w_out[l].astype(BF16), final_g.reshape(1, d),
                      final_norm=(l == depth - 1))
    return x2.reshape(b, s, d)
```

```python
import functools
import math

import numpy as np
import jax
import jax.numpy as jnp
from jax import lax
from jax.experimental import pallas as pl
from jax.experimental.pallas import tpu as pltpu

F32 = jnp.float32
BF16 = jnp.bfloat16

HEAD_DIM = 64
NORM_EPS = 1e-5
A_HEADS = 4
A_QK_DIM = HEAD_DIM // 2
B_HEADS = 8
B_KV_HEADS = 2
B_GROUP = B_HEADS // B_KV_HEADS
C_HEADS = 4
WINDOW = 128
A_W = A_HEADS * HEAD_DIM
B_W = B_HEADS * HEAD_DIM
B_KV_W = B_KV_HEADS * HEAD_DIM
C_W = C_HEADS * HEAD_DIM
GATE_W = A_W + B_W + C_W
LANES = 128
MASK_VALUE = -1e30
LOG2_E = math.log2(math.e)
C_ZERO_WEIGHT_LOG2 = -105.0 * LOG2_E
VMEM_LIMIT_BYTES = 48 * 1024 * 1024

ROW_TILE = 512
OUT_ROW_TILE = 1024
A_TILE = 256
A_KEYS = 2 * A_TILE
C_TILE = 256
B_TILE = WINDOW
B_BLOCKS = 2

OFF_AK = 0
OFF_BQ = OFF_AK + A_W
OFF_BK = OFF_BQ + B_W
OFF_CQ = OFF_BK + B_KV_W
OFF_CK = OFF_CQ + C_W
OFF_GATE = OFF_CK + C_W
RM_W = OFF_GATE + GATE_W
TOFF_AV = 0
TOFF_BV = TOFF_AV + A_W
TOFF_CV = TOFF_BV + B_KV_W
TOFF_AQ = TOFF_CV + C_W
T_W = TOFF_AQ + A_W
GATE_BLOCK_A = B_W // A_W
GATE_BLOCK_C = (B_W + A_W) // C_W


def _alibi_slopes():
    n = A_HEADS + B_HEADS
    s = (2.0 ** (-8.0 * np.arange(1, n + 1) / n)).astype(np.float32)
    return s[B_HEADS:], s[:B_HEADS]


def _split3_bf16(v):
    out = []
    r = np.float32(v)
    for _ in range(3):
        p = np.asarray(r, np.float32).astype(BF16).astype(np.float32)
        out.append(float(p))
        r = np.float32(r - p)
    return out


def _dot_nt(a, b):
    return lax.dot_general(a, b, (((1,), (1,)), ((), ())), preferred_element_type=F32)


def _silu(g):
    return g * jax.nn.sigmoid(g)


def _inproj_kernel(x_ref, g_ref, wrm_ref, wt_ref, cs_ref, ts_ref,
                   ak_ref, bq_ref, bk_ref, cq_ref, ck_ref, gate_ref,
                   avt_ref, bvt_ref, cvt_ref, aqt_ref):
    x = x_ref[...]
    ms = jnp.mean(x * x, axis=-1, keepdims=True)
    h = (x * lax.rsqrt(ms + NORM_EPS) * g_ref[...]).astype(BF16)
    qk = jnp.dot(h, wrm_ref[:, :OFF_GATE], preferred_element_type=F32) * cs_ref[...]
    ak_ref[...] = qk[:, OFF_AK:OFF_BQ].astype(BF16)
    bq_ref[...] = qk[:, OFF_BQ:OFF_BK].astype(BF16)
    bk_ref[...] = qk[:, OFF_BK:OFF_CQ].astype(BF16)
    cq_ref[...] = qk[:, OFF_CQ:OFF_CK].astype(BF16)
    ck_ref[...] = qk[:, OFF_CK:OFF_GATE].astype(BF16)
    gate_ref[...] = jnp.dot(h, wrm_ref[:, OFF_GATE:], preferred_element_type=F32)
    tr = (_dot_nt(wt_ref[...], h) * ts_ref[...]).astype(BF16)
    for c in range(ROW_TILE // A_TILE):
        avt_ref[c] = tr[TOFF_AV:TOFF_BV, c * A_TILE:(c + 1) * A_TILE]
        aqt_ref[c] = tr[TOFF_AQ:, c * A_TILE:(c + 1) * A_TILE]
    for c in range(ROW_TILE // B_TILE):
        bvt_ref[c] = tr[TOFF_BV:TOFF_CV, c * B_TILE:(c + 1) * B_TILE]
    for c in range(ROW_TILE // C_TILE):
        cvt_ref[c] = tr[TOFF_CV:TOFF_AQ, c * C_TILE:(c + 1) * C_TILE]


def _inproj(x2, g, wrm, wt, cs, ts):
    m, d = x2.shape
    tm = ROW_TILE
    row = lambda i: (i, 0)
    fixed = lambda i: (0, 0)
    tiles = lambda i: (i, 0, 0)
    widths = (A_W, B_W, B_KV_W, C_W, C_W)
    return pl.pallas_call(
        _inproj_kernel,
        grid=(m // tm,),
        in_specs=[
            pl.BlockSpec((tm, d), row),
            pl.BlockSpec((1, d), fixed),
            pl.BlockSpec((d, RM_W), fixed),
            pl.BlockSpec((T_W, d), fixed),
            pl.BlockSpec((1, OFF_GATE), fixed),
            pl.BlockSpec((T_W, 1), fixed),
        ],
        out_specs=[pl.BlockSpec((tm, w), row) for w in widths] + [
            pl.BlockSpec((tm, GATE_W), row),
            pl.BlockSpec((tm // A_TILE, A_W, A_TILE), tiles),
            pl.BlockSpec((tm // B_TILE, B_KV_W, B_TILE), tiles),
            pl.BlockSpec((tm // C_TILE, C_W, C_TILE), tiles),
            pl.BlockSpec((tm // A_TILE, A_W, A_TILE), tiles),
        ],
        out_shape=[jax.ShapeDtypeStruct((m, w), BF16) for w in widths] + [
            jax.ShapeDtypeStruct((m, GATE_W), F32),
            jax.ShapeDtypeStruct((m // A_TILE, A_W, A_TILE), BF16),
            jax.ShapeDtypeStruct((m // B_TILE, B_KV_W, B_TILE), BF16),
            jax.ShapeDtypeStruct((m // C_TILE, C_W, C_TILE), BF16),
            jax.ShapeDtypeStruct((m // A_TILE, A_W, A_TILE), BF16),
        ],
        compiler_params=pltpu.CompilerParams(
            dimension_semantics=("parallel",), vmem_limit_bytes=VMEM_LIMIT_BYTES),
        name="inproj",
    )(x2, g, wrm, wt, cs, ts)


def _attn_a_kernel(lq1_ref, lk1_ref, lq2_ref, lk2_ref, sg_ref, qfeat_ref,
                   qt_ref, k_ref, kfeat_ref, vt_ref, gate_ref, o_ref,
                   qz_sc, m_sc, l_sc, alpha_sc, acc_sc, s_sc, smax_sc, p_sc, *, lambda_init):
    t = A_TILE
    tk = A_KEYS
    qi = pl.program_id(1)
    n_full = qi // 2
    m_sc[...] = jnp.full(m_sc.shape, MASK_VALUE, F32)
    l_sc[...] = jnp.zeros(l_sc.shape, F32)
    acc_sc[...] = jnp.zeros(acc_sc.shape, F32)

    sub = lax.broadcasted_iota(jnp.int32, (LANES, t), 0)
    for pr in range(2):
        qp = qt_ref[pr * LANES:(pr + 1) * LANES, :]
        for c in range(4):
            hh, mp = divmod(c, 2)
            lo = hh * HEAD_DIM + mp * A_QK_DIM
            real = jnp.where((sub >= lo) & (sub < lo + A_QK_DIM), qp, jnp.zeros_like(qp))
            h = 2 * pr + hh
            feat = jnp.broadcast_to(qfeat_ref[:, h:h + 1], (LANES, t)).astype(BF16)
            qz_sc[pr, :, c * t:(c + 1) * t] = jnp.concatenate([real, feat], axis=0)

    def scores(pr, step):
        ks = pl.ds(pl.multiple_of(step * tk, tk), tk)
        lhs = jnp.concatenate([k_ref[ks, pr * LANES:(pr + 1) * LANES], kfeat_ref[ks, :]], axis=1)
        return jnp.dot(lhs, qz_sc[pr], preferred_element_type=F32)

    def softmax_step(pr, s, smax):
        m_old = m_sc[pr]
        m_new = jnp.maximum(m_old, smax)
        alpha = jnp.exp2(m_old - m_new)
        p = jnp.exp2(s - m_new)
        l_sc[pr] = alpha * l_sc[pr] + jnp.sum(p, axis=0, keepdims=True)
        m_sc[pr] = m_new
        return alpha, p.astype(BF16)

    def value_step(pr, step):
        for hh in range(2):
            h = 2 * pr + hh
            rows = slice(h * HEAD_DIM, (h + 1) * HEAD_DIM)
            cols = slice(hh * 2 * t, (hh + 1) * 2 * t)
            vt = jnp.concatenate([vt_ref[2 * step, rows, :], vt_ref[2 * step + 1, rows, :]], axis=1)
            pv = jnp.dot(vt, p_sc[pr, :, cols], preferred_element_type=F32)
            acc_sc[h] = alpha_sc[pr, :, cols] * acc_sc[h] + pv

    def causal_mask():
        kmq = (lax.broadcasted_iota(jnp.int32, (tk, 4 * t), 0)
               - (lax.broadcasted_iota(jnp.int32, (tk, 4 * t), 1) & (t - 1)))
        return kmq <= qi * t - n_full * tk

    def stage_scores(pr, s):
        s_sc[pr] = s
        smax_sc[pr] = jnp.max(s, axis=0, keepdims=True)

    def trip(i, masked_next, has_next):
        for pr in range(2):
            value_step(pr, jnp.maximum(i - 1, 0))
            alpha, p = softmax_step(pr, s_sc[pr], smax_sc[pr])
            if has_next:
                sn = scores(pr, i + 1)
                if masked_next:
                    sn = jnp.where(causal_mask(), sn, MASK_VALUE)
                stage_scores(pr, sn)
            alpha_sc[pr] = alpha
            p_sc[pr] = p

    p_sc[...] = jnp.zeros(p_sc.shape, BF16)
    alpha_sc[...] = jnp.ones(alpha_sc.shape, F32)
    keep = jnp.logical_or(causal_mask(), n_full > 0)
    for pr in range(2):
        stage_scores(pr, jnp.where(keep, scores(pr, 0), MASK_VALUE))

    def body(i, carry):
        trip(i, False, True)
        return carry

    lax.fori_loop(0, n_full - 1, body, 0)

    @pl.when(n_full > 0)
    def _():
        trip(n_full - 1, True, True)

    trip(n_full, False, False)
    for pr in range(2):
        value_step(pr, n_full)

    lam = (jnp.exp(jnp.sum(lq1_ref[...] * lk1_ref[...], axis=-1, keepdims=True))
           - jnp.exp(jnp.sum(lq2_ref[...] * lk2_ref[...], axis=-1, keepdims=True))
           + lambda_init)
    outs = []
    for h in range(A_HEADS):
        pr, hh = divmod(h, 2)
        acc = acc_sc[h]
        l = l_sc[pr, :, hh * 2 * t:(hh + 1) * 2 * t]
        o = acc[:, :t] / l[:, :t] - lam * (acc[:, t:] / l[:, t:])
        ms = jnp.mean(o * o, axis=0, keepdims=True)
        outs.append(o * lax.rsqrt(ms + NORM_EPS))
    o_all = jnp.concatenate(outs, axis=0) * sg_ref[...] * (1.0 - lambda_init)
    o_ref[...] = (o_all.T * _silu(gate_ref[...])).astype(BF16)


def _attn_a(aqt, ak, kfeat, avt, gate, qfeat, lq1, lk1, lq2, lk2, sg4, lambda_init, b, s):
    t = A_TILE
    tk = A_KEYS
    nt = s // t
    small = lambda shape: pl.BlockSpec(shape, lambda bi, qi: (0, 0))
    return pl.pallas_call(
        functools.partial(_attn_a_kernel, lambda_init=lambda_init),
        grid=(b, nt),
        in_specs=[
            small((1, A_QK_DIM)), small((1, A_QK_DIM)), small((1, A_QK_DIM)), small((1, A_QK_DIM)),
            small((A_W, 1)),
            small((LANES, A_HEADS)),
            pl.BlockSpec((None, A_W, t), lambda bi, qi: (bi * nt + qi, 0, 0)),
            pl.BlockSpec((s, A_W), lambda bi, qi: (bi, 0)),
            pl.BlockSpec((s, LANES), lambda bi, qi: (0, 0)),
            pl.BlockSpec((nt, A_W, t), lambda bi, qi: (bi, 0, 0)),
            pl.BlockSpec((t, A_W), lambda bi, qi: (bi * nt + qi, GATE_BLOCK_A)),
        ],
        out_specs=pl.BlockSpec((t, A_W), lambda bi, qi: (bi * nt + qi, 0)),
        out_shape=jax.ShapeDtypeStruct((b * s, A_W), BF16),
        scratch_shapes=[
            pltpu.VMEM((2, 2 * LANES, 4 * t), BF16),
            pltpu.VMEM((2, 1, 4 * t), F32),
            pltpu.VMEM((2, 1, 4 * t), F32),
            pltpu.VMEM((2, 1, 4 * t), F32),
            pltpu.VMEM((A_HEADS, HEAD_DIM, 2 * t), F32),
            pltpu.VMEM((2, tk, 4 * t), F32),
            pltpu.VMEM((2, 1, 4 * t), F32),
            pltpu.VMEM((2, tk, 4 * t), BF16),
        ],
        compiler_params=pltpu.CompilerParams(
            dimension_semantics=("parallel", "arbitrary"),
            vmem_limit_bytes=VMEM_LIMIT_BYTES),
        name="attn_a",
    )(lq1, lk1, lq2, lk2, sg4, qfeat, aqt, ak, kfeat, avt, gate)


def _attn_b_kernel(q_ref, kp_ref, kc_ref, vtp_ref, vtc_ref, gate_ref, bias_ref, sink_ref, o_ref):
    t = B_TILE
    qi = pl.program_id(1)
    lane = lax.broadcasted_iota(jnp.int32, (t, LANES), 1)
    sink = sink_ref[...]
    k_blocks = [kp_ref[...]] + [kc_ref[c * t:(c + 1) * t, :] for c in range(B_BLOCKS)]
    vt_blocks = [vtp_ref[...]] + [vtc_ref[c] for c in range(B_BLOCKS)]
    for c in range(B_BLOCKS):
        rows = slice(c * t, (c + 1) * t)
        k_all = jnp.concatenate(k_blocks[c:c + 2], axis=0)
        vt_all = jnp.concatenate(vt_blocks[c:c + 2], axis=1)
        pieces = [None] * B_HEADS
        for j in range(B_GROUP):
            qp = q_ref[rows, j * LANES:(j + 1) * LANES]
            zero = jnp.zeros_like(qp)
            pieces[j] = jnp.where(lane < HEAD_DIM, qp, zero)
            pieces[B_GROUP + j] = jnp.where(lane >= HEAD_DIM, qp, zero)
        qz = jnp.concatenate(pieces, axis=0)
        bias = bias_ref[jnp.minimum(qi, 1)] if c == 0 else bias_ref[1]
        s = _dot_nt(k_all, qz) + bias
        m = jnp.maximum(jnp.max(s, axis=0, keepdims=True), sink)
        p = jnp.exp(s - m)
        inv = 1.0 / (jnp.sum(p, axis=0, keepdims=True) + jnp.exp(sink - m))
        pb = p.astype(BF16)
        outs = []
        for g in range(B_KV_HEADS):
            cols = slice(g * B_GROUP * t, (g + 1) * B_GROUP * t)
            og = jnp.dot(vt_all[g * HEAD_DIM:(g + 1) * HEAD_DIM, :], pb[:, cols],
                         preferred_element_type=F32) * inv[:, cols]
            outs.extend(og[:, j * t:(j + 1) * t] for j in range(B_GROUP))
        o_all = jnp.concatenate(outs, axis=0)
        o_ref[rows, :] = (o_all.T * _silu(gate_ref[rows, :])).astype(BF16)


def _attn_b(bq, bk, bvt, gate, bias, sink_row, b, s):
    t = B_TILE
    tq = B_BLOCKS * t
    nt = s // tq
    cur = lambda bi, qi: (bi * nt + qi, 0)
    prev = lambda bi, qi: (jnp.maximum((bi * nt + qi) * B_BLOCKS - 1, 0), 0)
    cur3 = lambda bi, qi: (bi * nt + qi, 0, 0)
    prev3 = lambda bi, qi: (jnp.maximum((bi * nt + qi) * B_BLOCKS - 1, 0), 0, 0)
    return pl.pallas_call(
        _attn_b_kernel,
        grid=(b, nt),
        in_specs=[
            pl.BlockSpec((tq, B_W), cur),
            pl.BlockSpec((t, B_KV_W), prev),
            pl.BlockSpec((tq, B_KV_W), cur),
            pl.BlockSpec((None, B_KV_W, t), prev3),
            pl.BlockSpec((B_BLOCKS, B_KV_W, t), cur3),
            pl.BlockSpec((tq, B_W), cur),
            pl.BlockSpec((2, 2 * t, B_HEADS * t), lambda bi, qi: (0, 0, 0)),
            pl.BlockSpec((1, B_HEADS * t), lambda bi, qi: (0, 0)),
        ],
        out_specs=pl.BlockSpec((tq, B_W), cur),
        out_shape=jax.ShapeDtypeStruct((b * s, B_W), BF16),
        compiler_params=pltpu.CompilerParams(
            dimension_semantics=("parallel", "arbitrary"),
            vmem_limit_bytes=VMEM_LIMIT_BYTES),
        name="attn_b",
    )(bq, bk, bk, bvt, bvt, gate, bias, sink_row)


def _b_bias(slopes):
    t = B_TILE
    koff = np.arange(2 * t)[:, None] - t
    qoff = np.arange(t)[None, :]
    dist = qoff - koff
    band = (dist >= 0) & (dist < WINDOW)
    out = np.empty((2, 2 * t, B_HEADS * t), np.float32)
    for h in range(B_HEADS):
        term = (-np.float32(slopes[h]) * dist.astype(np.float32)).astype(np.float32)
        out[1, :, h * t:(h + 1) * t] = np.where(band, term, np.float32(MASK_VALUE))
        out[0, :, h * t:(h + 1) * t] = np.where(band & (koff >= 0), term, np.float32(MASK_VALUE))
    return jnp.asarray(out)


def _attn_c_kernel(q_ref, k_ref, vt_ref, u_ref, gate_ref, o_ref, qz_sc, carry_sc, acc_sc):
    t = C_TILE
    qi = pl.program_id(1)
    q = q_ref[...]
    lane = lax.broadcasted_iota(jnp.int32, (t, C_W), 1)
    for h in range(C_HEADS):
        own = (lane >= h * HEAD_DIM) & (lane < (h + 1) * HEAD_DIM)
        qz_sc[h * t:(h + 1) * t, :] = jnp.where(own, q, jnp.zeros_like(q))
    u2 = u_ref[...]

    def tile(kj, masked, carry_in):
        ks = pl.ds(pl.multiple_of(kj * t, t), t)
        z = _dot_nt(k_ref[ks, :], qz_sc[...])
        lse = jnp.log2(1.0 + jnp.exp2(jnp.minimum(z, -z)))
        log_beta = jnp.minimum(z, 0.0) - lse
        l1m = log_beta - z
        if masked:
            krow = lax.broadcasted_iota(jnp.int32, (t, C_HEADS * t), 0)
            qcol = lax.broadcasted_iota(jnp.int32, (t, C_HEADS * t), 1) & (t - 1)
            past = krow < qcol
            l1m = jnp.where(past, l1m, 0.0)
        hi = l1m.astype(BF16)
        lo = (l1m - hi.astype(F32)).astype(BF16)
        tail = jnp.dot(u2, jnp.concatenate([hi, lo], axis=0),
                       preferred_element_type=F32)
        a = jnp.exp2(log_beta + tail + carry_in)
        if masked:
            a = jnp.where(past, a, 0.0)
        a = a.astype(BF16)
        parts = [jnp.dot(vt_ref[kj, h * HEAD_DIM:(h + 1) * HEAD_DIM, :], a[:, h * t:(h + 1) * t],
                         preferred_element_type=F32) for h in range(C_HEADS)]
        return jnp.concatenate(parts, axis=0), carry_in + jnp.sum(l1m, axis=0, keepdims=True)

    no_carry = jnp.zeros((1, C_HEADS * t), F32)

    @pl.when(qi == 0)
    def _():
        acc, carry = tile(qi, True, no_carry)
        acc_sc[...] = acc
        carry_sc[...] = carry

    @pl.when(qi > 0)
    def _():
        acc_d, carry_d = tile(qi, True, no_carry)
        acc_p, carry_p = tile(qi - 1, False, carry_d)
        acc_sc[...] = acc_d + acc_p
        carry_sc[...] = carry_p

    def cond(state):
        i, go = state
        return jnp.logical_and(i < qi, go)

    def body(state):
        i, _ = state
        acc, carry = tile(qi - 1 - i, False, carry_sc[...])
        acc_sc[...] += acc
        carry_sc[...] = carry
        return i + 1, jnp.max(carry) > C_ZERO_WEIGHT_LOG2

    lax.while_loop(cond, body, (jnp.int32(1), jnp.max(carry_sc[...]) > C_ZERO_WEIGHT_LOG2))
    o_ref[...] = (acc_sc[...].T * _silu(gate_ref[...])).astype(BF16)


def _attn_c(cq, ck, cvt, u2, gate, b, s):
    t = C_TILE
    nt = s // t
    return pl.pallas_call(
        _attn_c_kernel,
        grid=(b, nt),
        in_specs=[
            pl.BlockSpec((t, C_W), lambda bi, qi: (bi * nt + qi, 0)),
            pl.BlockSpec((s, C_W), lambda bi, qi: (bi, 0)),
            pl.BlockSpec((nt, C_W, t), lambda bi, qi: (bi, 0, 0)),
            pl.BlockSpec((t, 2 * t), lambda bi, qi: (0, 0)),
            pl.BlockSpec((t, C_W), lambda bi, qi: (bi * nt + qi, GATE_BLOCK_C)),
        ],
        out_specs=pl.BlockSpec((t, C_W), lambda bi, qi: (bi * nt + qi, 0)),
        out_shape=jax.ShapeDtypeStruct((b * s, C_W), BF16),
        scratch_shapes=[
            pltpu.VMEM((C_HEADS * t, C_W), BF16),
            pltpu.VMEM((1, C_HEADS * t), F32),
            pltpu.VMEM((C_W, t), F32),
        ],
        compiler_params=pltpu.CompilerParams(
            dimension_semantics=("parallel", "arbitrary"),
            vmem_limit_bytes=VMEM_LIMIT_BYTES),
        name="attn_c",
    )(cq, ck, cvt, u2, gate)


def _outproj_kernel(ya_ref, yb_ref, yc_ref, x_ref, w_ref, fg_ref, o_ref, *, final_norm):
    y = jnp.concatenate([ya_ref[...], yb_ref[...], yc_ref[...]], axis=1)
    r = x_ref[...] + jnp.dot(y, w_ref[...], preferred_element_type=F32)
    if final_norm:
        ms = jnp.mean(r * r, axis=-1, keepdims=True)
        r = r * lax.rsqrt(ms + NORM_EPS) * fg_ref[...]
    o_ref[...] = r


def _outproj(ya, yb, yc, x2, w, fg, final_norm):
    m, d = x2.shape
    row = lambda i: (i, 0)
    fixed = lambda i: (0, 0)
    return pl.pallas_call(
        functools.partial(_outproj_kernel, final_norm=final_norm),
        grid=(m // OUT_ROW_TILE,),
        in_specs=[
            pl.BlockSpec((OUT_ROW_TILE, A_W), row),
            pl.BlockSpec((OUT_ROW_TILE, B_W), row),
            pl.BlockSpec((OUT_ROW_TILE, C_W), row),
            pl.BlockSpec((OUT_ROW_TILE, d), row),
            pl.BlockSpec((GATE_W, d), fixed),
            pl.BlockSpec((1, d), fixed),
        ],
        out_specs=pl.BlockSpec((OUT_ROW_TILE, d), row),
        out_shape=jax.ShapeDtypeStruct((m, d), F32),
        compiler_params=pltpu.CompilerParams(
            dimension_semantics=("parallel",), vmem_limit_bytes=VMEM_LIMIT_BYTES),
        name="outproj",
    )(ya, yb, yc, x2, w, fg)


def _split_w_in(w):
    splits = (A_W, A_W, A_W, A_W, B_W, B_KV_W, B_KV_W, B_W, C_W, C_W, C_W, C_W)
    pts = np.cumsum(splits)[:-1]
    aq, ak, av, ag, bq, bk, bv, bg, cq, ck, cv, cg = jnp.split(w, pts, axis=-1)
    d = w.shape[0]
    bq = bq.reshape(d, B_KV_HEADS, B_GROUP, HEAD_DIM).transpose(0, 2, 1, 3).reshape(d, B_W)
    wrm = jnp.concatenate([ak, bq, bk, cq, ck, bg, ag, cg], axis=-1)
    wt = jnp.concatenate([av, bv, cv, aq], axis=-1).T
    return wrm.astype(BF16), wt.astype(BF16)


def _q_scales():
    cs = np.ones((1, OFF_GATE), np.float32)
    cs[0, OFF_BQ:OFF_BK] = HEAD_DIM ** -0.5
    cs[0, OFF_CQ:OFF_CK] = HEAD_DIM ** -0.5 * LOG2_E
    ts = np.ones((T_W, 1), np.float32)
    ts[TOFF_AQ:, 0] = A_QK_DIM ** -0.5 * LOG2_E
    return jnp.asarray(cs), jnp.asarray(ts)


def kernel(x, norm_g, w_in, lambda_q1, lambda_k1, lambda_q2, lambda_k2, subln_g, sinks, w_out, final_g):
    b, s, d = x.shape
    depth = w_in.shape[0]
    m = b * s
    slopes_a, slopes_b = _alibi_slopes()

    pos = jnp.arange(s, dtype=jnp.int32)
    p_hi = (pos // 256).astype(BF16)
    p_lo = (pos % 256).astype(BF16)
    kfeat = jnp.stack([p_hi] * 3 + [p_lo] * 3, axis=-1)
    kfeat = jnp.pad(kfeat, ((0, 0), (0, LANES - 6)))
    qfeat_np = np.zeros((LANES, A_HEADS), np.float32)
    for h, sl in enumerate(slopes_a):
        sl2 = np.float32(np.float32(sl) * np.float32(LOG2_E))
        qfeat_np[:6, h] = _split3_bf16(np.float32(256.0) * sl2) + _split3_bf16(sl2)
    qfeat = jnp.asarray(qfeat_np)

    tri = (lax.broadcasted_iota(jnp.int32, (C_TILE, C_TILE), 1)
           > lax.broadcasted_iota(jnp.int32, (C_TILE, C_TILE), 0)).astype(BF16)
    tri2 = jnp.concatenate([tri, tri], axis=1)
    b_bias = _b_bias(slopes_b)
    cs, ts = _q_scales()

    x2 = x.reshape(m, d)
    for l in range(depth):
        lambda_init = 0.8 - 0.6 * math.exp(-0.3 * l)
        wrm, wt = _split_w_in(w_in[l])
        ak, bq, bk, cq, ck, gate, avt, bvt, cvt, aqt = _inproj(
            x2, norm_g[l].reshape(1, d), wrm, wt, cs, ts)
        ya = _attn_a(aqt, ak, kfeat, avt, gate, qfeat,
                     lambda_q1[l].reshape(1, -1), lambda_k1[l].reshape(1, -1),
                     lambda_q2[l].reshape(1, -1), lambda_k2[l].reshape(1, -1),
                     jnp.tile(subln_g[l], A_HEADS).reshape(-1, 1), lambda_init, b, s)
        sink_row = jnp.repeat(sinks[l], B_TILE).reshape(1, -1)
        yb = _attn_b(bq, bk, bvt, gate, b_bias, sink_row, b, s)
        yc = _attn_c(cq, ck, cvt, tri2, gate, b, s)
        x2 = _outproj(ya, yb, yc, x2, w_out[l].astype(BF16), final_g.reshape(1, d),
                      final_norm=(l == depth - 1))
    return x2.reshape(b, s, d)
```

```python
import functools
import math

import numpy as np
import jax
import jax.numpy as jnp
from jax import lax
from jax.experimental import pallas as pl
from jax.experimental.pallas import tpu as pltpu

F32 = jnp.float32
BF16 = jnp.bfloat16

HEAD_DIM = 64
NORM_EPS = 1e-5
A_HEADS = 4
A_QK_DIM = HEAD_DIM // 2
B_HEADS = 8
B_KV_HEADS = 2
B_GROUP = B_HEADS // B_KV_HEADS
C_HEADS = 4
WINDOW = 128
A_W = A_HEADS * HEAD_DIM
B_W = B_HEADS * HEAD_DIM
B_KV_W = B_KV_HEADS * HEAD_DIM
C_W = C_HEADS * HEAD_DIM
GATE_W = A_W + B_W + C_W
LANES = 128
MASK_VALUE = -1e30
LOG2_E = math.log2(math.e)
C_ZERO_WEIGHT_LOG2 = -105.0 * LOG2_E
VMEM_LIMIT_BYTES = 48 * 1024 * 1024

ROW_TILE = 512
OUT_ROW_TILE = 1024
A_TILE = 256
A_KEYS = 2 * A_TILE
A_NORM_ROWS = 16
C_TILE = 256
B_TILE = WINDOW
B_BLOCKS = 2

OFF_AK = 0
OFF_BQ = OFF_AK + A_W
OFF_BK = OFF_BQ + B_W
OFF_CQ = OFF_BK + B_KV_W
OFF_CK = OFF_CQ + C_W
OFF_GATE = OFF_CK + C_W
RM_W = OFF_GATE + GATE_W
TOFF_AV = 0
TOFF_BV = TOFF_AV + A_W
TOFF_CV = TOFF_BV + B_KV_W
TOFF_AQ = TOFF_CV + C_W
T_W = TOFF_AQ + A_W
GATE_BLOCK_A = B_W // A_W
GATE_BLOCK_C = (B_W + A_W) // C_W


def _alibi_slopes():
    n = A_HEADS + B_HEADS
    s = (2.0 ** (-8.0 * np.arange(1, n + 1) / n)).astype(np.float32)
    return s[B_HEADS:], s[:B_HEADS]


def _split3_bf16(v):
    out = []
    r = np.float32(v)
    for _ in range(3):
        p = np.asarray(r, np.float32).astype(BF16).astype(np.float32)
        out.append(float(p))
        r = np.float32(r - p)
    return out


def _dot_nt(a, b):
    return lax.dot_general(a, b, (((1,), (1,)), ((), ())), preferred_element_type=F32)


def _silu(g):
    return g * jax.nn.sigmoid(g)


def _inproj_kernel(x_ref, g_ref, wrm_ref, wt_ref, cs_ref, ts_ref,
                   ak_ref, bq_ref, bk_ref, cq_ref, ck_ref, gate_ref,
                   avt_ref, bvt_ref, cvt_ref, aqt_ref):
    x = x_ref[...]
    ms = jnp.mean(x * x, axis=-1, keepdims=True)
    h = (x * lax.rsqrt(ms + NORM_EPS) * g_ref[...]).astype(BF16)
    qk = jnp.dot(h, wrm_ref[:, :OFF_GATE], preferred_element_type=F32) * cs_ref[...]
    ak_ref[...] = qk[:, OFF_AK:OFF_BQ].astype(BF16)
    bq_ref[...] = qk[:, OFF_BQ:OFF_BK].astype(BF16)
    bk_ref[...] = qk[:, OFF_BK:OFF_CQ].astype(BF16)
    cq_ref[...] = qk[:, OFF_CQ:OFF_CK].astype(BF16)
    ck_ref[...] = qk[:, OFF_CK:OFF_GATE].astype(BF16)
    gate_ref[...] = jnp.dot(h, wrm_ref[:, OFF_GATE:], preferred_element_type=F32)
    tr = (_dot_nt(wt_ref[...], h) * ts_ref[...]).astype(BF16)
    for c in range(ROW_TILE // A_TILE):
        avt_ref[c] = tr[TOFF_AV:TOFF_BV, c * A_TILE:(c + 1) * A_TILE]
        aqt_ref[c] = tr[TOFF_AQ:, c * A_TILE:(c + 1) * A_TILE]
    for c in range(ROW_TILE // B_TILE):
        bvt_ref[c] = tr[TOFF_BV:TOFF_CV, c * B_TILE:(c + 1) * B_TILE]
    for c in range(ROW_TILE // C_TILE):
        cvt_ref[c] = tr[TOFF_CV:TOFF_AQ, c * C_TILE:(c + 1) * C_TILE]


def _inproj(x2, g, wrm, wt, cs, ts):
    m, d = x2.shape
    tm = ROW_TILE
    row = lambda i: (i, 0)
    fixed = lambda i: (0, 0)
    tiles = lambda i: (i, 0, 0)
    widths = (A_W, B_W, B_KV_W, C_W, C_W)
    return pl.pallas_call(
        _inproj_kernel,
        grid=(m // tm,),
        in_specs=[
            pl.BlockSpec((tm, d), row),
            pl.BlockSpec((1, d), fixed),
            pl.BlockSpec((d, RM_W), fixed),
            pl.BlockSpec((T_W, d), fixed),
            pl.BlockSpec((1, OFF_GATE), fixed),
            pl.BlockSpec((T_W, 1), fixed),
        ],
        out_specs=[pl.BlockSpec((tm, w), row) for w in widths] + [
            pl.BlockSpec((tm, GATE_W), row),
            pl.BlockSpec((tm // A_TILE, A_W, A_TILE), tiles),
            pl.BlockSpec((tm // B_TILE, B_KV_W, B_TILE), tiles),
            pl.BlockSpec((tm // C_TILE, C_W, C_TILE), tiles),
            pl.BlockSpec((tm // A_TILE, A_W, A_TILE), tiles),
        ],
        out_shape=[jax.ShapeDtypeStruct((m, w), BF16) for w in widths] + [
            jax.ShapeDtypeStruct((m, GATE_W), F32),
            jax.ShapeDtypeStruct((m // A_TILE, A_W, A_TILE), BF16),
            jax.ShapeDtypeStruct((m // B_TILE, B_KV_W, B_TILE), BF16),
            jax.ShapeDtypeStruct((m // C_TILE, C_W, C_TILE), BF16),
            jax.ShapeDtypeStruct((m // A_TILE, A_W, A_TILE), BF16),
        ],
        compiler_params=pltpu.CompilerParams(
            dimension_semantics=("parallel",), vmem_limit_bytes=VMEM_LIMIT_BYTES),
        name="inproj",
    )(x2, g, wrm, wt, cs, ts)


def _attn_a_kernel(lq1_ref, lk1_ref, lq2_ref, lk2_ref, sg_ref, qfeat_ref,
                   qt_ref, k_ref, kfeat_ref, vt_ref, gate_ref, o_ref,
                   qz_sc, m_sc, alpha_sc, acc_sc, s_sc, smax_sc, p_sc, *, lambda_init):
    t = A_TILE
    tk = A_KEYS
    qi = pl.program_id(1)
    n_full = qi // 2
    m_sc[...] = jnp.full(m_sc.shape, MASK_VALUE, F32)
    acc_sc[...] = jnp.zeros(acc_sc.shape, F32)
    ones_rows = jnp.ones((A_NORM_ROWS, tk), BF16)

    sub = lax.broadcasted_iota(jnp.int32, (LANES, t), 0)
    for pr in range(2):
        qp = qt_ref[pr * LANES:(pr + 1) * LANES, :]
        for c in range(4):
            hh, mp = divmod(c, 2)
            lo = hh * HEAD_DIM + mp * A_QK_DIM
            real = jnp.where((sub >= lo) & (sub < lo + A_QK_DIM), qp, jnp.zeros_like(qp))
            h = 2 * pr + hh
            feat = jnp.broadcast_to(qfeat_ref[:, h:h + 1], (LANES, t)).astype(BF16)
            qz_sc[pr, :, c * t:(c + 1) * t] = jnp.concatenate([real, feat], axis=0)

    def key_rows(pr, step):
        ks = pl.ds(pl.multiple_of(step * tk, tk), tk)
        return jnp.concatenate([k_ref[ks, pr * LANES:(pr + 1) * LANES], kfeat_ref[ks, :]], axis=1)

    def scores(pr, c, lhs):
        return jnp.dot(lhs, qz_sc[pr, :, c * t:(c + 1) * t], preferred_element_type=F32)

    def softmax_block(pr, c):
        cols = slice(c * t, (c + 1) * t)
        m_old = m_sc[pr, :, cols]
        m_new = jnp.maximum(m_old, smax_sc[pr, :, cols])
        m_sc[pr, :, cols] = m_new
        return jnp.exp2(m_old - m_new), jnp.exp2((s_sc[pr, :, cols] - m_new).astype(BF16))

    def value_block(pr, c, step):
        hh, mp = divmod(c, 2)
        h = 2 * pr + hh
        rows = slice(h * HEAD_DIM, (h + 1) * HEAD_DIM)
        cols = slice(c * t, (c + 1) * t)
        ocols = slice(mp * t, (mp + 1) * t)
        vt = jnp.concatenate([vt_ref[2 * step, rows, :], vt_ref[2 * step + 1, rows, :]], axis=1)
        vt1 = jnp.concatenate([vt, ones_rows], axis=0)
        pv = jnp.dot(vt1, p_sc[pr, :, cols], preferred_element_type=F32)
        acc_sc[h, :, ocols] = alpha_sc[pr, :, cols] * acc_sc[h, :, ocols] + pv

    def causal_mask():
        kmq = (lax.broadcasted_iota(jnp.int32, (tk, t), 0)
               - lax.broadcasted_iota(jnp.int32, (tk, t), 1))
        return kmq <= qi * t - n_full * tk

    def stage_scores(pr, c, s):
        cols = slice(c * t, (c + 1) * t)
        s_sc[pr, :, cols] = s
        smax_sc[pr, :, cols] = jnp.max(s, axis=0, keepdims=True)

    def trip(i, masked_next, has_next):
        for pr in range(2):
            lhs = key_rows(pr, i + 1) if has_next else None
            for c in range(4):
                cols = slice(c * t, (c + 1) * t)
                value_block(pr, c, jnp.maximum(i - 1, 0))
                alpha, p = softmax_block(pr, c)
                if has_next:
                    sn = scores(pr, c, lhs)
                    if masked_next:
                        sn = jnp.where(causal_mask(), sn, MASK_VALUE)
                    stage_scores(pr, c, sn)
                alpha_sc[pr, :, cols] = alpha
                p_sc[pr, :, cols] = p

    p_sc[...] = jnp.zeros(p_sc.shape, BF16)
    alpha_sc[...] = jnp.ones(alpha_sc.shape, F32)
    keep = jnp.logical_or(causal_mask(), n_full > 0)
    for pr in range(2):
        lhs0 = key_rows(pr, 0)
        for c in range(4):
            stage_scores(pr, c, jnp.where(keep, scores(pr, c, lhs0), MASK_VALUE))

    def body(i, carry):
        trip(i, False, True)
        return carry

    lax.fori_loop(0, n_full - 1, body, 0)

    @pl.when(n_full > 0)
    def _():
        trip(n_full - 1, True, True)

    trip(n_full, False, False)
    for pr in range(2):
        for c in range(4):
            value_block(pr, c, n_full)

    lam = (jnp.exp(jnp.sum(lq1_ref[...] * lk1_ref[...], axis=-1, keepdims=True))
           - jnp.exp(jnp.sum(lq2_ref[...] * lk2_ref[...], axis=-1, keepdims=True))
           + lambda_init)
    outs = []
    for h in range(A_HEADS):
        pr, hh = divmod(h, 2)
        acc = acc_sc[h, :HEAD_DIM, :]
        l = acc_sc[h, HEAD_DIM:HEAD_DIM + 1, :]
        o = acc[:, :t] / l[:, :t] - lam * (acc[:, t:] / l[:, t:])
        ms = jnp.mean(o * o, axis=0, keepdims=True)
        outs.append(o * lax.rsqrt(ms + NORM_EPS))
    o_all = jnp.concatenate(outs, axis=0) * sg_ref[...] * (1.0 - lambda_init)
    o_ref[...] = (o_all.T * _silu(gate_ref[...])).astype(BF16)


def _attn_a(aqt, ak, kfeat, avt, gate, qfeat, lq1, lk1, lq2, lk2, sg4, lambda_init, b, s):
    t = A_TILE
    tk = A_KEYS
    nt = s // t
    small = lambda shape: pl.BlockSpec(shape, lambda bi, qi: (0, 0))
    return pl.pallas_call(
        functools.partial(_attn_a_kernel, lambda_init=lambda_init),
        grid=(b, nt),
        in_specs=[
            small((1, A_QK_DIM)), small((1, A_QK_DIM)), small((1, A_QK_DIM)), small((1, A_QK_DIM)),
            small((A_W, 1)),
            small((LANES, A_HEADS)),
            pl.BlockSpec((None, A_W, t), lambda bi, qi: (bi * nt + qi, 0, 0)),
            pl.BlockSpec((s, A_W), lambda bi, qi: (bi, 0)),
            pl.BlockSpec((s, LANES), lambda bi, qi: (0, 0)),
            pl.BlockSpec((nt, A_W, t), lambda bi, qi: (bi, 0, 0)),
            pl.BlockSpec((t, A_W), lambda bi, qi: (bi * nt + qi, GATE_BLOCK_A)),
        ],
        out_specs=pl.BlockSpec((t, A_W), lambda bi, qi: (bi * nt + qi, 0)),
        out_shape=jax.ShapeDtypeStruct((b * s, A_W), BF16),
        scratch_shapes=[
            pltpu.VMEM((2, 2 * LANES, 4 * t), BF16),
            pltpu.VMEM((2, 1, 4 * t), F32),
            pltpu.VMEM((2, 1, 4 * t), F32),
            pltpu.VMEM((A_HEADS, HEAD_DIM + A_NORM_ROWS, 2 * t), F32),
            pltpu.VMEM((2, tk, 4 * t), F32),
            pltpu.VMEM((2, 1, 4 * t), F32),
            pltpu.VMEM((2, tk, 4 * t), BF16),
        ],
        compiler_params=pltpu.CompilerParams(
            dimension_semantics=("parallel", "arbitrary"),
            vmem_limit_bytes=VMEM_LIMIT_BYTES),
        name="attn_a",
    )(lq1, lk1, lq2, lk2, sg4, qfeat, aqt, ak, kfeat, avt, gate)


def _attn_b_kernel(q_ref, kp_ref, kc_ref, vtp_ref, vtc_ref, gate_ref, bias_ref, sink_ref, o_ref):
    t = B_TILE
    qi = pl.program_id(1)
    lane = lax.broadcasted_iota(jnp.int32, (t, LANES), 1)
    sink = sink_ref[...]
    k_blocks = [kp_ref[...]] + [kc_ref[c * t:(c + 1) * t, :] for c in range(B_BLOCKS)]
    vt_blocks = [vtp_ref[...]] + [vtc_ref[c] for c in range(B_BLOCKS)]
    for c in range(B_BLOCKS):
        rows = slice(c * t, (c + 1) * t)
        k_all = jnp.concatenate(k_blocks[c:c + 2], axis=0)
        vt_all = jnp.concatenate(vt_blocks[c:c + 2], axis=1)
        pieces = [None] * B_HEADS
        for j in range(B_GROUP):
            qp = q_ref[rows, j * LANES:(j + 1) * LANES]
            zero = jnp.zeros_like(qp)
            pieces[j] = jnp.where(lane < HEAD_DIM, qp, zero)
            pieces[B_GROUP + j] = jnp.where(lane >= HEAD_DIM, qp, zero)
        qz = jnp.concatenate(pieces, axis=0)
        bias = bias_ref[jnp.minimum(qi, 1)] if c == 0 else bias_ref[1]
        s = _dot_nt(k_all, qz) + bias
        m = jnp.maximum(jnp.max(s, axis=0, keepdims=True), sink)
        p = jnp.exp(s - m)
        inv = 1.0 / (jnp.sum(p, axis=0, keepdims=True) + jnp.exp(sink - m))
        pb = p.astype(BF16)
        outs = []
        for g in range(B_KV_HEADS):
            cols = slice(g * B_GROUP * t, (g + 1) * B_GROUP * t)
            og = jnp.dot(vt_all[g * HEAD_DIM:(g + 1) * HEAD_DIM, :], pb[:, cols],
                         preferred_element_type=F32) * inv[:, cols]
            outs.extend(og[:, j * t:(j + 1) * t] for j in range(B_GROUP))
        o_all = jnp.concatenate(outs, axis=0)
        o_ref[rows, :] = (o_all.T * _silu(gate_ref[rows, :])).astype(BF16)


def _attn_b(bq, bk, bvt, gate, bias, sink_row, b, s):
    t = B_TILE
    tq = B_BLOCKS * t
    nt = s // tq
    cur = lambda bi, qi: (bi * nt + qi, 0)
    prev = lambda bi, qi: (jnp.maximum((bi * nt + qi) * B_BLOCKS - 1, 0), 0)
    cur3 = lambda bi, qi: (bi * nt + qi, 0, 0)
    prev3 = lambda bi, qi: (jnp.maximum((bi * nt + qi) * B_BLOCKS - 1, 0), 0, 0)
    return pl.pallas_call(
        _attn_b_kernel,
        grid=(b, nt),
        in_specs=[
            pl.BlockSpec((tq, B_W), cur),
            pl.BlockSpec((t, B_KV_W), prev),
            pl.BlockSpec((tq, B_KV_W), cur),
            pl.BlockSpec((None, B_KV_W, t), prev3),
            pl.BlockSpec((B_BLOCKS, B_KV_W, t), cur3),
            pl.BlockSpec((tq, B_W), cur),
            pl.BlockSpec((2, 2 * t, B_HEADS * t), lambda bi, qi: (0, 0, 0)),
            pl.BlockSpec((1, B_HEADS * t), lambda bi, qi: (0, 0)),
        ],
        out_specs=pl.BlockSpec((tq, B_W), cur),
        out_shape=jax.ShapeDtypeStruct((b * s, B_W), BF16),
        compiler_params=pltpu.CompilerParams(
            dimension_semantics=("parallel", "arbitrary"),
            vmem_limit_bytes=VMEM_LIMIT_BYTES),
        name="attn_b",
    )(bq, bk, bk, bvt, bvt, gate, bias, sink_row)


def _b_bias(slopes):
    t = B_TILE
    koff = np.arange(2 * t)[:, None] - t
    qoff = np.arange(t)[None, :]
    dist = qoff - koff
    band = (dist >= 0) & (dist < WINDOW)
    out = np.empty((2, 2 * t, B_HEADS * t), np.float32)
    for h in range(B_HEADS):
        term = (-np.float32(slopes[h]) * dist.astype(np.float32)).astype(np.float32)
        out[1, :, h * t:(h + 1) * t] = np.where(band, term, np.float32(MASK_VALUE))
        out[0, :, h * t:(h + 1) * t] = np.where(band & (koff >= 0), term, np.float32(MASK_VALUE))
    return jnp.asarray(out)


def _attn_c_kernel(q_ref, k_ref, vt_ref, u_ref, gate_ref, o_ref, qz_sc, carry_sc, acc_sc):
    t = C_TILE
    qi = pl.program_id(1)
    q = q_ref[...]
    lane = lax.broadcasted_iota(jnp.int32, (t, C_W), 1)
    for h in range(C_HEADS):
        own = (lane >= h * HEAD_DIM) & (lane < (h + 1) * HEAD_DIM)
        qz_sc[h * t:(h + 1) * t, :] = jnp.where(own, q, jnp.zeros_like(q))
    u2 = u_ref[...]

    def tile(kj, masked, carry_in):
        ks = pl.ds(pl.multiple_of(kj * t, t), t)
        z = _dot_nt(k_ref[ks, :], qz_sc[...])
        lse = jnp.log2(1.0 + jnp.exp2(jnp.minimum(z, -z)))
        log_beta = jnp.minimum(z, 0.0) - lse
        l1m = log_beta - z
        if masked:
            krow = lax.broadcasted_iota(jnp.int32, (t, C_HEADS * t), 0)
            qcol = lax.broadcasted_iota(jnp.int32, (t, C_HEADS * t), 1) & (t - 1)
            past = krow < qcol
            l1m = jnp.where(past, l1m, 0.0)
        hi = l1m.astype(BF16)
        lo = (l1m - hi.astype(F32)).astype(BF16)
        tail = jnp.dot(u2, jnp.concatenate([hi, lo], axis=0),
                       preferred_element_type=F32)
        a = jnp.exp2(log_beta + tail + carry_in)
        if masked:
            a = jnp.where(past, a, 0.0)
        a = a.astype(BF16)
        parts = [jnp.dot(vt_ref[kj, h * HEAD_DIM:(h + 1) * HEAD_DIM, :], a[:, h * t:(h + 1) * t],
                         preferred_element_type=F32) for h in range(C_HEADS)]
        return jnp.concatenate(parts, axis=0), carry_in + jnp.sum(l1m, axis=0, keepdims=True)

    no_carry = jnp.zeros((1, C_HEADS * t), F32)

    @pl.when(qi == 0)
    def _():
        acc, carry = tile(qi, True, no_carry)
        acc_sc[...] = acc
        carry_sc[...] = carry

    @pl.when(qi > 0)
    def _():
        acc_d, carry_d = tile(qi, True, no_carry)
        acc_p, carry_p = tile(qi - 1, False, carry_d)
        acc_sc[...] = acc_d + acc_p
        carry_sc[...] = carry_p

    def cond(state):
        i, go = state
        return jnp.logical_and(i < qi, go)

    def body(state):
        i, _ = state
        acc, carry = tile(qi - 1 - i, False, carry_sc[...])
        acc_sc[...] += acc
        carry_sc[...] = carry
        return i + 1, jnp.max(carry) > C_ZERO_WEIGHT_LOG2

    lax.while_loop(cond, body, (jnp.int32(1), jnp.max(carry_sc[...]) > C_ZERO_WEIGHT_LOG2))
    o_ref[...] = (acc_sc[...].T * _silu(gate_ref[...])).astype(BF16)


def _attn_c(cq, ck, cvt, u2, gate, b, s):
    t = C_TILE
    nt = s // t
    return pl.pallas_call(
        _attn_c_kernel,
        grid=(b, nt),
        in_specs=[
            pl.BlockSpec((t, C_W), lambda bi, qi: (bi * nt + qi, 0)),
            pl.BlockSpec((s, C_W), lambda bi, qi: (bi, 0)),
            pl.BlockSpec((nt, C_W, t), lambda bi, qi: (bi, 0, 0)),
            pl.BlockSpec((t, 2 * t), lambda bi, qi: (0, 0)),
            pl.BlockSpec((t, C_W), lambda bi, qi: (bi * nt + qi, GATE_BLOCK_C)),
        ],
        out_specs=pl.BlockSpec((t, C_W), lambda bi, qi: (bi * nt + qi, 0)),
        out_shape=jax.ShapeDtypeStruct((b * s, C_W), BF16),
        scratch_shapes=[
            pltpu.VMEM((C_HEADS * t, C_W), BF16),
            pltpu.VMEM((1, C_HEADS * t), F32),
            pltpu.VMEM((C_W, t), F32),
        ],
        compiler_params=pltpu.CompilerParams(
            dimension_semantics=("parallel", "arbitrary"),
            vmem_limit_bytes=VMEM_LIMIT_BYTES),
        name="attn_c",
    )(cq, ck, cvt, u2, gate)


def _outproj_kernel(ya_ref, yb_ref, yc_ref, x_ref, w_ref, fg_ref, o_ref, *, final_norm):
    y = jnp.concatenate([ya_ref[...], yb_ref[...], yc_ref[...]], axis=1)
    r = x_ref[...] + jnp.dot(y, w_ref[...], preferred_element_type=F32)
    if final_norm:
        ms = jnp.mean(r * r, axis=-1, keepdims=True)
        r = r * lax.rsqrt(ms + NORM_EPS) * fg_ref[...]
    o_ref[...] = r


def _outproj(ya, yb, yc, x2, w, fg, final_norm):
    m, d = x2.shape
    row = lambda i: (i, 0)
    fixed = lambda i: (0, 0)
    return pl.pallas_call(
        functools.partial(_outproj_kernel, final_norm=final_norm),
        grid=(m // OUT_ROW_TILE,),
        in_specs=[
            pl.BlockSpec((OUT_ROW_TILE, A_W), row),
            pl.BlockSpec((OUT_ROW_TILE, B_W), row),
            pl.BlockSpec((OUT_ROW_TILE, C_W), row),
            pl.BlockSpec((OUT_ROW_TILE, d), row),
            pl.BlockSpec((GATE_W, d), fixed),
            pl.BlockSpec((1, d), fixed),
        ],
        out_specs=pl.BlockSpec((OUT_ROW_TILE, d), row),
        out_shape=jax.ShapeDtypeStruct((m, d), F32),
        compiler_params=pltpu.CompilerParams(
            dimension_semantics=("parallel",), vmem_limit_bytes=VMEM_LIMIT_BYTES),
        name="outproj",
    )(ya, yb, yc, x2, w, fg)


def _split_w_in(w):
    splits = (A_W, A_W, A_W, A_W, B_W, B_KV_W, B_KV_W, B_W, C_W, C_W, C_W, C_W)
    pts = np.cumsum(splits)[:-1]
    aq, ak, av, ag, bq, bk, bv, bg, cq, ck, cv, cg = jnp.split(w, pts, axis=-1)
    d = w.shape[0]
    bq = bq.reshape(d, B_KV_HEADS, B_GROUP, HEAD_DIM).transpose(0, 2, 1, 3).reshape(d, B_W)
    wrm = jnp.concatenate([ak, bq, bk, cq, ck, bg, ag, cg], axis=-1)
    wt = jnp.concatenate([av, bv, cv, aq], axis=-1).T
    return wrm.astype(BF16), wt.astype(BF16)


def _q_scales():
    cs = np.ones((1, OFF_GATE), np.float32)
    cs[0, OFF_BQ:OFF_BK] = HEAD_DIM ** -0.5
    cs[0, OFF_CQ:OFF_CK] = HEAD_DIM ** -0.5 * LOG2_E
    ts = np.ones((T_W, 1), np.float32)
    ts[TOFF_AQ:, 0] = A_QK_DIM ** -0.5 * LOG2_E
    return jnp.asarray(cs), jnp.asarray(ts)


def kernel(x, norm_g, w_in, lambda_q1, lambda_k1, lambda_q2, lambda_k2, subln_g, sinks, w_out, final_g):
    b, s, d = x.shape
    depth = w_in.shape[0]
    m = b * s
    slopes_a, slopes_b = _alibi_slopes()

    pos = jnp.arange(s, dtype=jnp.int32)
    p_hi = (pos // 256).astype(BF16)
    p_lo = (pos % 256).astype(BF16)
    kfeat = jnp.stack([p_hi] * 3 + [p_lo] * 3, axis=-1)
    kfeat = jnp.pad(kfeat, ((0, 0), (0, LANES - 6)))
    qfeat_np = np.zeros((LANES, A_HEADS), np.float32)
    for h, sl in enumerate(slopes_a):
        sl2 = np.float32(np.float32(sl) * np.float32(LOG2_E))
        qfeat_np[:6, h] = _split3_bf16(np.float32(256.0) * sl2) + _split3_bf16(sl2)
    qfeat = jnp.asarray(qfeat_np)

    tri = (lax.broadcasted_iota(jnp.int32, (C_TILE, C_TILE), 1)
           > lax.broadcasted_iota(jnp.int32, (C_TILE, C_TILE), 0)).astype(BF16)
    tri2 = jnp.concatenate([tri, tri], axis=1)
    b_bias = _b_bias(slopes_b)
    cs, ts = _q_scales()

    x2 = x.reshape(m, d)
    for l in range(depth):
        lambda_init = 0.8 - 0.6 * math.exp(-0.3 * l)
        wrm, wt = _split_w_in(w_in[l])
        ak, bq, bk, cq, ck, gate, avt, bvt, cvt, aqt = _inproj(
            x2, norm_g[l].reshape(1, d), wrm, wt, cs, ts)
        ya = _attn_a(aqt, ak, kfeat, avt, gate, qfeat,
                     lambda_q1[l].reshape(1, -1), lambda_k1[l].reshape(1, -1),
                     lambda_q2[l].reshape(1, -1), lambda_k2[l].reshape(1, -1),
                     jnp.tile(subln_g[l], A_HEADS).reshape(-1, 1), lambda_init, b, s)
        sink_row = jnp.repeat(sinks[l], B_TILE).reshape(1, -1)
        yb = _attn_b(bq, bk, bvt, gate, b_bias, sink_row, b, s)
        yc = _attn_c(cq, ck, cvt, tri2, gate, b, s)
        x2 = _outproj(ya, yb, yc, x2, w_out[l].astype(BF16), final_g.reshape(1, d),
                      final_norm=(l == depth - 1))
    return x2.reshape(b, s, d)
```

```python
import functools
import math

import numpy as np
import jax
import jax.numpy as jnp
from jax import lax
from jax.experimental import pallas as pl
from jax.experimental.pallas import tpu as pltpu

F32 = jnp.float32
BF16 = jnp.bfloat16

HEAD_DIM = 64
NORM_EPS = 1e-5
A_HEADS = 4
A_QK_DIM = HEAD_DIM // 2
B_HEADS = 8
B_KV_HEADS = 2
B_GROUP = B_HEADS // B_KV_HEADS
C_HEADS = 4
WINDOW = 128
A_W = A_HEADS * HEAD_DIM
B_W = B_HEADS * HEAD_DIM
B_KV_W = B_KV_HEADS * HEAD_DIM
C_W = C_HEADS * HEAD_DIM
GATE_W = A_W + B_W + C_W
LANES = 128
MASK_VALUE = -1e30
LOG2_E = math.log2(math.e)
C_ZERO_WEIGHT_LOG2 = -105.0 * LOG2_E
VMEM_LIMIT_BYTES = 48 * 1024 * 1024

ROW_TILE = 512
OUT_ROW_TILE = 1024
A_TILE = 256
A_KEYS = 2 * A_TILE
A_NORM_ROWS = 16
C_TILE = 256
B_TILE = WINDOW
B_BLOCKS = 2

OFF_AK = 0
OFF_BQ = OFF_AK + A_W
OFF_BK = OFF_BQ + B_W
OFF_CQ = OFF_BK + B_KV_W
OFF_CK = OFF_CQ + C_W
OFF_GATE = OFF_CK + C_W
RM_W = OFF_GATE + GATE_W
TOFF_AV = 0
TOFF_BV = TOFF_AV + A_W
TOFF_CV = TOFF_BV + B_KV_W
TOFF_AQ = TOFF_CV + C_W
T_W = TOFF_AQ + A_W
GATE_BLOCK_A = B_W // A_W
GATE_BLOCK_C = (B_W + A_W) // C_W


def _alibi_slopes():
    n = A_HEADS + B_HEADS
    s = (2.0 ** (-8.0 * np.arange(1, n + 1) / n)).astype(np.float32)
    return s[B_HEADS:], s[:B_HEADS]


def _split3_bf16(v):
    out = []
    r = np.float32(v)
    for _ in range(3):
        p = np.asarray(r, np.float32).astype(BF16).astype(np.float32)
        out.append(float(p))
        r = np.float32(r - p)
    return out


def _dot_nt(a, b):
    return lax.dot_general(a, b, (((1,), (1,)), ((), ())), preferred_element_type=F32)


def _silu(g):
    return g * jax.nn.sigmoid(g)


def _inproj_kernel(x_ref, g_ref, wrm_ref, wt_ref, cs_ref, ts_ref,
                   ak_ref, bq_ref, bk_ref, cq_ref, ck_ref, gate_ref,
                   avt_ref, bvt_ref, cvt_ref, aqt_ref):
    x = x_ref[...]
    ms = jnp.mean(x * x, axis=-1, keepdims=True)
    h = (x * lax.rsqrt(ms + NORM_EPS) * g_ref[...]).astype(BF16)
    qk = jnp.dot(h, wrm_ref[:, :OFF_GATE], preferred_element_type=F32) * cs_ref[...]
    ak_ref[...] = qk[:, OFF_AK:OFF_BQ].astype(BF16)
    bq_ref[...] = qk[:, OFF_BQ:OFF_BK].astype(BF16)
    bk_ref[...] = qk[:, OFF_BK:OFF_CQ].astype(BF16)
    cq_ref[...] = qk[:, OFF_CQ:OFF_CK].astype(BF16)
    ck_ref[...] = qk[:, OFF_CK:OFF_GATE].astype(BF16)
    gate_ref[...] = jnp.dot(h, wrm_ref[:, OFF_GATE:], preferred_element_type=F32)
    tr = (_dot_nt(wt_ref[...], h) * ts_ref[...]).astype(BF16)
    for c in range(ROW_TILE // A_TILE):
        avt_ref[c] = tr[TOFF_AV:TOFF_BV, c * A_TILE:(c + 1) * A_TILE]
        aqt_ref[c] = tr[TOFF_AQ:, c * A_TILE:(c + 1) * A_TILE]
    for c in range(ROW_TILE // B_TILE):
        bvt_ref[c] = tr[TOFF_BV:TOFF_CV, c * B_TILE:(c + 1) * B_TILE]
    for c in range(ROW_TILE // C_TILE):
        cvt_ref[c] = tr[TOFF_CV:TOFF_AQ, c * C_TILE:(c + 1) * C_TILE]


def _inproj(x2, g, wrm, wt, cs, ts, layer):
    m, d = x2.shape
    tm = ROW_TILE
    row = lambda i: (i, 0)
    fixed = lambda i: (0, 0)
    tiles = lambda i: (i, 0, 0)
    this_layer = lambda i: (layer, 0, 0)
    widths = (A_W, B_W, B_KV_W, C_W, C_W)
    return pl.pallas_call(
        _inproj_kernel,
        grid=(m // tm,),
        in_specs=[
            pl.BlockSpec((tm, d), row),
            pl.BlockSpec((1, d), fixed),
            pl.BlockSpec((None, d, RM_W), this_layer),
            pl.BlockSpec((None, T_W, d), this_layer),
            pl.BlockSpec((1, OFF_GATE), fixed),
            pl.BlockSpec((T_W, 1), fixed),
        ],
        out_specs=[pl.BlockSpec((tm, w), row) for w in widths] + [
            pl.BlockSpec((tm, GATE_W), row),
            pl.BlockSpec((tm // A_TILE, A_W, A_TILE), tiles),
            pl.BlockSpec((tm // B_TILE, B_KV_W, B_TILE), tiles),
            pl.BlockSpec((tm // C_TILE, C_W, C_TILE), tiles),
            pl.BlockSpec((tm // A_TILE, A_W, A_TILE), tiles),
        ],
        out_shape=[jax.ShapeDtypeStruct((m, w), BF16) for w in widths] + [
            jax.ShapeDtypeStruct((m, GATE_W), F32),
            jax.ShapeDtypeStruct((m // A_TILE, A_W, A_TILE), BF16),
            jax.ShapeDtypeStruct((m // B_TILE, B_KV_W, B_TILE), BF16),
            jax.ShapeDtypeStruct((m // C_TILE, C_W, C_TILE), BF16),
            jax.ShapeDtypeStruct((m // A_TILE, A_W, A_TILE), BF16),
        ],
        compiler_params=pltpu.CompilerParams(
            dimension_semantics=("parallel",), vmem_limit_bytes=VMEM_LIMIT_BYTES),
        name="inproj",
    )(x2, g, wrm, wt, cs, ts)


def _attn_a_kernel(lq1_ref, lk1_ref, lq2_ref, lk2_ref, sg_ref, qfeat_ref,
                   qt_ref, k_ref, kfeat_ref, vt_ref, gate_ref, o_ref,
                   qz_sc, m_sc, alpha_sc, acc_sc, s_sc, smax_sc, p_sc, *, lambda_init):
    t = A_TILE
    tk = A_KEYS
    qi = pl.program_id(1)
    n_full = qi // 2
    m_sc[...] = jnp.full(m_sc.shape, MASK_VALUE, F32)
    acc_sc[...] = jnp.zeros(acc_sc.shape, F32)
    ones_rows = jnp.ones((A_NORM_ROWS, tk), BF16)

    sub = lax.broadcasted_iota(jnp.int32, (LANES, t), 0)
    for pr in range(2):
        qp = qt_ref[pr * LANES:(pr + 1) * LANES, :]
        for c in range(4):
            hh, mp = divmod(c, 2)
            lo = hh * HEAD_DIM + mp * A_QK_DIM
            real = jnp.where((sub >= lo) & (sub < lo + A_QK_DIM), qp, jnp.zeros_like(qp))
            h = 2 * pr + hh
            feat = jnp.broadcast_to(qfeat_ref[:, h:h + 1], (LANES, t)).astype(BF16)
            qz_sc[pr, :, c * t:(c + 1) * t] = jnp.concatenate([real, feat], axis=0)

    def key_rows(pr, step, nk):
        ks = pl.ds(pl.multiple_of(step * tk, tk), nk)
        return jnp.concatenate([k_ref[ks, pr * LANES:(pr + 1) * LANES], kfeat_ref[ks, :]], axis=1)

    def scores(pr, c, lhs):
        return jnp.dot(lhs, qz_sc[pr, :, c * t:(c + 1) * t], preferred_element_type=F32)

    def softmax_block(pr, c, nk):
        cols = slice(c * t, (c + 1) * t)
        m_old = m_sc[pr, :, cols]
        m_new = jnp.maximum(m_old, smax_sc[pr, :, cols])
        m_sc[pr, :, cols] = m_new
        return jnp.exp2(m_old - m_new), jnp.exp2((s_sc[pr, :nk, cols] - m_new).astype(BF16))

    def value_block(pr, c, step, nk):
        hh, mp = divmod(c, 2)
        h = 2 * pr + hh
        rows = slice(h * HEAD_DIM, (h + 1) * HEAD_DIM)
        cols = slice(c * t, (c + 1) * t)
        ocols = slice(mp * t, (mp + 1) * t)
        vt = jnp.concatenate([vt_ref[2 * step + blk, rows, :] for blk in range(nk // t)], axis=1)
        vt1 = jnp.concatenate([vt, ones_rows[:, :nk]], axis=0)
        pv = jnp.dot(vt1, p_sc[pr, :nk, cols], preferred_element_type=F32)
        acc_sc[h, :, ocols] = alpha_sc[pr, :, cols] * acc_sc[h, :, ocols] + pv

    def causal_mask(nk):
        kmq = (lax.broadcasted_iota(jnp.int32, (nk, t), 0)
               - lax.broadcasted_iota(jnp.int32, (nk, t), 1))
        return kmq <= qi * t - n_full * tk

    def stage_scores(pr, c, s):
        cols = slice(c * t, (c + 1) * t)
        s_sc[pr, :s.shape[0], cols] = s
        smax_sc[pr, :, cols] = jnp.max(s, axis=0, keepdims=True)

    def trip(i, nk_cur=tk, nk_next=tk, masked_next=False):
        for pr in range(2):
            lhs = key_rows(pr, i + 1, nk_next) if nk_next else None
            for c in range(4):
                cols = slice(c * t, (c + 1) * t)
                value_block(pr, c, jnp.maximum(i - 1, 0), tk)
                alpha, p = softmax_block(pr, c, nk_cur)
                if nk_next:
                    sn = scores(pr, c, lhs)
                    if masked_next:
                        sn = jnp.where(causal_mask(nk_next), sn, MASK_VALUE)
                    stage_scores(pr, c, sn)
                alpha_sc[pr, :, cols] = alpha
                p_sc[pr, :nk_cur, cols] = p

    p_sc[...] = jnp.zeros(p_sc.shape, BF16)
    alpha_sc[...] = jnp.ones(alpha_sc.shape, F32)
    keep = jnp.logical_or(causal_mask(tk), n_full > 0)
    for pr in range(2):
        lhs0 = key_rows(pr, 0, tk)
        for c in range(4):
            stage_scores(pr, c, jnp.where(keep, scores(pr, c, lhs0), MASK_VALUE))

    def body(i, carry):
        trip(i)
        return carry

    lax.fori_loop(0, n_full - 1, body, 0)

    short = jnp.logical_and(qi % 2 == 0, n_full > 0)

    def tail(nk):
        @pl.when(n_full > 0)
        def _():
            trip(n_full - 1, nk_next=nk, masked_next=True)

        trip(n_full, nk_cur=nk, nk_next=0)
        for pr in range(2):
            for c in range(4):
                value_block(pr, c, n_full, nk)

    @pl.when(short)
    def _():
        tail(t)

    @pl.when(jnp.logical_not(short))
    def _():
        tail(tk)

    lam = (jnp.exp(jnp.sum(lq1_ref[...] * lk1_ref[...], axis=-1, keepdims=True))
           - jnp.exp(jnp.sum(lq2_ref[...] * lk2_ref[...], axis=-1, keepdims=True))
           + lambda_init)
    outs = []
    for h in range(A_HEADS):
        pr, hh = divmod(h, 2)
        acc = acc_sc[h, :HEAD_DIM, :]
        l = acc_sc[h, HEAD_DIM:HEAD_DIM + 1, :]
        o = acc[:, :t] / l[:, :t] - lam * (acc[:, t:] / l[:, t:])
        ms = jnp.mean(o * o, axis=0, keepdims=True)
        outs.append(o * lax.rsqrt(ms + NORM_EPS))
    o_all = jnp.concatenate(outs, axis=0) * sg_ref[...] * (1.0 - lambda_init)
    o_ref[...] = (o_all.T * _silu(gate_ref[...])).astype(BF16)


def _attn_a(aqt, ak, kfeat, avt, gate, qfeat, lq1, lk1, lq2, lk2, sg4, lambda_init, b, s):
    t = A_TILE
    tk = A_KEYS
    nt = s // t
    small = lambda shape: pl.BlockSpec(shape, lambda bi, qi: (0, 0))
    return pl.pallas_call(
        functools.partial(_attn_a_kernel, lambda_init=lambda_init),
        grid=(b, nt),
        in_specs=[
            small((1, A_QK_DIM)), small((1, A_QK_DIM)), small((1, A_QK_DIM)), small((1, A_QK_DIM)),
            small((A_W, 1)),
            small((LANES, A_HEADS)),
            pl.BlockSpec((None, A_W, t), lambda bi, qi: (bi * nt + qi, 0, 0)),
            pl.BlockSpec((s, A_W), lambda bi, qi: (bi, 0)),
            pl.BlockSpec((s, LANES), lambda bi, qi: (0, 0)),
            pl.BlockSpec((nt, A_W, t), lambda bi, qi: (bi, 0, 0)),
            pl.BlockSpec((t, A_W), lambda bi, qi: (bi * nt + qi, GATE_BLOCK_A)),
        ],
        out_specs=pl.BlockSpec((t, A_W), lambda bi, qi: (bi * nt + qi, 0)),
        out_shape=jax.ShapeDtypeStruct((b * s, A_W), BF16),
        scratch_shapes=[
            pltpu.VMEM((2, 2 * LANES, 4 * t), BF16),
            pltpu.VMEM((2, 1, 4 * t), F32),
            pltpu.VMEM((2, 1, 4 * t), F32),
            pltpu.VMEM((A_HEADS, HEAD_DIM + A_NORM_ROWS, 2 * t), F32),
            pltpu.VMEM((2, tk, 4 * t), F32),
            pltpu.VMEM((2, 1, 4 * t), F32),
            pltpu.VMEM((2, tk, 4 * t), BF16),
        ],
        compiler_params=pltpu.CompilerParams(
            dimension_semantics=("parallel", "arbitrary"),
            vmem_limit_bytes=VMEM_LIMIT_BYTES),
        name="attn_a",
    )(lq1, lk1, lq2, lk2, sg4, qfeat, aqt, ak, kfeat, avt, gate)


def _attn_b_kernel(q_ref, kp_ref, kc_ref, vtp_ref, vtc_ref, gate_ref, bias_ref, sink_ref, o_ref):
    t = B_TILE
    qi = pl.program_id(1)
    lane = lax.broadcasted_iota(jnp.int32, (t, LANES), 1)
    sink = sink_ref[...]
    k_blocks = [kp_ref[...]] + [kc_ref[c * t:(c + 1) * t, :] for c in range(B_BLOCKS)]
    vt_blocks = [vtp_ref[...]] + [vtc_ref[c] for c in range(B_BLOCKS)]
    for c in range(B_BLOCKS):
        rows = slice(c * t, (c + 1) * t)
        k_all = jnp.concatenate(k_blocks[c:c + 2], axis=0)
        vt_all = jnp.concatenate(vt_blocks[c:c + 2], axis=1)
        pieces = [None] * B_HEADS
        for j in range(B_GROUP):
            qp = q_ref[rows, j * LANES:(j + 1) * LANES]
            zero = jnp.zeros_like(qp)
            pieces[j] = jnp.where(lane < HEAD_DIM, qp, zero)
            pieces[B_GROUP + j] = jnp.where(lane >= HEAD_DIM, qp, zero)
        qz = jnp.concatenate(pieces, axis=0)
        bias = bias_ref[jnp.minimum(qi, 1)] if c == 0 else bias_ref[1]
        s = _dot_nt(k_all, qz) + bias
        m = jnp.maximum(jnp.max(s, axis=0, keepdims=True), sink)
        p = jnp.exp2(s - m)
        inv = 1.0 / (jnp.sum(p, axis=0, keepdims=True) + jnp.exp2(sink - m))
        pb = p.astype(BF16)
        outs = []
        for g in range(B_KV_HEADS):
            cols = slice(g * B_GROUP * t, (g + 1) * B_GROUP * t)
            og = jnp.dot(vt_all[g * HEAD_DIM:(g + 1) * HEAD_DIM, :], pb[:, cols],
                         preferred_element_type=F32) * inv[:, cols]
            outs.extend(og[:, j * t:(j + 1) * t] for j in range(B_GROUP))
        o_all = jnp.concatenate(outs, axis=0)
        o_ref[rows, :] = (o_all.T * _silu(gate_ref[rows, :])).astype(BF16)


def _attn_b(bq, bk, bvt, gate, bias, sink_row, b, s):
    t = B_TILE
    tq = B_BLOCKS * t
    nt = s // tq
    cur = lambda bi, qi: (bi * nt + qi, 0)
    prev = lambda bi, qi: (jnp.maximum((bi * nt + qi) * B_BLOCKS - 1, 0), 0)
    cur3 = lambda bi, qi: (bi * nt + qi, 0, 0)
    prev3 = lambda bi, qi: (jnp.maximum((bi * nt + qi) * B_BLOCKS - 1, 0), 0, 0)
    return pl.pallas_call(
        _attn_b_kernel,
        grid=(b, nt),
        in_specs=[
            pl.BlockSpec((tq, B_W), cur),
            pl.BlockSpec((t, B_KV_W), prev),
            pl.BlockSpec((tq, B_KV_W), cur),
            pl.BlockSpec((None, B_KV_W, t), prev3),
            pl.BlockSpec((B_BLOCKS, B_KV_W, t), cur3),
            pl.BlockSpec((tq, B_W), cur),
            pl.BlockSpec((2, 2 * t, B_HEADS * t), lambda bi, qi: (0, 0, 0)),
            pl.BlockSpec((1, B_HEADS * t), lambda bi, qi: (0, 0)),
        ],
        out_specs=pl.BlockSpec((tq, B_W), cur),
        out_shape=jax.ShapeDtypeStruct((b * s, B_W), BF16),
        compiler_params=pltpu.CompilerParams(
            dimension_semantics=("parallel", "arbitrary"),
            vmem_limit_bytes=VMEM_LIMIT_BYTES),
        name="attn_b",
    )(bq, bk, bk, bvt, bvt, gate, bias, sink_row)


def _b_bias(slopes):
    t = B_TILE
    koff = np.arange(2 * t)[:, None] - t
    qoff = np.arange(t)[None, :]
    dist = qoff - koff
    band = (dist >= 0) & (dist < WINDOW)
    out = np.empty((2, 2 * t, B_HEADS * t), np.float32)
    for h in range(B_HEADS):
        term = (-np.float32(slopes[h]) * dist.astype(np.float32) * np.float32(LOG2_E)).astype(np.float32)
        out[1, :, h * t:(h + 1) * t] = np.where(band, term, np.float32(MASK_VALUE))
        out[0, :, h * t:(h + 1) * t] = np.where(band & (koff >= 0), term, np.float32(MASK_VALUE))
    return jnp.asarray(out)


def _attn_c_kernel(q_ref, k_ref, vt_ref, u_ref, gate_ref, o_ref, qz_sc, carry_sc, acc_sc):
    t = C_TILE
    qi = pl.program_id(1)
    q = q_ref[...]
    lane = lax.broadcasted_iota(jnp.int32, (t, C_W), 1)
    for h in range(C_HEADS):
        own = (lane >= h * HEAD_DIM) & (lane < (h + 1) * HEAD_DIM)
        qz_sc[h * t:(h + 1) * t, :] = jnp.where(own, q, jnp.zeros_like(q))
    u2 = u_ref[...]

    def tile(kj, masked, carry_in):
        ks = pl.ds(pl.multiple_of(kj * t, t), t)
        z = _dot_nt(k_ref[ks, :], qz_sc[...])
        lse = jnp.log2(1.0 + jnp.exp2(jnp.minimum(z, -z)))
        log_beta = jnp.minimum(z, 0.0) - lse
        l1m = log_beta - z
        if masked:
            krow = lax.broadcasted_iota(jnp.int32, (t, C_HEADS * t), 0)
            qcol = lax.broadcasted_iota(jnp.int32, (t, C_HEADS * t), 1) & (t - 1)
            past = krow < qcol
            l1m = jnp.where(past, l1m, 0.0)
        hi = l1m.astype(BF16)
        lo = (l1m - hi.astype(F32)).astype(BF16)
        tail = jnp.dot(u2, jnp.concatenate([hi, lo], axis=0),
                       preferred_element_type=F32)
        a = jnp.exp2(log_beta + tail + carry_in)
        if masked:
            a = jnp.where(past, a, 0.0)
        a = a.astype(BF16)
        parts = [jnp.dot(vt_ref[kj, h * HEAD_DIM:(h + 1) * HEAD_DIM, :], a[:, h * t:(h + 1) * t],
                         preferred_element_type=F32) for h in range(C_HEADS)]
        return jnp.concatenate(parts, axis=0), carry_in + jnp.sum(l1m, axis=0, keepdims=True)

    acc0, carry0 = tile(qi, True, jnp.zeros((1, C_HEADS * t), F32))
    acc_sc[...] = acc0
    carry_sc[...] = carry0

    def cond(state):
        i, go = state
        return jnp.logical_and(i < qi, go)

    def body(state):
        i, _ = state
        acc, carry = tile(qi - 1 - i, False, carry_sc[...])
        acc_sc[...] += acc
        carry_sc[...] = carry
        return i + 1, jnp.max(carry) > C_ZERO_WEIGHT_LOG2

    lax.while_loop(cond, body, (jnp.int32(0), jnp.max(carry0) > C_ZERO_WEIGHT_LOG2))
    o_ref[...] = (acc_sc[...].T * _silu(gate_ref[...])).astype(BF16)


def _attn_c(cq, ck, cvt, u2, gate, b, s):
    t = C_TILE
    nt = s // t
    return pl.pallas_call(
        _attn_c_kernel,
        grid=(b, nt),
        in_specs=[
            pl.BlockSpec((t, C_W), lambda bi, qi: (bi * nt + qi, 0)),
            pl.BlockSpec((s, C_W), lambda bi, qi: (bi, 0)),
            pl.BlockSpec((nt, C_W, t), lambda bi, qi: (bi, 0, 0)),
            pl.BlockSpec((t, 2 * t), lambda bi, qi: (0, 0)),
            pl.BlockSpec((t, C_W), lambda bi, qi: (bi * nt + qi, GATE_BLOCK_C)),
        ],
        out_specs=pl.BlockSpec((t, C_W), lambda bi, qi: (bi * nt + qi, 0)),
        out_shape=jax.ShapeDtypeStruct((b * s, C_W), BF16),
        scratch_shapes=[
            pltpu.VMEM((C_HEADS * t, C_W), BF16),
            pltpu.VMEM((1, C_HEADS * t), F32),
            pltpu.VMEM((C_W, t), F32),
        ],
        compiler_params=pltpu.CompilerParams(
            dimension_semantics=("parallel", "arbitrary"),
            vmem_limit_bytes=VMEM_LIMIT_BYTES),
        name="attn_c",
    )(cq, ck, cvt, u2, gate)


def _outproj_kernel(ya_ref, yb_ref, yc_ref, x_ref, w_ref, fg_ref, o_ref, *, final_norm):
    y = jnp.concatenate([ya_ref[...], yb_ref[...], yc_ref[...]], axis=1)
    r = x_ref[...] + jnp.dot(y, w_ref[...], preferred_element_type=F32)
    if final_norm:
        ms = jnp.mean(r * r, axis=-1, keepdims=True)
        r = r * lax.rsqrt(ms + NORM_EPS) * fg_ref[...]
    o_ref[...] = r


def _outproj(ya, yb, yc, x2, w, fg, layer, final_norm):
    m, d = x2.shape
    row = lambda i: (i, 0)
    fixed = lambda i: (0, 0)
    return pl.pallas_call(
        functools.partial(_outproj_kernel, final_norm=final_norm),
        grid=(m // OUT_ROW_TILE,),
        in_specs=[
            pl.BlockSpec((OUT_ROW_TILE, A_W), row),
            pl.BlockSpec((OUT_ROW_TILE, B_W), row),
            pl.BlockSpec((OUT_ROW_TILE, C_W), row),
            pl.BlockSpec((OUT_ROW_TILE, d), row),
            pl.BlockSpec((None, GATE_W, d), lambda i: (layer, 0, 0)),
            pl.BlockSpec((1, d), fixed),
        ],
        out_specs=pl.BlockSpec((OUT_ROW_TILE, d), row),
        out_shape=jax.ShapeDtypeStruct((m, d), F32),
        compiler_params=pltpu.CompilerParams(
            dimension_semantics=("parallel",), vmem_limit_bytes=VMEM_LIMIT_BYTES),
        name="outproj",
    )(ya, yb, yc, x2, w, fg)


def _split_w_in(w):
    splits = (A_W, A_W, A_W, A_W, B_W, B_KV_W, B_KV_W, B_W, C_W, C_W, C_W, C_W)
    pts = np.cumsum(splits)[:-1]
    aq, ak, av, ag, bq, bk, bv, bg, cq, ck, cv, cg = jnp.split(w, pts, axis=-1)
    depth, d = w.shape[:2]
    bq = bq.reshape(depth, d, B_KV_HEADS, B_GROUP, HEAD_DIM).transpose(0, 1, 3, 2, 4)
    bq = bq.reshape(depth, d, B_W)
    wrm = jnp.concatenate([ak, bq, bk, cq, ck, bg, ag, cg], axis=-1)
    wt = jnp.swapaxes(jnp.concatenate([av, bv, cv, aq], axis=-1), 1, 2)
    return wrm.astype(BF16), wt.astype(BF16)


def _q_scales():
    cs = np.ones((1, OFF_GATE), np.float32)
    cs[0, OFF_BQ:OFF_BK] = HEAD_DIM ** -0.5 * LOG2_E
    cs[0, OFF_CQ:OFF_CK] = HEAD_DIM ** -0.5 * LOG2_E
    ts = np.ones((T_W, 1), np.float32)
    ts[TOFF_AQ:, 0] = A_QK_DIM ** -0.5 * LOG2_E
    return jnp.asarray(cs), jnp.asarray(ts)


def kernel(x, norm_g, w_in, lambda_q1, lambda_k1, lambda_q2, lambda_k2, subln_g, sinks, w_out, final_g):
    b, s, d = x.shape
    depth = w_in.shape[0]
    m = b * s
    slopes_a, slopes_b = _alibi_slopes()

    pos = jnp.arange(s, dtype=jnp.int32)
    p_hi = (pos // 256).astype(BF16)
    p_lo = (pos % 256).astype(BF16)
    kfeat = jnp.stack([p_hi] * 3 + [p_lo] * 3, axis=-1)
    kfeat = jnp.pad(kfeat, ((0, 0), (0, LANES - 6)))
    qfeat_np = np.zeros((LANES, A_HEADS), np.float32)
    for h, sl in enumerate(slopes_a):
        sl2 = np.float32(np.float32(sl) * np.float32(LOG2_E))
        qfeat_np[:6, h] = _split3_bf16(np.float32(256.0) * sl2) + _split3_bf16(sl2)
    qfeat = jnp.asarray(qfeat_np)

    tri = (lax.broadcasted_iota(jnp.int32, (C_TILE, C_TILE), 1)
           > lax.broadcasted_iota(jnp.int32, (C_TILE, C_TILE), 0)).astype(BF16)
    tri2 = jnp.concatenate([tri, tri], axis=1)
    b_bias = _b_bias(slopes_b)
    cs, ts = _q_scales()

    x2 = x.reshape(m, d)
    wrm_all, wt_all = _split_w_in(w_in)
    w_out_all = w_out.astype(BF16)
    sg_all = jnp.tile(subln_g, (1, A_HEADS)).reshape(depth, -1, 1)
    sink_rows = jnp.repeat(sinks * LOG2_E, B_TILE, axis=1).reshape(depth, 1, -1)
    for l in range(depth):
        lambda_init = 0.8 - 0.6 * math.exp(-0.3 * l)
        ak, bq, bk, cq, ck, gate, avt, bvt, cvt, aqt = _inproj(
            x2, norm_g[l].reshape(1, d), wrm_all, wt_all, cs, ts, l)
        ya = _attn_a(aqt, ak, kfeat, avt, gate, qfeat,
                     lambda_q1[l].reshape(1, -1), lambda_k1[l].reshape(1, -1),
                     lambda_q2[l].reshape(1, -1), lambda_k2[l].reshape(1, -1),
                     sg_all[l], lambda_init, b, s)
        yb = _attn_b(bq, bk, bvt, gate, b_bias, sink_rows[l], b, s)
        yc = _attn_c(cq, ck, cvt, tri2, gate, b, s)
        x2 = _outproj(ya, yb, yc, x2, w_out_all, final_g.reshape(1, d), l,
                      final_norm=(l == depth - 1))
    return x2.reshape(b, s, d)
```

```python
import functools
import math

import numpy as np
import jax
import jax.numpy as jnp
from jax import lax
from jax.experimental import pallas as pl
from jax.experimental.pallas import tpu as pltpu

F32 = jnp.float32
BF16 = jnp.bfloat16

HEAD_DIM = 64
NORM_EPS = 1e-5
A_HEADS = 4
A_QK_DIM = HEAD_DIM // 2
B_HEADS = 8
B_KV_HEADS = 2
B_GROUP = B_HEADS // B_KV_HEADS
C_HEADS = 4
WINDOW = 128
A_W = A_HEADS * HEAD_DIM
B_W = B_HEADS * HEAD_DIM
B_KV_W = B_KV_HEADS * HEAD_DIM
C_W = C_HEADS * HEAD_DIM
GATE_W = A_W + B_W + C_W
LANES = 128
MASK_VALUE = -1e30
LOG2_E = math.log2(math.e)
C_ZERO_WEIGHT_LOG2 = -105.0 * LOG2_E
VMEM_LIMIT_BYTES = 48 * 1024 * 1024

ROW_TILE = 512
OUT_ROW_TILE = 1024
A_TILE = 256
A_KEYS = 2 * A_TILE
A_NORM_ROWS = 16
C_TILE = 256
B_TILE = WINDOW
B_BLOCKS = 4

OFF_AK = 0
OFF_BQ = OFF_AK + A_W
OFF_BK = OFF_BQ + B_W
OFF_CQ = OFF_BK + B_KV_W
OFF_CK = OFF_CQ + C_W
OFF_GATE = OFF_CK + C_W
RM_W = OFF_GATE + GATE_W
TOFF_AV = 0
TOFF_BV = TOFF_AV + A_W
TOFF_CV = TOFF_BV + B_KV_W
TOFF_AQ = TOFF_CV + C_W
T_W = TOFF_AQ + A_W
GATE_BLOCK_A = B_W // A_W
GATE_BLOCK_C = (B_W + A_W) // C_W


def _alibi_slopes():
    n = A_HEADS + B_HEADS
    s = (2.0 ** (-8.0 * np.arange(1, n + 1) / n)).astype(np.float32)
    return s[B_HEADS:], s[:B_HEADS]


def _split3_bf16(v):
    out = []
    r = np.float32(v)
    for _ in range(3):
        p = np.asarray(r, np.float32).astype(BF16).astype(np.float32)
        out.append(float(p))
        r = np.float32(r - p)
    return out


def _dot_nt(a, b):
    return lax.dot_general(a, b, (((1,), (1,)), ((), ())), preferred_element_type=F32)


def _silu(g):
    return g * jax.nn.sigmoid(g)


def _inproj_kernel(x_ref, g_ref, wrm_ref, wt_ref, cs_ref, ts_ref,
                   ak_ref, bq_ref, bk_ref, cq_ref, ck_ref, gate_ref,
                   avt_ref, bvt_ref, cvt_ref, aqt_ref):
    x = x_ref[...]
    ms = jnp.mean(x * x, axis=-1, keepdims=True)
    h = (x * lax.rsqrt(ms + NORM_EPS) * g_ref[...]).astype(BF16)
    qk = jnp.dot(h, wrm_ref[:, :OFF_GATE], preferred_element_type=F32) * cs_ref[...]
    ak_ref[...] = qk[:, OFF_AK:OFF_BQ].astype(BF16)
    bq_ref[...] = qk[:, OFF_BQ:OFF_BK].astype(BF16)
    bk_ref[...] = qk[:, OFF_BK:OFF_CQ].astype(BF16)
    cq_ref[...] = qk[:, OFF_CQ:OFF_CK].astype(BF16)
    ck_ref[...] = qk[:, OFF_CK:OFF_GATE].astype(BF16)
    gate_ref[...] = jnp.dot(h, wrm_ref[:, OFF_GATE:], preferred_element_type=F32)
    tr = (_dot_nt(wt_ref[...], h) * ts_ref[...]).astype(BF16)
    for c in range(ROW_TILE // A_TILE):
        avt_ref[c] = tr[TOFF_AV:TOFF_BV, c * A_TILE:(c + 1) * A_TILE]
        aqt_ref[c] = tr[TOFF_AQ:, c * A_TILE:(c + 1) * A_TILE]
    for c in range(ROW_TILE // B_TILE):
        bvt_ref[c] = tr[TOFF_BV:TOFF_CV, c * B_TILE:(c + 1) * B_TILE]
    for c in range(ROW_TILE // C_TILE):
        cvt_ref[c] = tr[TOFF_CV:TOFF_AQ, c * C_TILE:(c + 1) * C_TILE]


def _inproj(x2, g, wrm, wt, cs, ts, layer):
    m, d = x2.shape
    tm = ROW_TILE
    row = lambda i: (i, 0)
    fixed = lambda i: (0, 0)
    tiles = lambda i: (i, 0, 0)
    this_layer = lambda i: (layer, 0, 0)
    widths = (A_W, B_W, B_KV_W, C_W, C_W)
    return pl.pallas_call(
        _inproj_kernel,
        grid=(m // tm,),
        in_specs=[
            pl.BlockSpec((tm, d), row),
            pl.BlockSpec((1, d), fixed),
            pl.BlockSpec((None, d, RM_W), this_layer),
            pl.BlockSpec((None, T_W, d), this_layer),
            pl.BlockSpec((1, OFF_GATE), fixed),
            pl.BlockSpec((T_W, 1), fixed),
        ],
        out_specs=[pl.BlockSpec((tm, w), row) for w in widths] + [
            pl.BlockSpec((tm, GATE_W), row),
            pl.BlockSpec((tm // A_TILE, A_W, A_TILE), tiles),
            pl.BlockSpec((tm // B_TILE, B_KV_W, B_TILE), tiles),
            pl.BlockSpec((tm // C_TILE, C_W, C_TILE), tiles),
            pl.BlockSpec((tm // A_TILE, A_W, A_TILE), tiles),
        ],
        out_shape=[jax.ShapeDtypeStruct((m, w), BF16) for w in widths] + [
            jax.ShapeDtypeStruct((m, GATE_W), F32),
            jax.ShapeDtypeStruct((m // A_TILE, A_W, A_TILE), BF16),
            jax.ShapeDtypeStruct((m // B_TILE, B_KV_W, B_TILE), BF16),
            jax.ShapeDtypeStruct((m // C_TILE, C_W, C_TILE), BF16),
            jax.ShapeDtypeStruct((m // A_TILE, A_W, A_TILE), BF16),
        ],
        compiler_params=pltpu.CompilerParams(
            dimension_semantics=("parallel",), vmem_limit_bytes=VMEM_LIMIT_BYTES),
        name="inproj",
    )(x2, g, wrm, wt, cs, ts)


def _attn_a_kernel(lq1_ref, lk1_ref, lq2_ref, lk2_ref, sg_ref, qfeat_ref,
                   qt_ref, k_ref, kfeat_ref, vt_ref, gate_ref, o_ref,
                   qz_sc, m_sc, alpha_sc, acc_sc, s_sc, smax_sc, p_sc, *, lambda_init):
    t = A_TILE
    tk = A_KEYS
    qi = pl.program_id(1)
    n_full = qi // 2
    m_sc[...] = jnp.full(m_sc.shape, MASK_VALUE, F32)
    acc_sc[...] = jnp.zeros(acc_sc.shape, F32)
    ones_rows = jnp.ones((A_NORM_ROWS, tk), BF16)

    sub = lax.broadcasted_iota(jnp.int32, (LANES, t), 0)
    for pr in range(2):
        qp = qt_ref[pr * LANES:(pr + 1) * LANES, :]
        for c in range(4):
            hh, mp = divmod(c, 2)
            lo = hh * HEAD_DIM + mp * A_QK_DIM
            real = jnp.where((sub >= lo) & (sub < lo + A_QK_DIM), qp, jnp.zeros_like(qp))
            h = 2 * pr + hh
            feat = jnp.broadcast_to(qfeat_ref[:, h:h + 1], (LANES, t)).astype(BF16)
            qz_sc[pr, :, c * t:(c + 1) * t] = jnp.concatenate([real, feat], axis=0)

    def key_rows(pr, step, nk):
        ks = pl.ds(pl.multiple_of(step * tk, tk), nk)
        return jnp.concatenate([k_ref[ks, pr * LANES:(pr + 1) * LANES], kfeat_ref[ks, :]], axis=1)

    def scores(pr, c, lhs):
        return jnp.dot(lhs, qz_sc[pr, :, c * t:(c + 1) * t], preferred_element_type=F32)

    def softmax_block(pr, c, nk):
        cols = slice(c * t, (c + 1) * t)
        m_old = m_sc[pr, :, cols]
        m_new = jnp.maximum(m_old, smax_sc[pr, :, cols])
        m_sc[pr, :, cols] = m_new
        return jnp.exp2(m_old - m_new), jnp.exp2((s_sc[pr, :nk, cols] - m_new).astype(BF16))

    def value_block(pr, c, step, nk):
        hh, mp = divmod(c, 2)
        h = 2 * pr + hh
        rows = slice(h * HEAD_DIM, (h + 1) * HEAD_DIM)
        cols = slice(c * t, (c + 1) * t)
        ocols = slice(mp * t, (mp + 1) * t)
        vt = jnp.concatenate([vt_ref[2 * step + blk, rows, :] for blk in range(nk // t)], axis=1)
        vt1 = jnp.concatenate([vt, ones_rows[:, :nk]], axis=0)
        pv = jnp.dot(vt1, p_sc[pr, :nk, cols], preferred_element_type=F32)
        acc_sc[h, :, ocols] = alpha_sc[pr, :, cols] * acc_sc[h, :, ocols] + pv

    def causal_mask(nk):
        kmq = (lax.broadcasted_iota(jnp.int32, (nk, t), 0)
               - lax.broadcasted_iota(jnp.int32, (nk, t), 1))
        return kmq <= qi * t - n_full * tk

    def stage_scores(pr, c, s):
        cols = slice(c * t, (c + 1) * t)
        s_sc[pr, :s.shape[0], cols] = s
        smax_sc[pr, :, cols] = jnp.max(s, axis=0, keepdims=True)

    def trip(i, nk_cur=tk, nk_next=tk, masked_next=False):
        for pr in range(2):
            lhs = key_rows(pr, i + 1, nk_next) if nk_next else None
            for c in range(4):
                cols = slice(c * t, (c + 1) * t)
                if nk_next:
                    sn = scores(pr, c, lhs)
                value_block(pr, c, jnp.maximum(i - 1, 0), tk)
                alpha, p = softmax_block(pr, c, nk_cur)
                if nk_next:
                    if masked_next:
                        sn = jnp.where(causal_mask(nk_next), sn, MASK_VALUE)
                    stage_scores(pr, c, sn)
                alpha_sc[pr, :, cols] = alpha
                p_sc[pr, :nk_cur, cols] = p

    p_sc[...] = jnp.zeros(p_sc.shape, BF16)
    alpha_sc[...] = jnp.ones(alpha_sc.shape, F32)
    keep = jnp.logical_or(causal_mask(tk), n_full > 0)
    for pr in range(2):
        lhs0 = key_rows(pr, 0, tk)
        for c in range(4):
            stage_scores(pr, c, jnp.where(keep, scores(pr, c, lhs0), MASK_VALUE))

    def body(i, carry):
        trip(i)
        return carry

    lax.fori_loop(0, n_full - 1, body, 0)

    short = jnp.logical_and(qi % 2 == 0, n_full > 0)

    def tail(nk):
        @pl.when(n_full > 0)
        def _():
            trip(n_full - 1, nk_next=nk, masked_next=True)

        trip(n_full, nk_cur=nk, nk_next=0)
        for pr in range(2):
            for c in range(4):
                value_block(pr, c, n_full, nk)

    @pl.when(short)
    def _():
        tail(t)

    @pl.when(jnp.logical_not(short))
    def _():
        tail(tk)

    lam = (jnp.exp(jnp.sum(lq1_ref[...] * lk1_ref[...], axis=-1, keepdims=True))
           - jnp.exp(jnp.sum(lq2_ref[...] * lk2_ref[...], axis=-1, keepdims=True))
           + lambda_init)
    outs = []
    for h in range(A_HEADS):
        pr, hh = divmod(h, 2)
        acc = acc_sc[h, :HEAD_DIM, :]
        l = acc_sc[h, HEAD_DIM:HEAD_DIM + 1, :]
        o = acc[:, :t] / l[:, :t] - lam * (acc[:, t:] / l[:, t:])
        ms = jnp.mean(o * o, axis=0, keepdims=True)
        outs.append(o * lax.rsqrt(ms + NORM_EPS))
    o_all = jnp.concatenate(outs, axis=0) * sg_ref[...] * (1.0 - lambda_init)
    o_ref[...] = (o_all.T * _silu(gate_ref[...])).astype(BF16)


def _attn_a(aqt, ak, kfeat, avt, gate, qfeat, lq1, lk1, lq2, lk2, sg4, lambda_init, b, s):
    t = A_TILE
    tk = A_KEYS
    nt = s // t
    small = lambda shape: pl.BlockSpec(shape, lambda bi, qi: (0, 0))
    return pl.pallas_call(
        functools.partial(_attn_a_kernel, lambda_init=lambda_init),
        grid=(b, nt),
        in_specs=[
            small((1, A_QK_DIM)), small((1, A_QK_DIM)), small((1, A_QK_DIM)), small((1, A_QK_DIM)),
            small((A_W, 1)),
            small((LANES, A_HEADS)),
            pl.BlockSpec((None, A_W, t), lambda bi, qi: (bi * nt + qi, 0, 0)),
            pl.BlockSpec((s, A_W), lambda bi, qi: (bi, 0)),
            pl.BlockSpec((s, LANES), lambda bi, qi: (0, 0)),
            pl.BlockSpec((nt, A_W, t), lambda bi, qi: (bi, 0, 0)),
            pl.BlockSpec((t, A_W), lambda bi, qi: (bi * nt + qi, GATE_BLOCK_A)),
        ],
        out_specs=pl.BlockSpec((t, A_W), lambda bi, qi: (bi * nt + qi, 0)),
        out_shape=jax.ShapeDtypeStruct((b * s, A_W), BF16),
        scratch_shapes=[
            pltpu.VMEM((2, 2 * LANES, 4 * t), BF16),
            pltpu.VMEM((2, 1, 4 * t), F32),
            pltpu.VMEM((2, 1, 4 * t), F32),
            pltpu.VMEM((A_HEADS, HEAD_DIM + A_NORM_ROWS, 2 * t), F32),
            pltpu.VMEM((2, tk, 4 * t), F32),
            pltpu.VMEM((2, 1, 4 * t), F32),
            pltpu.VMEM((2, tk, 4 * t), BF16),
        ],
        compiler_params=pltpu.CompilerParams(
            dimension_semantics=("parallel", "arbitrary"),
            vmem_limit_bytes=VMEM_LIMIT_BYTES),
        name="attn_a",
    )(lq1, lk1, lq2, lk2, sg4, qfeat, aqt, ak, kfeat, avt, gate)


def _attn_b_kernel(q_ref, kp_ref, kc_ref, vtp_ref, vtc_ref, gate_ref, bias_ref, sink_ref, o_ref):
    t = B_TILE
    qi = pl.program_id(1)
    lane = lax.broadcasted_iota(jnp.int32, (t, LANES), 1)
    sink = sink_ref[...]
    k_blocks = [kp_ref[...]] + [kc_ref[c * t:(c + 1) * t, :] for c in range(B_BLOCKS)]
    vt_blocks = [vtp_ref[...]] + [vtc_ref[c] for c in range(B_BLOCKS)]
    for c in range(B_BLOCKS):
        rows = slice(c * t, (c + 1) * t)
        k_all = jnp.concatenate(k_blocks[c:c + 2], axis=0)
        vt_all = jnp.concatenate(vt_blocks[c:c + 2], axis=1)
        pieces = [None] * B_HEADS
        for j in range(B_GROUP):
            qp = q_ref[rows, j * LANES:(j + 1) * LANES]
            zero = jnp.zeros_like(qp)
            pieces[j] = jnp.where(lane < HEAD_DIM, qp, zero)
            pieces[B_GROUP + j] = jnp.where(lane >= HEAD_DIM, qp, zero)
        qz = jnp.concatenate(pieces, axis=0)
        bias = bias_ref[jnp.minimum(qi, 1)] if c == 0 else bias_ref[1]
        s = _dot_nt(k_all, qz) + bias
        m = jnp.maximum(jnp.max(s, axis=0, keepdims=True), sink)
        p = jnp.exp2(s - m)
        inv = 1.0 / (jnp.sum(p, axis=0, keepdims=True) + jnp.exp2(sink - m))
        pb = p.astype(BF16)
        outs = []
        for g in range(B_KV_HEADS):
            cols = slice(g * B_GROUP * t, (g + 1) * B_GROUP * t)
            og = jnp.dot(vt_all[g * HEAD_DIM:(g + 1) * HEAD_DIM, :], pb[:, cols],
                         preferred_element_type=F32) * inv[:, cols]
            outs.extend(og[:, j * t:(j + 1) * t] for j in range(B_GROUP))
        o_all = jnp.concatenate(outs, axis=0)
        o_ref[rows, :] = (o_all.T * _silu(gate_ref[rows, :])).astype(BF16)


def _attn_b(bq, bk, bvt, gate, bias, sink_row, b, s):
    t = B_TILE
    tq = B_BLOCKS * t
    nt = s // tq
    cur = lambda bi, qi: (bi * nt + qi, 0)
    prev = lambda bi, qi: (jnp.maximum((bi * nt + qi) * B_BLOCKS - 1, 0), 0)
    cur3 = lambda bi, qi: (bi * nt + qi, 0, 0)
    prev3 = lambda bi, qi: (jnp.maximum((bi * nt + qi) * B_BLOCKS - 1, 0), 0, 0)
    return pl.pallas_call(
        _attn_b_kernel,
        grid=(b, nt),
        in_specs=[
            pl.BlockSpec((tq, B_W), cur),
            pl.BlockSpec((t, B_KV_W), prev),
            pl.BlockSpec((tq, B_KV_W), cur),
            pl.BlockSpec((None, B_KV_W, t), prev3),
            pl.BlockSpec((B_BLOCKS, B_KV_W, t), cur3),
            pl.BlockSpec((tq, B_W), cur),
            pl.BlockSpec((2, 2 * t, B_HEADS * t), lambda bi, qi: (0, 0, 0)),
            pl.BlockSpec((1, B_HEADS * t), lambda bi, qi: (0, 0)),
        ],
        out_specs=pl.BlockSpec((tq, B_W), cur),
        out_shape=jax.ShapeDtypeStruct((b * s, B_W), BF16),
        compiler_params=pltpu.CompilerParams(
            dimension_semantics=("parallel", "arbitrary"),
            vmem_limit_bytes=VMEM_LIMIT_BYTES),
        name="attn_b",
    )(bq, bk, bk, bvt, bvt, gate, bias, sink_row)


def _b_bias(slopes):
    t = B_TILE
    koff = np.arange(2 * t)[:, None] - t
    qoff = np.arange(t)[None, :]
    dist = qoff - koff
    band = (dist >= 0) & (dist < WINDOW)
    out = np.empty((2, 2 * t, B_HEADS * t), np.float32)
    for h in range(B_HEADS):
        term = (-np.float32(slopes[h]) * dist.astype(np.float32) * np.float32(LOG2_E)).astype(np.float32)
        out[1, :, h * t:(h + 1) * t] = np.where(band, term, np.float32(MASK_VALUE))
        out[0, :, h * t:(h + 1) * t] = np.where(band & (koff >= 0), term, np.float32(MASK_VALUE))
    return jnp.asarray(out)


def _attn_c_kernel(q_ref, k_ref, vt_ref, u_ref, gate_ref, o_ref, qz_sc, carry_sc, acc_sc):
    t = C_TILE
    qi = pl.program_id(1)
    q = q_ref[...]
    lane = lax.broadcasted_iota(jnp.int32, (t, C_W), 1)
    for h in range(C_HEADS):
        own = (lane >= h * HEAD_DIM) & (lane < (h + 1) * HEAD_DIM)
        qz_sc[h * t:(h + 1) * t, :] = jnp.where(own, q, jnp.zeros_like(q))
    u2 = u_ref[...]

    def tile(kj, masked, carry_in):
        ks = pl.ds(pl.multiple_of(kj * t, t), t)
        z = _dot_nt(k_ref[ks, :], qz_sc[...])
        lse = jnp.log2(1.0 + jnp.exp2(jnp.minimum(z, -z)))
        log_beta = jnp.minimum(z, 0.0) - lse
        l1m = log_beta - z
        if masked:
            krow = lax.broadcasted_iota(jnp.int32, (t, C_HEADS * t), 0)
            qcol = lax.broadcasted_iota(jnp.int32, (t, C_HEADS * t), 1) & (t - 1)
            past = krow < qcol
            l1m = jnp.where(past, l1m, 0.0)
        hi = l1m.astype(BF16)
        lo = (l1m - hi.astype(F32)).astype(BF16)
        tail = jnp.dot(u2, jnp.concatenate([hi, lo], axis=0),
                       preferred_element_type=F32)
        a = jnp.exp2(log_beta + tail + carry_in)
        if masked:
            a = jnp.where(past, a, 0.0)
        a = a.astype(BF16)
        parts = [jnp.dot(vt_ref[kj, h * HEAD_DIM:(h + 1) * HEAD_DIM, :], a[:, h * t:(h + 1) * t],
                         preferred_element_type=F32) for h in range(C_HEADS)]
        return jnp.concatenate(parts, axis=0), carry_in + jnp.sum(l1m, axis=0, keepdims=True)

    acc0, carry0 = tile(qi, True, jnp.zeros((1, C_HEADS * t), F32))
    acc_sc[...] = acc0
    carry_sc[...] = carry0

    def cond(state):
        i, go = state
        return jnp.logical_and(i < qi, go)

    def body(state):
        i, _ = state
        acc, carry = tile(qi - 1 - i, False, carry_sc[...])
        acc_sc[...] += acc
        carry_sc[...] = carry
        return i + 1, jnp.max(carry) > C_ZERO_WEIGHT_LOG2

    lax.while_loop(cond, body, (jnp.int32(0), jnp.max(carry0) > C_ZERO_WEIGHT_LOG2))
    o_ref[...] = (acc_sc[...].T * _silu(gate_ref[...])).astype(BF16)


def _attn_c(cq, ck, cvt, u2, gate, b, s):
    t = C_TILE
    nt = s // t
    return pl.pallas_call(
        _attn_c_kernel,
        grid=(b, nt),
        in_specs=[
            pl.BlockSpec((t, C_W), lambda bi, qi: (bi * nt + qi, 0)),
            pl.BlockSpec((s, C_W), lambda bi, qi: (bi, 0)),
            pl.BlockSpec((nt, C_W, t), lambda bi, qi: (bi, 0, 0)),
            pl.BlockSpec((t, 2 * t), lambda bi, qi: (0, 0)),
            pl.BlockSpec((t, C_W), lambda bi, qi: (bi * nt + qi, GATE_BLOCK_C)),
        ],
        out_specs=pl.BlockSpec((t, C_W), lambda bi, qi: (bi * nt + qi, 0)),
        out_shape=jax.ShapeDtypeStruct((b * s, C_W), BF16),
        scratch_shapes=[
            pltpu.VMEM((C_HEADS * t, C_W), BF16),
            pltpu.VMEM((1, C_HEADS * t), F32),
            pltpu.VMEM((C_W, t), F32),
        ],
        compiler_params=pltpu.CompilerParams(
            dimension_semantics=("parallel", "arbitrary"),
            vmem_limit_bytes=VMEM_LIMIT_BYTES),
        name="attn_c",
    )(cq, ck, cvt, u2, gate)


def _outproj_kernel(ya_ref, yb_ref, yc_ref, x_ref, w_ref, fg_ref, o_ref, *, final_norm):
    y = jnp.concatenate([ya_ref[...], yb_ref[...], yc_ref[...]], axis=1)
    r = x_ref[...] + jnp.dot(y, w_ref[...], preferred_element_type=F32)
    if final_norm:
        ms = jnp.mean(r * r, axis=-1, keepdims=True)
        r = r * lax.rsqrt(ms + NORM_EPS) * fg_ref[...]
    o_ref[...] = r


def _outproj(ya, yb, yc, x2, w, fg, layer, final_norm):
    m, d = x2.shape
    row = lambda i: (i, 0)
    fixed = lambda i: (0, 0)
    return pl.pallas_call(
        functools.partial(_outproj_kernel, final_norm=final_norm),
        grid=(m // OUT_ROW_TILE,),
        in_specs=[
            pl.BlockSpec((OUT_ROW_TILE, A_W), row),
            pl.BlockSpec((OUT_ROW_TILE, B_W), row),
            pl.BlockSpec((OUT_ROW_TILE, C_W), row),
            pl.BlockSpec((OUT_ROW_TILE, d), row),
            pl.BlockSpec((None, GATE_W, d), lambda i: (layer, 0, 0)),
            pl.BlockSpec((1, d), fixed),
        ],
        out_specs=pl.BlockSpec((OUT_ROW_TILE, d), row),
        out_shape=jax.ShapeDtypeStruct((m, d), F32),
        compiler_params=pltpu.CompilerParams(
            dimension_semantics=("parallel",), vmem_limit_bytes=VMEM_LIMIT_BYTES),
        name="outproj",
    )(ya, yb, yc, x2, w, fg)


def _split_w_in(w):
    splits = (A_W, A_W, A_W, A_W, B_W, B_KV_W, B_KV_W, B_W, C_W, C_W, C_W, C_W)
    pts = np.cumsum(splits)[:-1]
    aq, ak, av, ag, bq, bk, bv, bg, cq, ck, cv, cg = jnp.split(w, pts, axis=-1)
    depth, d = w.shape[:2]
    bq = bq.reshape(depth, d, B_KV_HEADS, B_GROUP, HEAD_DIM).transpose(0, 1, 3, 2, 4)
    bq = bq.reshape(depth, d, B_W)
    wrm = jnp.concatenate([ak, bq, bk, cq, ck, bg, ag, cg], axis=-1)
    wt = jnp.swapaxes(jnp.concatenate([av, bv, cv, aq], axis=-1), 1, 2)
    return wrm.astype(BF16), wt.astype(BF16)


def _q_scales():
    cs = np.ones((1, OFF_GATE), np.float32)
    cs[0, OFF_BQ:OFF_BK] = HEAD_DIM ** -0.5 * LOG2_E
    cs[0, OFF_CQ:OFF_CK] = HEAD_DIM ** -0.5 * LOG2_E
    ts = np.ones((T_W, 1), np.float32)
    ts[TOFF_AQ:, 0] = A_QK_DIM ** -0.5 * LOG2_E
    return jnp.asarray(cs), jnp.asarray(ts)


def kernel(x, norm_g, w_in, lambda_q1, lambda_k1, lambda_q2, lambda_k2, subln_g, sinks, w_out, final_g):
    b, s, d = x.shape
    depth = w_in.shape[0]
    m = b * s
    slopes_a, slopes_b = _alibi_slopes()

    pos = jnp.arange(s, dtype=jnp.int32)
    p_hi = (pos // 256).astype(BF16)
    p_lo = (pos % 256).astype(BF16)
    kfeat = jnp.stack([p_hi] * 3 + [p_lo] * 3, axis=-1)
    kfeat = jnp.pad(kfeat, ((0, 0), (0, LANES - 6)))
    qfeat_np = np.zeros((LANES, A_HEADS), np.float32)
    for h, sl in enumerate(slopes_a):
        sl2 = np.float32(np.float32(sl) * np.float32(LOG2_E))
        qfeat_np[:6, h] = _split3_bf16(np.float32(256.0) * sl2) + _split3_bf16(sl2)
    qfeat = jnp.asarray(qfeat_np)

    tri = (lax.broadcasted_iota(jnp.int32, (C_TILE, C_TILE), 1)
           > lax.broadcasted_iota(jnp.int32, (C_TILE, C_TILE), 0)).astype(BF16)
    tri2 = jnp.concatenate([tri, tri], axis=1)
    b_bias = _b_bias(slopes_b)
    cs, ts = _q_scales()

    x2 = x.reshape(m, d)
    wrm_all, wt_all = _split_w_in(w_in)
    w_out_all = w_out.astype(BF16)
    sg_all = jnp.tile(subln_g, (1, A_HEADS)).reshape(depth, -1, 1)
    sink_rows = jnp.repeat(sinks * LOG2_E, B_TILE, axis=1).reshape(depth, 1, -1)
    for l in range(depth):
        lambda_init = 0.8 - 0.6 * math.exp(-0.3 * l)
        ak, bq, bk, cq, ck, gate, avt, bvt, cvt, aqt = _inproj(
            x2, norm_g[l].reshape(1, d), wrm_all, wt_all, cs, ts, l)
        ya = _attn_a(aqt, ak, kfeat, avt, gate, qfeat,
                     lambda_q1[l].reshape(1, -1), lambda_k1[l].reshape(1, -1),
                     lambda_q2[l].reshape(1, -1), lambda_k2[l].reshape(1, -1),
                     sg_all[l], lambda_init, b, s)
        yb = _attn_b(bq, bk, bvt, gate, b_bias, sink_rows[l], b, s)
        yc = _attn_c(cq, ck, cvt, tri2, gate, b, s)
        x2 = _outproj(ya, yb, yc, x2, w_out_all, final_g.reshape(1, d), l,
                      final_norm=(l == depth - 1))
    return x2.reshape(b, s, d)
```

```python
import functools
import math

import numpy as np
import jax
import jax.numpy as jnp
from jax import lax
from jax.experimental import pallas as pl
from jax.experimental.pallas import tpu as pltpu

F32 = jnp.float32
BF16 = jnp.bfloat16

HEAD_DIM = 64
NORM_EPS = 1e-5
A_HEADS = 4
A_QK_DIM = HEAD_DIM // 2
B_HEADS = 8
B_KV_HEADS = 2
B_GROUP = B_HEADS // B_KV_HEADS
C_HEADS = 4
WINDOW = 128
A_W = A_HEADS * HEAD_DIM
B_W = B_HEADS * HEAD_DIM
B_KV_W = B_KV_HEADS * HEAD_DIM
C_W = C_HEADS * HEAD_DIM
GATE_W = A_W + B_W + C_W
LANES = 128
MASK_VALUE = -1e30
LOG2_E = math.log2(math.e)
C_ZERO_WEIGHT_LOG2 = -105.0 * LOG2_E
VMEM_LIMIT_BYTES = 48 * 1024 * 1024

ROW_TILE = 512
OUT_ROW_TILE = 1024
A_TILE = 256
A_KEYS = 2 * A_TILE
A_NORM_ROWS = 16
C_TILE = 256
C_SUBTILES = 2
B_TILE = WINDOW
B_BLOCKS = 8

OFF_AK = 0
OFF_BQ = OFF_AK + A_W
OFF_BK = OFF_BQ + B_W
OFF_CQ = OFF_BK + B_KV_W
OFF_CK = OFF_CQ + C_W
OFF_GATE = OFF_CK + C_W
RM_W = OFF_GATE + GATE_W
TOFF_AV = 0
TOFF_BV = TOFF_AV + A_W
TOFF_CV = TOFF_BV + B_KV_W
TOFF_AQ = TOFF_CV + C_W
T_W = TOFF_AQ + A_W
GATE_BLOCK_A = B_W // A_W
GATE_BLOCK_C = (B_W + A_W) // C_W


def _alibi_slopes():
    n = A_HEADS + B_HEADS
    s = (2.0 ** (-8.0 * np.arange(1, n + 1) / n)).astype(np.float32)
    return s[B_HEADS:], s[:B_HEADS]


def _split3_bf16(v):
    out = []
    r = np.float32(v)
    for _ in range(3):
        p = np.asarray(r, np.float32).astype(BF16).astype(np.float32)
        out.append(float(p))
        r = np.float32(r - p)
    return out


def _dot_nt(a, b):
    return lax.dot_general(a, b, (((1,), (1,)), ((), ())), preferred_element_type=F32)


def _silu(g):
    return g * jax.nn.sigmoid(g)


def _inproj_kernel(x_ref, g_ref, wrm_ref, wt_ref, cs_ref, ts_ref,
                   ak_ref, bq_ref, bk_ref, cq_ref, ck_ref, gate_ref,
                   avt_ref, bvt_ref, cvt_ref, aqt_ref):
    x = x_ref[...]
    ms = jnp.mean(x * x, axis=-1, keepdims=True)
    h = (x * lax.rsqrt(ms + NORM_EPS) * g_ref[...]).astype(BF16)
    qk = jnp.dot(h, wrm_ref[:, :OFF_GATE], preferred_element_type=F32) * cs_ref[...]
    ak_ref[...] = qk[:, OFF_AK:OFF_BQ].astype(BF16)
    bq_ref[...] = qk[:, OFF_BQ:OFF_BK].astype(BF16)
    bk_ref[...] = qk[:, OFF_BK:OFF_CQ].astype(BF16)
    cq_ref[...] = qk[:, OFF_CQ:OFF_CK].astype(BF16)
    ck_ref[...] = qk[:, OFF_CK:OFF_GATE].astype(BF16)
    gate_ref[...] = jnp.dot(h, wrm_ref[:, OFF_GATE:], preferred_element_type=F32)
    tr = (_dot_nt(wt_ref[...], h) * ts_ref[...]).astype(BF16)
    for c in range(ROW_TILE // A_TILE):
        avt_ref[c] = tr[TOFF_AV:TOFF_BV, c * A_TILE:(c + 1) * A_TILE]
        aqt_ref[c] = tr[TOFF_AQ:, c * A_TILE:(c + 1) * A_TILE]
    for c in range(ROW_TILE // B_TILE):
        bvt_ref[c] = tr[TOFF_BV:TOFF_CV, c * B_TILE:(c + 1) * B_TILE]
    for c in range(ROW_TILE // C_TILE):
        cvt_ref[c] = tr[TOFF_CV:TOFF_AQ, c * C_TILE:(c + 1) * C_TILE]


def _inproj(x2, g, wrm, wt, cs, ts, layer):
    m, d = x2.shape
    tm = ROW_TILE
    row = lambda i: (i, 0)
    fixed = lambda i: (0, 0)
    tiles = lambda i: (i, 0, 0)
    this_layer = lambda i: (layer, 0, 0)
    widths = (A_W, B_W, B_KV_W, C_W, C_W)
    return pl.pallas_call(
        _inproj_kernel,
        grid=(m // tm,),
        in_specs=[
            pl.BlockSpec((tm, d), row),
            pl.BlockSpec((1, d), fixed),
            pl.BlockSpec((None, d, RM_W), this_layer),
            pl.BlockSpec((None, T_W, d), this_layer),
            pl.BlockSpec((1, OFF_GATE), fixed),
            pl.BlockSpec((T_W, 1), fixed),
        ],
        out_specs=[pl.BlockSpec((tm, w), row) for w in widths] + [
            pl.BlockSpec((tm, GATE_W), row),
            pl.BlockSpec((tm // A_TILE, A_W, A_TILE), tiles),
            pl.BlockSpec((tm // B_TILE, B_KV_W, B_TILE), tiles),
            pl.BlockSpec((tm // C_TILE, C_W, C_TILE), tiles),
            pl.BlockSpec((tm // A_TILE, A_W, A_TILE), tiles),
        ],
        out_shape=[jax.ShapeDtypeStruct((m, w), BF16) for w in widths] + [
            jax.ShapeDtypeStruct((m, GATE_W), F32),
            jax.ShapeDtypeStruct((m // A_TILE, A_W, A_TILE), BF16),
            jax.ShapeDtypeStruct((m // B_TILE, B_KV_W, B_TILE), BF16),
            jax.ShapeDtypeStruct((m // C_TILE, C_W, C_TILE), BF16),
            jax.ShapeDtypeStruct((m // A_TILE, A_W, A_TILE), BF16),
        ],
        compiler_params=pltpu.CompilerParams(
            dimension_semantics=("parallel",), vmem_limit_bytes=VMEM_LIMIT_BYTES),
        name="inproj",
    )(x2, g, wrm, wt, cs, ts)


def _attn_a_kernel(lq1_ref, lk1_ref, lq2_ref, lk2_ref, sg_ref, qfeat_ref,
                   qt_ref, k_ref, kfeat_ref, vt_ref, gate_ref, o_ref,
                   qz_sc, m_sc, alpha_sc, acc_sc, s_sc, smax_sc, p_sc, *, lambda_init):
    t = A_TILE
    tk = A_KEYS
    qi = pl.program_id(1)
    n_full = qi // 2
    m_sc[...] = jnp.full(m_sc.shape, MASK_VALUE, F32)
    acc_sc[...] = jnp.zeros(acc_sc.shape, F32)
    ones_rows = jnp.ones((A_NORM_ROWS, tk), BF16)

    sub = lax.broadcasted_iota(jnp.int32, (LANES, t), 0)
    for pr in range(2):
        qp = qt_ref[pr * LANES:(pr + 1) * LANES, :]
        for c in range(4):
            hh, mp = divmod(c, 2)
            lo = hh * HEAD_DIM + mp * A_QK_DIM
            real = jnp.where((sub >= lo) & (sub < lo + A_QK_DIM), qp, jnp.zeros_like(qp))
            h = 2 * pr + hh
            feat = jnp.broadcast_to(qfeat_ref[:, h:h + 1], (LANES, t)).astype(BF16)
            qz_sc[pr, :, c * t:(c + 1) * t] = jnp.concatenate([real, feat], axis=0)

    def key_rows(pr, step, nk):
        ks = pl.ds(pl.multiple_of(step * tk, tk), nk)
        return jnp.concatenate([k_ref[ks, pr * LANES:(pr + 1) * LANES], kfeat_ref[ks, :]], axis=1)

    def scores(pr, c, lhs):
        return jnp.dot(lhs, qz_sc[pr, :, c * t:(c + 1) * t], preferred_element_type=F32)

    def softmax_block(pr, c, nk):
        cols = slice(c * t, (c + 1) * t)
        m_old = m_sc[pr, :, cols]
        m_new = jnp.maximum(m_old, smax_sc[pr, :, cols])
        m_sc[pr, :, cols] = m_new
        return jnp.exp2(m_old - m_new), jnp.exp2((s_sc[pr, :nk, cols] - m_new).astype(BF16))

    def value_block(pr, c, step, nk):
        hh, mp = divmod(c, 2)
        h = 2 * pr + hh
        rows = slice(h * HEAD_DIM, (h + 1) * HEAD_DIM)
        cols = slice(c * t, (c + 1) * t)
        ocols = slice(mp * t, (mp + 1) * t)
        vt = jnp.concatenate([vt_ref[2 * step + blk, rows, :] for blk in range(nk // t)], axis=1)
        vt1 = jnp.concatenate([vt, ones_rows[:, :nk]], axis=0)
        pv = jnp.dot(vt1, p_sc[pr, :nk, cols], preferred_element_type=F32)
        acc_sc[h, :, ocols] = alpha_sc[pr, :, cols] * acc_sc[h, :, ocols] + pv

    def causal_mask(nk):
        kmq = (lax.broadcasted_iota(jnp.int32, (nk, t), 0)
               - lax.broadcasted_iota(jnp.int32, (nk, t), 1))
        return kmq <= qi * t - n_full * tk

    def stage_scores(pr, c, s):
        cols = slice(c * t, (c + 1) * t)
        s_sc[pr, :s.shape[0], cols] = s
        smax_sc[pr, :, cols] = jnp.max(s, axis=0, keepdims=True)

    def trip(i, nk_cur=tk, nk_next=tk, masked_next=False):
        for pr in range(2):
            lhs = key_rows(pr, i + 1, nk_next) if nk_next else None
            for c in range(4):
                cols = slice(c * t, (c + 1) * t)
                if nk_next:
                    sn = scores(pr, c, lhs)
                value_block(pr, c, jnp.maximum(i - 1, 0), tk)
                alpha, p = softmax_block(pr, c, nk_cur)
                if nk_next:
                    if masked_next:
                        sn = jnp.where(causal_mask(nk_next), sn, MASK_VALUE)
                    stage_scores(pr, c, sn)
                alpha_sc[pr, :, cols] = alpha
                p_sc[pr, :nk_cur, cols] = p

    p_sc[...] = jnp.zeros(p_sc.shape, BF16)
    alpha_sc[...] = jnp.ones(alpha_sc.shape, F32)
    keep = jnp.logical_or(causal_mask(tk), n_full > 0)
    for pr in range(2):
        lhs0 = key_rows(pr, 0, tk)
        for c in range(4):
            stage_scores(pr, c, jnp.where(keep, scores(pr, c, lhs0), MASK_VALUE))

    def body(i, carry):
        trip(i)
        return carry

    lax.fori_loop(0, n_full - 1, body, 0)

    short = jnp.logical_and(qi % 2 == 0, n_full > 0)

    def tail(nk):
        @pl.when(n_full > 0)
        def _():
            trip(n_full - 1, nk_next=nk, masked_next=True)

        trip(n_full, nk_cur=nk, nk_next=0)
        for pr in range(2):
            for c in range(4):
                value_block(pr, c, n_full, nk)

    @pl.when(short)
    def _():
        tail(t)

    @pl.when(jnp.logical_not(short))
    def _():
        tail(tk)

    lam = (jnp.exp(jnp.sum(lq1_ref[...] * lk1_ref[...], axis=-1, keepdims=True))
           - jnp.exp(jnp.sum(lq2_ref[...] * lk2_ref[...], axis=-1, keepdims=True))
           + lambda_init)
    outs = []
    for h in range(A_HEADS):
        pr, hh = divmod(h, 2)
        acc = acc_sc[h, :HEAD_DIM, :]
        l = acc_sc[h, HEAD_DIM:HEAD_DIM + 1, :]
        o = acc[:, :t] / l[:, :t] - lam * (acc[:, t:] / l[:, t:])
        ms = jnp.mean(o * o, axis=0, keepdims=True)
        outs.append(o * lax.rsqrt(ms + NORM_EPS))
    o_all = jnp.concatenate(outs, axis=0) * sg_ref[...] * (1.0 - lambda_init)
    o_ref[...] = (o_all.T * _silu(gate_ref[...])).astype(BF16)


def _attn_a(aqt, ak, kfeat, avt, gate, qfeat, lq1, lk1, lq2, lk2, sg4, lambda_init, b, s):
    t = A_TILE
    tk = A_KEYS
    nt = s // t
    small = lambda shape: pl.BlockSpec(shape, lambda bi, qi: (0, 0))
    return pl.pallas_call(
        functools.partial(_attn_a_kernel, lambda_init=lambda_init),
        grid=(b, nt),
        in_specs=[
            small((1, A_QK_DIM)), small((1, A_QK_DIM)), small((1, A_QK_DIM)), small((1, A_QK_DIM)),
            small((A_W, 1)),
            small((LANES, A_HEADS)),
            pl.BlockSpec((None, A_W, t), lambda bi, qi: (bi * nt + qi, 0, 0)),
            pl.BlockSpec((s, A_W), lambda bi, qi: (bi, 0)),
            pl.BlockSpec((s, LANES), lambda bi, qi: (0, 0)),
            pl.BlockSpec((nt, A_W, t), lambda bi, qi: (bi, 0, 0)),
            pl.BlockSpec((t, A_W), lambda bi, qi: (bi * nt + qi, GATE_BLOCK_A)),
        ],
        out_specs=pl.BlockSpec((t, A_W), lambda bi, qi: (bi * nt + qi, 0)),
        out_shape=jax.ShapeDtypeStruct((b * s, A_W), BF16),
        scratch_shapes=[
            pltpu.VMEM((2, 2 * LANES, 4 * t), BF16),
            pltpu.VMEM((2, 1, 4 * t), F32),
            pltpu.VMEM((2, 1, 4 * t), F32),
            pltpu.VMEM((A_HEADS, HEAD_DIM + A_NORM_ROWS, 2 * t), F32),
            pltpu.VMEM((2, tk, 4 * t), F32),
            pltpu.VMEM((2, 1, 4 * t), F32),
            pltpu.VMEM((2, tk, 4 * t), BF16),
        ],
        compiler_params=pltpu.CompilerParams(
            dimension_semantics=("parallel", "arbitrary"),
            vmem_limit_bytes=VMEM_LIMIT_BYTES),
        name="attn_a",
    )(lq1, lk1, lq2, lk2, sg4, qfeat, aqt, ak, kfeat, avt, gate)


def _attn_b_kernel(q_ref, kp_ref, kc_ref, vtp_ref, vtc_ref, gate_ref, bias_ref, sink_ref, o_ref):
    t = B_TILE
    qi = pl.program_id(1)
    lane = lax.broadcasted_iota(jnp.int32, (t, LANES), 1)
    sink = sink_ref[...]
    k_blocks = [kp_ref[...]] + [kc_ref[c * t:(c + 1) * t, :] for c in range(B_BLOCKS)]
    vt_blocks = [vtp_ref[...]] + [vtc_ref[c] for c in range(B_BLOCKS)]
    for c in range(B_BLOCKS):
        rows = slice(c * t, (c + 1) * t)
        k_all = jnp.concatenate(k_blocks[c:c + 2], axis=0)
        vt_all = jnp.concatenate(vt_blocks[c:c + 2], axis=1)
        pieces = [None] * B_HEADS
        for j in range(B_GROUP):
            qp = q_ref[rows, j * LANES:(j + 1) * LANES]
            zero = jnp.zeros_like(qp)
            pieces[j] = jnp.where(lane < HEAD_DIM, qp, zero)
            pieces[B_GROUP + j] = jnp.where(lane >= HEAD_DIM, qp, zero)
        qz = jnp.concatenate(pieces, axis=0)
        bias = bias_ref[jnp.minimum(qi, 1)] if c == 0 else bias_ref[1]
        s = _dot_nt(k_all, qz) + bias
        m = jnp.maximum(jnp.max(s, axis=0, keepdims=True), sink)
        p = jnp.exp2(s - m)
        inv = 1.0 / (jnp.sum(p, axis=0, keepdims=True) + jnp.exp2(sink - m))
        pb = p.astype(BF16)
        outs = []
        for g in range(B_KV_HEADS):
            cols = slice(g * B_GROUP * t, (g + 1) * B_GROUP * t)
            og = jnp.dot(vt_all[g * HEAD_DIM:(g + 1) * HEAD_DIM, :], pb[:, cols],
                         preferred_element_type=F32) * inv[:, cols]
            outs.extend(og[:, j * t:(j + 1) * t] for j in range(B_GROUP))
        o_all = jnp.concatenate(outs, axis=0)
        o_ref[rows, :] = (o_all.T * _silu(gate_ref[rows, :])).astype(BF16)


def _attn_b(bq, bk, bvt, gate, bias, sink_row, b, s):
    t = B_TILE
    tq = B_BLOCKS * t
    nt = s // tq
    cur = lambda bi, qi: (bi * nt + qi, 0)
    prev = lambda bi, qi: (jnp.maximum((bi * nt + qi) * B_BLOCKS - 1, 0), 0)
    cur3 = lambda bi, qi: (bi * nt + qi, 0, 0)
    prev3 = lambda bi, qi: (jnp.maximum((bi * nt + qi) * B_BLOCKS - 1, 0), 0, 0)
    return pl.pallas_call(
        _attn_b_kernel,
        grid=(b, nt),
        in_specs=[
            pl.BlockSpec((tq, B_W), cur),
            pl.BlockSpec((t, B_KV_W), prev),
            pl.BlockSpec((tq, B_KV_W), cur),
            pl.BlockSpec((None, B_KV_W, t), prev3),
            pl.BlockSpec((B_BLOCKS, B_KV_W, t), cur3),
            pl.BlockSpec((tq, B_W), cur),
            pl.BlockSpec((2, 2 * t, B_HEADS * t), lambda bi, qi: (0, 0, 0)),
            pl.BlockSpec((1, B_HEADS * t), lambda bi, qi: (0, 0)),
        ],
        out_specs=pl.BlockSpec((tq, B_W), cur),
        out_shape=jax.ShapeDtypeStruct((b * s, B_W), BF16),
        compiler_params=pltpu.CompilerParams(
            dimension_semantics=("parallel", "arbitrary"),
            vmem_limit_bytes=VMEM_LIMIT_BYTES),
        name="attn_b",
    )(bq, bk, bk, bvt, bvt, gate, bias, sink_row)


def _b_bias(slopes):
    t = B_TILE
    koff = np.arange(2 * t)[:, None] - t
    qoff = np.arange(t)[None, :]
    dist = qoff - koff
    band = (dist >= 0) & (dist < WINDOW)
    out = np.empty((2, 2 * t, B_HEADS * t), np.float32)
    for h in range(B_HEADS):
        term = (-np.float32(slopes[h]) * dist.astype(np.float32) * np.float32(LOG2_E)).astype(np.float32)
        out[1, :, h * t:(h + 1) * t] = np.where(band, term, np.float32(MASK_VALUE))
        out[0, :, h * t:(h + 1) * t] = np.where(band & (koff >= 0), term, np.float32(MASK_VALUE))
    return jnp.asarray(out)


def _attn_c_kernel(q_ref, k_ref, vt_ref, u_ref, gate_ref, o_ref, qz_sc, carry_sc, acc_sc):
    t = C_TILE
    step = pl.program_id(1)
    lane = lax.broadcasted_iota(jnp.int32, (t, C_W), 1)
    for sub in range(C_SUBTILES):
        q = q_ref[sub * t:(sub + 1) * t, :]
        for h in range(C_HEADS):
            own = (lane >= h * HEAD_DIM) & (lane < (h + 1) * HEAD_DIM)
            qz_sc[sub, h * t:(h + 1) * t, :] = jnp.where(own, q, jnp.zeros_like(q))
    u2 = u_ref[...]

    def tile(sub, kj, masked, carry_in):
        ks = pl.ds(pl.multiple_of(kj * t, t), t)
        z = _dot_nt(k_ref[ks, :], qz_sc[sub])
        lse = jnp.log2(1.0 + jnp.exp2(jnp.minimum(z, -z)))
        log_beta = jnp.minimum(z, 0.0) - lse
        l1m = log_beta - z
        if masked:
            krow = lax.broadcasted_iota(jnp.int32, (t, C_HEADS * t), 0)
            qcol = lax.broadcasted_iota(jnp.int32, (t, C_HEADS * t), 1) & (t - 1)
            past = krow < qcol
            l1m = jnp.where(past, l1m, 0.0)
        hi = l1m.astype(BF16)
        lo = (l1m - hi.astype(F32)).astype(BF16)
        tail = jnp.dot(u2, jnp.concatenate([hi, lo], axis=0),
                       preferred_element_type=F32)
        a = jnp.exp2(log_beta + tail + carry_in)
        if masked:
            a = jnp.where(past, a, 0.0)
        a = a.astype(BF16)
        parts = [jnp.dot(vt_ref[kj, h * HEAD_DIM:(h + 1) * HEAD_DIM, :], a[:, h * t:(h + 1) * t],
                         preferred_element_type=F32) for h in range(C_HEADS)]
        return jnp.concatenate(parts, axis=0), carry_in + jnp.sum(l1m, axis=0, keepdims=True)

    tiles = [C_SUBTILES * step + sub for sub in range(C_SUBTILES)]
    alive = []
    for sub in range(C_SUBTILES):
        acc, carry = tile(sub, tiles[sub], True, jnp.zeros((1, C_HEADS * t), F32))
        acc_sc[sub] = acc
        carry_sc[sub] = carry
        alive.append(jnp.logical_and(tiles[sub] > 0, jnp.max(carry) > C_ZERO_WEIGHT_LOG2))

    def cond(state):
        return functools.reduce(jnp.logical_or, state[1:])

    def body(state):
        r = state[0]
        nxt = []
        for sub in range(C_SUBTILES):
            has_keys = tiles[sub] - 1 - r >= 0
            acc, carry = tile(sub, jnp.maximum(tiles[sub] - 1 - r, 0), False, carry_sc[sub])
            acc_sc[sub] += jnp.where(has_keys, acc, 0.0)
            carry_sc[sub] = jnp.where(has_keys, carry, carry_sc[sub])
            more_keys = tiles[sub] - 2 - r >= 0
            nxt.append(jnp.logical_and(more_keys, jnp.max(carry) > C_ZERO_WEIGHT_LOG2))
        return (r + 1, *nxt)

    lax.while_loop(cond, body, (jnp.int32(0), *alive))
    for sub in range(C_SUBTILES):
        rows = slice(sub * t, (sub + 1) * t)
        o_ref[rows, :] = (acc_sc[sub].T * _silu(gate_ref[rows, :])).astype(BF16)


def _attn_c(cq, ck, cvt, u2, gate, b, s):
    t = C_TILE
    tq = C_SUBTILES * t
    nt = s // t
    nq = s // tq
    return pl.pallas_call(
        _attn_c_kernel,
        grid=(b, nq),
        in_specs=[
            pl.BlockSpec((tq, C_W), lambda bi, qi: (bi * nq + qi, 0)),
            pl.BlockSpec((s, C_W), lambda bi, qi: (bi, 0)),
            pl.BlockSpec((nt, C_W, t), lambda bi, qi: (bi, 0, 0)),
            pl.BlockSpec((t, 2 * t), lambda bi, qi: (0, 0)),
            pl.BlockSpec((tq, C_W), lambda bi, qi: (bi * nq + qi, GATE_BLOCK_C)),
        ],
        out_specs=pl.BlockSpec((tq, C_W), lambda bi, qi: (bi * nq + qi, 0)),
        out_shape=jax.ShapeDtypeStruct((b * s, C_W), BF16),
        scratch_shapes=[
            pltpu.VMEM((C_SUBTILES, C_HEADS * t, C_W), BF16),
            pltpu.VMEM((C_SUBTILES, 1, C_HEADS * t), F32),
            pltpu.VMEM((C_SUBTILES, C_W, t), F32),
        ],
        compiler_params=pltpu.CompilerParams(
            dimension_semantics=("parallel", "arbitrary"),
            vmem_limit_bytes=VMEM_LIMIT_BYTES),
        name="attn_c",
    )(cq, ck, cvt, u2, gate)


def _outproj_kernel(ya_ref, yb_ref, yc_ref, x_ref, w_ref, fg_ref, o_ref, *, final_norm):
    y = jnp.concatenate([ya_ref[...], yb_ref[...], yc_ref[...]], axis=1)
    r = x_ref[...] + jnp.dot(y, w_ref[...], preferred_element_type=F32)
    if final_norm:
        ms = jnp.mean(r * r, axis=-1, keepdims=True)
        r = r * lax.rsqrt(ms + NORM_EPS) * fg_ref[...]
    o_ref[...] = r


def _outproj(ya, yb, yc, x2, w, fg, layer, final_norm):
    m, d = x2.shape
    row = lambda i: (i, 0)
    fixed = lambda i: (0, 0)
    return pl.pallas_call(
        functools.partial(_outproj_kernel, final_norm=final_norm),
        grid=(m // OUT_ROW_TILE,),
        in_specs=[
            pl.BlockSpec((OUT_ROW_TILE, A_W), row),
            pl.BlockSpec((OUT_ROW_TILE, B_W), row),
            pl.BlockSpec((OUT_ROW_TILE, C_W), row),
            pl.BlockSpec((OUT_ROW_TILE, d), row),
            pl.BlockSpec((None, GATE_W, d), lambda i: (layer, 0, 0)),
            pl.BlockSpec((1, d), fixed),
        ],
        out_specs=pl.BlockSpec((OUT_ROW_TILE, d), row),
        out_shape=jax.ShapeDtypeStruct((m, d), F32),
        compiler_params=pltpu.CompilerParams(
            dimension_semantics=("parallel",), vmem_limit_bytes=VMEM_LIMIT_BYTES),
        name="outproj",
    )(ya, yb, yc, x2, w, fg)


def _split_w_in(w):
    splits = (A_W, A_W, A_W, A_W, B_W, B_KV_W, B_KV_W, B_W, C_W, C_W, C_W, C_W)
    pts = np.cumsum(splits)[:-1]
    aq, ak, av, ag, bq, bk, bv, bg, cq, ck, cv, cg = jnp.split(w, pts, axis=-1)
    depth, d = w.shape[:2]
    bq = bq.reshape(depth, d, B_KV_HEADS, B_GROUP, HEAD_DIM).transpose(0, 1, 3, 2, 4)
    bq = bq.reshape(depth, d, B_W)
    wrm = jnp.concatenate([ak, bq, bk, cq, ck, bg, ag, cg], axis=-1)
    wt = jnp.swapaxes(jnp.concatenate([av, bv, cv, aq], axis=-1), 1, 2)
    return wrm.astype(BF16), wt.astype(BF16)


def _q_scales():
    cs = np.ones((1, OFF_GATE), np.float32)
    cs[0, OFF_BQ:OFF_BK] = HEAD_DIM ** -0.5 * LOG2_E
    cs[0, OFF_CQ:OFF_CK] = HEAD_DIM ** -0.5 * LOG2_E
    ts = np.ones((T_W, 1), np.float32)
    ts[TOFF_AQ:, 0] = A_QK_DIM ** -0.5 * LOG2_E
    return jnp.asarray(cs), jnp.asarray(ts)


def kernel(x, norm_g, w_in, lambda_q1, lambda_k1, lambda_q2, lambda_k2, subln_g, sinks, w_out, final_g):
    b, s, d = x.shape
    depth = w_in.shape[0]
    m = b * s
    slopes_a, slopes_b = _alibi_slopes()

    pos = jnp.arange(s, dtype=jnp.int32)
    p_hi = (pos // 256).astype(BF16)
    p_lo = (pos % 256).astype(BF16)
    kfeat = jnp.stack([p_hi] * 3 + [p_lo] * 3, axis=-1)
    kfeat = jnp.pad(kfeat, ((0, 0), (0, LANES - 6)))
    qfeat_np = np.zeros((LANES, A_HEADS), np.float32)
    for h, sl in enumerate(slopes_a):
        sl2 = np.float32(np.float32(sl) * np.float32(LOG2_E))
        qfeat_np[:6, h] = _split3_bf16(np.float32(256.0) * sl2) + _split3_bf16(sl2)
    qfeat = jnp.asarray(qfeat_np)

    tri = (lax.broadcasted_iota(jnp.int32, (C_TILE, C_TILE), 1)
           > lax.broadcasted_iota(jnp.int32, (C_TILE, C_TILE), 0)).astype(BF16)
    tri2 = jnp.concatenate([tri, tri], axis=1)
    b_bias = _b_bias(slopes_b)
    cs, ts = _q_scales()

    x2 = x.reshape(m, d)
    wrm_all, wt_all = _split_w_in(w_in)
    w_out_all = w_out.astype(BF16)
    sg_all = jnp.tile(subln_g, (1, A_HEADS)).reshape(depth, -1, 1)
    sink_rows = jnp.repeat(sinks * LOG2_E, B_TILE, axis=1).reshape(depth, 1, -1)
    for l in range(depth):
        lambda_init = 0.8 - 0.6 * math.exp(-0.3 * l)
        ak, bq, bk, cq, ck, gate, avt, bvt, cvt, aqt = _inproj(
            x2, norm_g[l].reshape(1, d), wrm_all, wt_all, cs, ts, l)
        ya = _attn_a(aqt, ak, kfeat, avt, gate, qfeat,
                     lambda_q1[l].reshape(1, -1), lambda_k1[l].reshape(1, -1),
                     lambda_q2[l].reshape(1, -1), lambda_k2[l].reshape(1, -1),
                     sg_all[l], lambda_init, b, s)
        yb = _attn_b(bq, bk, bvt, gate, b_bias, sink_rows[l], b, s)
        yc = _attn_c(cq, ck, cvt, tri2, gate, b, s)
        x2 = _outproj(ya, yb, yc, x2, w_out_all, final_g.reshape(1, d), l,
                      final_norm=(l == depth - 1))
    return x2.reshape(b, s, d)
```

```python
import functools
import math

import numpy as np
import jax
import jax.numpy as jnp
from jax import lax
from jax.experimental import pallas as pl
from jax.experimental.pallas import tpu as pltpu

F32 = jnp.float32
BF16 = jnp.bfloat16

HEAD_DIM = 64
NORM_EPS = 1e-5
A_HEADS = 4
A_QK_DIM = HEAD_DIM // 2
B_HEADS = 8
B_KV_HEADS = 2
B_GROUP = B_HEADS // B_KV_HEADS
C_HEADS = 4
WINDOW = 128
A_W = A_HEADS * HEAD_DIM
B_W = B_HEADS * HEAD_DIM
B_KV_W = B_KV_HEADS * HEAD_DIM
C_W = C_HEADS * HEAD_DIM
GATE_W = A_W + B_W + C_W
LANES = 128
MASK_VALUE = -1e30
LOG2_E = math.log2(math.e)
C_ZERO_WEIGHT_LOG2 = -105.0 * LOG2_E
VMEM_LIMIT_BYTES = 48 * 1024 * 1024

ROW_TILE = 512
OUT_ROW_TILE = 1024
A_TILE = 256
A_KEYS = 2 * A_TILE
A_NORM_ROWS = 16
C_TILE = 256
C_SUBTILES = 2
B_TILE = WINDOW
B_BLOCKS = 8

OFF_AK = 0
OFF_BQ = OFF_AK + A_W
OFF_BK = OFF_BQ + B_W
OFF_CQ = OFF_BK + B_KV_W
OFF_CK = OFF_CQ + C_W
OFF_GATE = OFF_CK + C_W
RM_W = OFF_GATE + GATE_W
TOFF_AV = 0
TOFF_BV = TOFF_AV + A_W
TOFF_CV = TOFF_BV + B_KV_W
TOFF_AQ = TOFF_CV + C_W
T_W = TOFF_AQ + A_W
GATE_BLOCK_A = B_W // A_W
GATE_BLOCK_C = (B_W + A_W) // C_W


def _alibi_slopes():
    n = A_HEADS + B_HEADS
    s = (2.0 ** (-8.0 * np.arange(1, n + 1) / n)).astype(np.float32)
    return s[B_HEADS:], s[:B_HEADS]


def _split3_bf16(v):
    out = []
    r = np.float32(v)
    for _ in range(3):
        p = np.asarray(r, np.float32).astype(BF16).astype(np.float32)
        out.append(float(p))
        r = np.float32(r - p)
    return out


def _dot_nt(a, b):
    return lax.dot_general(a, b, (((1,), (1,)), ((), ())), preferred_element_type=F32)


def _silu(g):
    return g * jax.nn.sigmoid(g)


def _inproj_kernel(x_ref, g_ref, wrm_ref, wt_ref, cs_ref, ts_ref,
                   ak_ref, bq_ref, bk_ref, cq_ref, ck_ref, gate_ref,
                   avt_ref, bvt_ref, cvt_ref, aqt_ref):
    x = x_ref[...]
    ms = jnp.mean(x * x, axis=-1, keepdims=True)
    h = (x * lax.rsqrt(ms + NORM_EPS) * g_ref[...]).astype(BF16)
    qk = jnp.dot(h, wrm_ref[:, :OFF_GATE], preferred_element_type=F32) * cs_ref[...]
    ak_ref[...] = qk[:, OFF_AK:OFF_BQ].astype(BF16)
    bq_ref[...] = qk[:, OFF_BQ:OFF_BK].astype(BF16)
    bk_ref[...] = qk[:, OFF_BK:OFF_CQ].astype(BF16)
    cq_ref[...] = qk[:, OFF_CQ:OFF_CK].astype(BF16)
    ck_ref[...] = qk[:, OFF_CK:OFF_GATE].astype(BF16)
    gate_ref[...] = jnp.dot(h, wrm_ref[:, OFF_GATE:], preferred_element_type=F32)
    tr = (_dot_nt(wt_ref[...], h) * ts_ref[...]).astype(BF16)
    for c in range(ROW_TILE // A_TILE):
        avt_ref[c] = tr[TOFF_AV:TOFF_BV, c * A_TILE:(c + 1) * A_TILE]
        aqt_ref[c] = tr[TOFF_AQ:, c * A_TILE:(c + 1) * A_TILE]
    for c in range(ROW_TILE // B_TILE):
        bvt_ref[c] = tr[TOFF_BV:TOFF_CV, c * B_TILE:(c + 1) * B_TILE]
    for c in range(ROW_TILE // C_TILE):
        cvt_ref[c] = tr[TOFF_CV:TOFF_AQ, c * C_TILE:(c + 1) * C_TILE]


def _inproj(x2, g, wrm, wt, cs, ts, layer):
    m, d = x2.shape
    tm = ROW_TILE
    row = lambda i: (i, 0)
    fixed = lambda i: (0, 0)
    tiles = lambda i: (i, 0, 0)
    this_layer = lambda i: (layer, 0, 0)
    widths = (A_W, B_W, B_KV_W, C_W, C_W)
    return pl.pallas_call(
        _inproj_kernel,
        grid=(m // tm,),
        in_specs=[
            pl.BlockSpec((tm, d), row),
            pl.BlockSpec((1, d), fixed),
            pl.BlockSpec((None, d, RM_W), this_layer),
            pl.BlockSpec((None, T_W, d), this_layer),
            pl.BlockSpec((1, OFF_GATE), fixed),
            pl.BlockSpec((T_W, 1), fixed),
        ],
        out_specs=[pl.BlockSpec((tm, w), row) for w in widths] + [
            pl.BlockSpec((tm, GATE_W), row),
            pl.BlockSpec((tm // A_TILE, A_W, A_TILE), tiles),
            pl.BlockSpec((tm // B_TILE, B_KV_W, B_TILE), tiles),
            pl.BlockSpec((tm // C_TILE, C_W, C_TILE), tiles),
            pl.BlockSpec((tm // A_TILE, A_W, A_TILE), tiles),
        ],
        out_shape=[jax.ShapeDtypeStruct((m, w), BF16) for w in widths] + [
            jax.ShapeDtypeStruct((m, GATE_W), F32),
            jax.ShapeDtypeStruct((m // A_TILE, A_W, A_TILE), BF16),
            jax.ShapeDtypeStruct((m // B_TILE, B_KV_W, B_TILE), BF16),
            jax.ShapeDtypeStruct((m // C_TILE, C_W, C_TILE), BF16),
            jax.ShapeDtypeStruct((m // A_TILE, A_W, A_TILE), BF16),
        ],
        compiler_params=pltpu.CompilerParams(
            dimension_semantics=("parallel",), vmem_limit_bytes=VMEM_LIMIT_BYTES),
        name="inproj",
    )(x2, g, wrm, wt, cs, ts)


def _attn_a_kernel(lq1_ref, lk1_ref, lq2_ref, lk2_ref, sg_ref, qfeat_ref,
                   qt_ref, k_ref, kfeat_ref, vt_ref, gate_ref, o_ref,
                   qz_sc, m_sc, acc_sc, s_sc, smax_sc, *, lambda_init):
    t = A_TILE
    tk = A_KEYS
    qi = pl.program_id(1)
    n_full = qi // 2
    m_sc[...] = jnp.full(m_sc.shape, MASK_VALUE, F32)
    acc_sc[...] = jnp.zeros(acc_sc.shape, F32)
    ones_rows = jnp.ones((A_NORM_ROWS, tk), BF16)

    sub = lax.broadcasted_iota(jnp.int32, (LANES, t), 0)
    for pr in range(2):
        qp = qt_ref[pr * LANES:(pr + 1) * LANES, :]
        for c in range(4):
            hh, mp = divmod(c, 2)
            lo = hh * HEAD_DIM + mp * A_QK_DIM
            real = jnp.where((sub >= lo) & (sub < lo + A_QK_DIM), qp, jnp.zeros_like(qp))
            h = 2 * pr + hh
            feat = jnp.broadcast_to(qfeat_ref[:, h:h + 1], (LANES, t)).astype(BF16)
            qz_sc[pr, :, c * t:(c + 1) * t] = jnp.concatenate([real, feat], axis=0)

    def key_rows(pr, step, nk):
        ks = pl.ds(pl.multiple_of(step * tk, tk), nk)
        return jnp.concatenate([k_ref[ks, pr * LANES:(pr + 1) * LANES], kfeat_ref[ks, :]], axis=1)

    def scores(pr, c, lhs):
        return jnp.dot(lhs, qz_sc[pr, :, c * t:(c + 1) * t], preferred_element_type=F32)

    def softmax_block(pr, c, nk):
        cols = slice(c * t, (c + 1) * t)
        m_old = m_sc[pr, :, cols]
        m_new = jnp.maximum(m_old, smax_sc[pr, :, cols])
        m_sc[pr, :, cols] = m_new
        return jnp.exp2(m_old - m_new), jnp.exp2((s_sc[pr, :nk, cols] - m_new).astype(BF16))

    def value_block(pr, c, step, nk, alpha, p):
        hh, mp = divmod(c, 2)
        h = 2 * pr + hh
        rows = slice(h * HEAD_DIM, (h + 1) * HEAD_DIM)
        ocols = slice(mp * t, (mp + 1) * t)
        vt = jnp.concatenate([vt_ref[2 * step + blk, rows, :] for blk in range(nk // t)], axis=1)
        vt1 = jnp.concatenate([vt, ones_rows[:, :nk]], axis=0)
        pv = jnp.dot(vt1, p, preferred_element_type=F32)
        acc_sc[h, :, ocols] = alpha * acc_sc[h, :, ocols] + pv

    def causal_mask(nk):
        kmq = (lax.broadcasted_iota(jnp.int32, (nk, t), 0)
               - lax.broadcasted_iota(jnp.int32, (nk, t), 1))
        return kmq <= qi * t - n_full * tk

    def stage_scores(pr, c, s):
        cols = slice(c * t, (c + 1) * t)
        s_sc[pr, :s.shape[0], cols] = s
        smax_sc[pr, :, cols] = jnp.max(s, axis=0, keepdims=True)

    def trip(i, nk_cur=tk, nk_next=tk, masked_next=False):
        for pr in range(2):
            lhs = key_rows(pr, i + 1, nk_next) if nk_next else None
            for c in range(4):
                if nk_next:
                    sn = scores(pr, c, lhs)
                alpha, p = softmax_block(pr, c, nk_cur)
                value_block(pr, c, i, nk_cur, alpha, p)
                if nk_next:
                    if masked_next:
                        sn = jnp.where(causal_mask(nk_next), sn, MASK_VALUE)
                    stage_scores(pr, c, sn)

    keep =jnp.logical_or(causal_mask(tk), n_full > 0)
    for pr in range(2):
        lhs0 = key_rows(pr, 0, tk)
        for c in range(4):
            stage_scores(pr, c, jnp.where(keep, scores(pr, c, lhs0), MASK_VALUE))

    def body(i, carry):
        trip(i)
        return carry

    lax.fori_loop(0, n_full - 1, body, 0)

    short = jnp.logical_and(qi % 2 == 0, n_full > 0)

    def tail(nk):
        @pl.when(n_full > 0)
        def _():
            trip(n_full - 1, nk_next=nk, masked_next=True)

        trip(n_full, nk_cur=nk, nk_next=0)

    @pl.when(short)
    def _():
        tail(t)

    @pl.when(jnp.logical_not(short))
    def _():
        tail(tk)

    lam = (jnp.exp(jnp.sum(lq1_ref[...] * lk1_ref[...], axis=-1, keepdims=True))
           - jnp.exp(jnp.sum(lq2_ref[...] * lk2_ref[...], axis=-1, keepdims=True))
           + lambda_init)
    outs = []
    for h in range(A_HEADS):
        pr, hh = divmod(h, 2)
        acc = acc_sc[h, :HEAD_DIM, :]
        l = acc_sc[h, HEAD_DIM:HEAD_DIM + 1, :]
        o = acc[:, :t] / l[:, :t] - lam * (acc[:, t:] / l[:, t:])
        ms = jnp.mean(o * o, axis=0, keepdims=True)
        outs.append(o * lax.rsqrt(ms + NORM_EPS))
    o_all = jnp.concatenate(outs, axis=0) * sg_ref[...] * (1.0 - lambda_init)
    o_ref[...] = (o_all.T * _silu(gate_ref[...])).astype(BF16)


def _attn_a(aqt, ak, kfeat, avt, gate, qfeat, lq1, lk1, lq2, lk2, sg4, lambda_init, b, s):
    t = A_TILE
    tk = A_KEYS
    nt = s // t
    small = lambda shape: pl.BlockSpec(shape, lambda bi, qi: (0, 0))
    return pl.pallas_call(
        functools.partial(_attn_a_kernel, lambda_init=lambda_init),
        grid=(b, nt),
        in_specs=[
            small((1, A_QK_DIM)), small((1, A_QK_DIM)), small((1, A_QK_DIM)), small((1, A_QK_DIM)),
            small((A_W, 1)),
            small((LANES, A_HEADS)),
            pl.BlockSpec((None, A_W, t), lambda bi, qi: (bi * nt + qi, 0, 0)),
            pl.BlockSpec((s, A_W), lambda bi, qi: (bi, 0)),
            pl.BlockSpec((s, LANES), lambda bi, qi: (0, 0)),
            pl.BlockSpec((nt, A_W, t), lambda bi, qi: (bi, 0, 0)),
            pl.BlockSpec((t, A_W), lambda bi, qi: (bi * nt + qi, GATE_BLOCK_A)),
        ],
        out_specs=pl.BlockSpec((t, A_W), lambda bi, qi: (bi * nt + qi, 0)),
        out_shape=jax.ShapeDtypeStruct((b * s, A_W), BF16),
        scratch_shapes=[
            pltpu.VMEM((2, 2 * LANES, 4 * t), BF16),
            pltpu.VMEM((2, 1, 4 * t), F32),
            pltpu.VMEM((A_HEADS, HEAD_DIM + A_NORM_ROWS, 2 * t), F32),
            pltpu.VMEM((2, tk, 4 * t), F32),
            pltpu.VMEM((2, 1, 4 * t), F32),
        ],
        compiler_params=pltpu.CompilerParams(
            dimension_semantics=("parallel", "arbitrary"),
            vmem_limit_bytes=VMEM_LIMIT_BYTES),
        name="attn_a",
    )(lq1, lk1, lq2, lk2, sg4, qfeat, aqt, ak, kfeat, avt, gate)


def _attn_b_kernel(q_ref, kp_ref, kc_ref, vtp_ref, vtc_ref, gate_ref, bias_ref, sink_ref, o_ref):
    t = B_TILE
    qi = pl.program_id(1)
    lane = lax.broadcasted_iota(jnp.int32, (t, LANES), 1)
    sink = sink_ref[...]
    k_blocks = [kp_ref[...]] + [kc_ref[c * t:(c + 1) * t, :] for c in range(B_BLOCKS)]
    vt_blocks = [vtp_ref[...]] + [vtc_ref[c] for c in range(B_BLOCKS)]
    for c in range(B_BLOCKS):
        rows = slice(c * t, (c + 1) * t)
        k_all = jnp.concatenate(k_blocks[c:c + 2], axis=0)
        vt_all = jnp.concatenate(vt_blocks[c:c + 2], axis=1)
        pieces = [None] * B_HEADS
        for j in range(B_GROUP):
            qp = q_ref[rows, j * LANES:(j + 1) * LANES]
            zero = jnp.zeros_like(qp)
            pieces[j] = jnp.where(lane < HEAD_DIM, qp, zero)
            pieces[B_GROUP + j] = jnp.where(lane >= HEAD_DIM, qp, zero)
        qz = jnp.concatenate(pieces, axis=0)
        bias = bias_ref[jnp.minimum(qi, 1)] if c == 0 else bias_ref[1]
        s = _dot_nt(k_all, qz) + bias
        m = jnp.maximum(jnp.max(s, axis=0, keepdims=True), sink)
        p = jnp.exp2(s - m)
        inv = 1.0 / (jnp.sum(p, axis=0, keepdims=True) + jnp.exp2(sink - m))
        pb = p.astype(BF16)
        outs = []
        for g in range(B_KV_HEADS):
            cols = slice(g * B_GROUP * t, (g + 1) * B_GROUP * t)
            og = jnp.dot(vt_all[g * HEAD_DIM:(g + 1) * HEAD_DIM, :], pb[:, cols],
                         preferred_element_type=F32) * inv[:, cols]
            outs.extend(og[:, j * t:(j + 1) * t] for j in range(B_GROUP))
        o_all = jnp.concatenate(outs, axis=0)
        o_ref[rows, :] = (o_all.T * _silu(gate_ref[rows, :])).astype(BF16)


def _attn_b(bq, bk, bvt, gate, bias, sink_row, b, s):
    t = B_TILE
    tq = B_BLOCKS * t
    nt = s // tq
    cur = lambda bi, qi: (bi * nt + qi, 0)
    prev = lambda bi, qi: (jnp.maximum((bi * nt + qi) * B_BLOCKS - 1, 0), 0)
    cur3 = lambda bi, qi: (bi * nt + qi, 0, 0)
    prev3 = lambda bi, qi: (jnp.maximum((bi * nt + qi) * B_BLOCKS - 1, 0), 0, 0)
    return pl.pallas_call(
        _attn_b_kernel,
        grid=(b, nt),
        in_specs=[
            pl.BlockSpec((tq, B_W), cur),
            pl.BlockSpec((t, B_KV_W), prev),
            pl.BlockSpec((tq, B_KV_W), cur),
            pl.BlockSpec((None, B_KV_W, t), prev3),
            pl.BlockSpec((B_BLOCKS, B_KV_W, t), cur3),
            pl.BlockSpec((tq, B_W), cur),
            pl.BlockSpec((2, 2 * t, B_HEADS * t), lambda bi, qi: (0, 0, 0)),
            pl.BlockSpec((1, B_HEADS * t), lambda bi, qi: (0, 0)),
        ],
        out_specs=pl.BlockSpec((tq, B_W), cur),
        out_shape=jax.ShapeDtypeStruct((b * s, B_W), BF16),
        compiler_params=pltpu.CompilerParams(
            dimension_semantics=("parallel", "arbitrary"),
            vmem_limit_bytes=VMEM_LIMIT_BYTES),
        name="attn_b",
    )(bq, bk, bk, bvt, bvt, gate, bias, sink_row)


def _b_bias(slopes):
    t = B_TILE
    koff = np.arange(2 * t)[:, None] - t
    qoff = np.arange(t)[None, :]
    dist = qoff - koff
    band = (dist >= 0) & (dist < WINDOW)
    out = np.empty((2, 2 * t, B_HEADS * t), np.float32)
    for h in range(B_HEADS):
        term = (-np.float32(slopes[h]) * dist.astype(np.float32) * np.float32(LOG2_E)).astype(np.float32)
        out[1, :, h * t:(h + 1) * t] = np.where(band, term, np.float32(MASK_VALUE))
        out[0, :, h * t:(h + 1) * t] = np.where(band & (koff >= 0), term, np.float32(MASK_VALUE))
    return jnp.asarray(out)


def _attn_c_kernel(q_ref, k_ref, vt_ref, u_ref, gate_ref, o_ref, qz_sc, carry_sc, acc_sc):
    t = C_TILE
    step = pl.program_id(1)
    lane = lax.broadcasted_iota(jnp.int32, (t, C_W), 1)
    for sub in range(C_SUBTILES):
        q = q_ref[sub * t:(sub + 1) * t, :]
        for h in range(C_HEADS):
            own = (lane >= h * HEAD_DIM) & (lane < (h + 1) * HEAD_DIM)
            qz_sc[sub, h * t:(h + 1) * t, :] = jnp.where(own, q, jnp.zeros_like(q))
    u2 = u_ref[...]

    def tile(sub, kj, masked, carry_in):
        ks = pl.ds(pl.multiple_of(kj * t, t), t)
        z = _dot_nt(k_ref[ks, :], qz_sc[sub])
        lse = jnp.log2(1.0 + jnp.exp2(jnp.minimum(z, -z)))
        log_beta = jnp.minimum(z, 0.0) - lse
        l1m = log_beta - z
        if masked:
            krow = lax.broadcasted_iota(jnp.int32, (t, C_HEADS * t), 0)
            qcol = lax.broadcasted_iota(jnp.int32, (t, C_HEADS * t), 1) & (t - 1)
            past = krow < qcol
            l1m = jnp.where(past, l1m, 0.0)
        hi = l1m.astype(BF16)
        lo = (l1m - hi.astype(F32)).astype(BF16)
        tail = jnp.dot(u2, jnp.concatenate([hi, lo], axis=0),
                       preferred_element_type=F32)
        a = jnp.exp2(log_beta + tail + carry_in)
        if masked:
            a = jnp.where(past, a, 0.0)
        a = a.astype(BF16)
        parts = [jnp.dot(vt_ref[kj, h * HEAD_DIM:(h + 1) * HEAD_DIM, :], a[:, h * t:(h + 1) * t],
                         preferred_element_type=F32) for h in range(C_HEADS)]
        return jnp.concatenate(parts, axis=0), carry_in + jnp.sum(l1m, axis=0, keepdims=True)

    tiles = [C_SUBTILES * step + sub for sub in range(C_SUBTILES)]
    alive = []
    for sub in range(C_SUBTILES):
        acc, carry = tile(sub, tiles[sub], True, jnp.zeros((1, C_HEADS * t), F32))
        acc_sc[sub] = acc
        carry_sc[sub] = carry
        alive.append(jnp.logical_and(tiles[sub] > 0, jnp.max(carry) > C_ZERO_WEIGHT_LOG2))

    def cond(state):
        return functools.reduce(jnp.logical_or, state[1:])

    def body(state):
        r = state[0]
        nxt = []
        for sub in range(C_SUBTILES):
            has_keys = tiles[sub] - 1 - r >= 0
            acc, carry = tile(sub, jnp.maximum(tiles[sub] - 1 - r, 0), False, carry_sc[sub])
            acc_sc[sub] += jnp.where(has_keys, acc, 0.0)
            carry_sc[sub] = jnp.where(has_keys, carry, carry_sc[sub])
            more_keys = tiles[sub] - 2 - r >= 0
            nxt.append(jnp.logical_and(more_keys, jnp.max(carry) > C_ZERO_WEIGHT_LOG2))
        return (r + 1, *nxt)

    lax.while_loop(cond, body, (jnp.int32(0), *alive))
    for sub in range(C_SUBTILES):
        rows = slice(sub * t, (sub + 1) * t)
        o_ref[rows, :] = (acc_sc[sub].T * _silu(gate_ref[rows, :])).astype(BF16)


def _attn_c(cq, ck, cvt, u2, gate, b, s):
    t = C_TILE
    tq = C_SUBTILES * t
    nt = s // t
    nq = s // tq
    return pl.pallas_call(
        _attn_c_kernel,
        grid=(b, nq),
        in_specs=[
            pl.BlockSpec((tq, C_W), lambda bi, qi: (bi * nq + qi, 0)),
            pl.BlockSpec((s, C_W), lambda bi, qi: (bi, 0)),
            pl.BlockSpec((nt, C_W, t), lambda bi, qi: (bi, 0, 0)),
            pl.BlockSpec((t, 2 * t), lambda bi, qi: (0, 0)),
            pl.BlockSpec((tq, C_W), lambda bi, qi: (bi * nq + qi, GATE_BLOCK_C)),
        ],
        out_specs=pl.BlockSpec((tq, C_W), lambda bi, qi: (bi * nq + qi, 0)),
        out_shape=jax.ShapeDtypeStruct((b * s, C_W), BF16),
        scratch_shapes=[
            pltpu.VMEM((C_SUBTILES, C_HEADS * t, C_W), BF16),
            pltpu.VMEM((C_SUBTILES, 1, C_HEADS * t), F32),
            pltpu.VMEM((C_SUBTILES, C_W, t), F32),
        ],
        compiler_params=pltpu.CompilerParams(
            dimension_semantics=("parallel", "arbitrary"),
            vmem_limit_bytes=VMEM_LIMIT_BYTES),
        name="attn_c",
    )(cq, ck, cvt, u2, gate)


def _outproj_kernel(ya_ref, yb_ref, yc_ref, x_ref, w_ref, fg_ref, o_ref, *, final_norm):
    y = jnp.concatenate([ya_ref[...], yb_ref[...], yc_ref[...]], axis=1)
    r = x_ref[...] + jnp.dot(y, w_ref[...], preferred_element_type=F32)
    if final_norm:
        ms = jnp.mean(r * r, axis=-1, keepdims=True)
        r = r * lax.rsqrt(ms + NORM_EPS) * fg_ref[...]
    o_ref[...] = r


def _outproj(ya, yb, yc, x2, w, fg, layer, final_norm):
    m, d = x2.shape
    row = lambda i: (i, 0)
    fixed = lambda i: (0, 0)
    return pl.pallas_call(
        functools.partial(_outproj_kernel, final_norm=final_norm),
        grid=(m // OUT_ROW_TILE,),
        in_specs=[
            pl.BlockSpec((OUT_ROW_TILE, A_W), row),
            pl.BlockSpec((OUT_ROW_TILE, B_W), row),
            pl.BlockSpec((OUT_ROW_TILE, C_W), row),
            pl.BlockSpec((OUT_ROW_TILE, d), row),
            pl.BlockSpec((None, GATE_W, d), lambda i: (layer, 0, 0)),
            pl.BlockSpec((1, d), fixed),
        ],
        out_specs=pl.BlockSpec((OUT_ROW_TILE, d), row),
        out_shape=jax.ShapeDtypeStruct((m, d), F32),
        compiler_params=pltpu.CompilerParams(
            dimension_semantics=("parallel",), vmem_limit_bytes=VMEM_LIMIT_BYTES),
        name="outproj",
    )(ya, yb, yc, x2, w, fg)


def _split_w_in(w):
    splits = (A_W, A_W, A_W, A_W, B_W, B_KV_W, B_KV_W, B_W, C_W, C_W, C_W, C_W)
    pts = np.cumsum(splits)[:-1]
    aq, ak, av, ag, bq, bk, bv, bg, cq, ck, cv, cg = jnp.split(w, pts, axis=-1)
    depth, d = w.shape[:2]
    bq = bq.reshape(depth, d, B_KV_HEADS, B_GROUP, HEAD_DIM).transpose(0, 1, 3, 2, 4)
    bq = bq.reshape(depth, d, B_W)
    wrm = jnp.concatenate([ak, bq, bk, cq, ck, bg, ag, cg], axis=-1)
    wt = jnp.swapaxes(jnp.concatenate([av, bv, cv, aq], axis=-1), 1, 2)
    return wrm.astype(BF16), wt.astype(BF16)


def _q_scales():
    cs = np.ones((1, OFF_GATE), np.float32)
    cs[0, OFF_BQ:OFF_BK] = HEAD_DIM ** -0.5 * LOG2_E
    cs[0, OFF_CQ:OFF_CK] = HEAD_DIM ** -0.5 * LOG2_E
    ts = np.ones((T_W, 1), np.float32)
    ts[TOFF_AQ:, 0] = A_QK_DIM ** -0.5 * LOG2_E
    return jnp.asarray(cs), jnp.asarray(ts)


def kernel(x, norm_g, w_in, lambda_q1, lambda_k1, lambda_q2, lambda_k2, subln_g, sinks, w_out, final_g):
    b, s, d = x.shape
    depth = w_in.shape[0]
    m = b * s
    slopes_a, slopes_b = _alibi_slopes()

    pos = jnp.arange(s, dtype=jnp.int32)
    p_hi = (pos // 256).astype(BF16)
    p_lo = (pos % 256).astype(BF16)
    kfeat = jnp.stack([p_hi] * 3 + [p_lo] * 3, axis=-1)
    kfeat = jnp.pad(kfeat, ((0, 0), (0, LANES - 6)))
    qfeat_np = np.zeros((LANES, A_HEADS), np.float32)
    for h, sl in enumerate(slopes_a):
        sl2 = np.float32(np.float32(sl) * np.float32(LOG2_E))
        qfeat_np[:6, h] = _split3_bf16(np.float32(256.0) * sl2) + _split3_bf16(sl2)
    qfeat = jnp.asarray(qfeat_np)

    tri = (lax.broadcasted_iota(jnp.int32, (C_TILE, C_TILE), 1)
           > lax.broadcasted_iota(jnp.int32, (C_TILE, C_TILE), 0)).astype(BF16)
    tri2 = jnp.concatenate([tri, tri], axis=1)
    b_bias = _b_bias(slopes_b)
    cs, ts = _q_scales()

    x2 = x.reshape(m, d)
    wrm_all, wt_all = _split_w_in(w_in)
    w_out_all = w_out.astype(BF16)
    sg_all = jnp.tile(subln_g, (1, A_HEADS)).reshape(depth, -1, 1)
    sink_rows = jnp.repeat(sinks * LOG2_E, B_TILE, axis=1).reshape(depth, 1, -1)
    for l in range(depth):
        lambda_init = 0.8 - 0.6 * math.exp(-0.3 * l)
        ak, bq, bk, cq, ck, gate, avt, bvt, cvt, aqt = _inproj(
            x2, norm_g[l].reshape(1, d), wrm_all, wt_all, cs, ts, l)
        ya = _attn_a(aqt, ak, kfeat, avt, gate, qfeat,
                     lambda_q1[l].reshape(1, -1), lambda_k1[l].reshape(1, -1),
                     lambda_q2[l].reshape(1, -1), lambda_k2[l].reshape(1, -1),
                     sg_all[l], lambda_init, b, s)
        yb = _attn_b(bq, bk, bvt, gate, b_bias, sink_rows[l], b, s)
        yc = _attn_c(cq, ck, cvt, tri2, gate, b, s)
        x2 = _outproj(ya, yb, yc, x2, w_out_all, final_g.reshape(1, d), l,
                      final_norm=(l == depth - 1))
    return x2.reshape(b, s, d)
```

```python
import functools
import math

import numpy as np
import jax
import jax.numpy as jnp
from jax import lax
from jax.experimental import pallas as pl
from jax.experimental.pallas import tpu as pltpu

F32 = jnp.float32
BF16 = jnp.bfloat16

HEAD_DIM = 64
NORM_EPS = 1e-5
A_HEADS = 4
A_QK_DIM = HEAD_DIM // 2
B_HEADS = 8
B_KV_HEADS = 2
B_GROUP = B_HEADS // B_KV_HEADS
C_HEADS = 4
WINDOW = 128
A_W = A_HEADS * HEAD_DIM
B_W = B_HEADS * HEAD_DIM
B_KV_W = B_KV_HEADS * HEAD_DIM
C_W = C_HEADS * HEAD_DIM
GATE_W = A_W + B_W + C_W
LANES = 128
MASK_VALUE = -1e30
LOG2_E = math.log2(math.e)
C_ZERO_WEIGHT_LOG2 = -105.0 * LOG2_E
VMEM_LIMIT_BYTES = 48 * 1024 * 1024

ROW_TILE = 512
OUT_ROW_TILE = 1024
A_TILE = 256
A_KEYS = 2 * A_TILE
A_NORM_ROWS = 16
C_TILE = 256
C_SUBTILES = 2
B_TILE = WINDOW
B_BLOCKS = 8

OFF_AK = 0
OFF_BQ = OFF_AK + A_W
OFF_BK = OFF_BQ + B_W
OFF_CQ = OFF_BK + B_KV_W
OFF_CK = OFF_CQ + C_W
OFF_GATE = OFF_CK + C_W
RM_W = OFF_GATE + GATE_W
TOFF_AV = 0
TOFF_BV = TOFF_AV + A_W
TOFF_CV = TOFF_BV + B_KV_W
TOFF_AQ = TOFF_CV + C_W
T_W = TOFF_AQ + A_W
GATE_BLOCK_A = B_W // A_W
GATE_BLOCK_C = (B_W + A_W) // C_W


def _alibi_slopes():
    n = A_HEADS + B_HEADS
    s = (2.0 ** (-8.0 * np.arange(1, n + 1) / n)).astype(np.float32)
    return s[B_HEADS:], s[:B_HEADS]


def _split3_bf16(v):
    out = []
    r = np.float32(v)
    for _ in range(3):
        p = np.asarray(r, np.float32).astype(BF16).astype(np.float32)
        out.append(float(p))
        r = np.float32(r - p)
    return out


def _dot_nt(a, b):
    return lax.dot_general(a, b, (((1,), (1,)), ((), ())), preferred_element_type=F32)


def _silu(g):
    return g * jax.nn.sigmoid(g)


def _inproj_kernel(x_ref, g_ref, wrm_ref, wt_ref, cs_ref, ts_ref,
                   ak_ref, bq_ref, bk_ref, cq_ref, ck_ref, gate_ref,
                   avt_ref, bvt_ref, cvt_ref, aqt_ref):
    x = x_ref[...]
    ms = jnp.mean(x * x, axis=-1, keepdims=True)
    h = (x * lax.rsqrt(ms + NORM_EPS) * g_ref[...]).astype(BF16)
    qk = jnp.dot(h, wrm_ref[:, :OFF_GATE], preferred_element_type=F32) * cs_ref[...]
    ak_ref[...] = qk[:, OFF_AK:OFF_BQ].astype(BF16)
    bq_ref[...] = qk[:, OFF_BQ:OFF_BK].astype(BF16)
    bk_ref[...] = qk[:, OFF_BK:OFF_CQ].astype(BF16)
    cq_ref[...] = qk[:, OFF_CQ:OFF_CK].astype(BF16)
    ck_ref[...] = qk[:, OFF_CK:OFF_GATE].astype(BF16)
    gate_ref[...] = jnp.dot(h, wrm_ref[:, OFF_GATE:], preferred_element_type=F32)
    tr = (_dot_nt(wt_ref[...], h) * ts_ref[...]).astype(BF16)
    for c in range(ROW_TILE // A_TILE):
        avt_ref[c] = tr[TOFF_AV:TOFF_BV, c * A_TILE:(c + 1) * A_TILE]
        aqt_ref[c] = tr[TOFF_AQ:, c * A_TILE:(c + 1) * A_TILE]
    for c in range(ROW_TILE // B_TILE):
        bvt_ref[c] = tr[TOFF_BV:TOFF_CV, c * B_TILE:(c + 1) * B_TILE]
    for c in range(ROW_TILE // C_TILE):
        cvt_ref[c] = tr[TOFF_CV:TOFF_AQ, c * C_TILE:(c + 1) * C_TILE]


def _inproj(x2, g, wrm, wt, cs, ts, layer):
    m, d = x2.shape
    tm = ROW_TILE
    row = lambda i: (i, 0)
    fixed = lambda i: (0, 0)
    tiles = lambda i: (i, 0, 0)
    this_layer = lambda i: (layer, 0, 0)
    widths = (A_W, B_W, B_KV_W, C_W, C_W)
    return pl.pallas_call(
        _inproj_kernel,
        grid=(m // tm,),
        in_specs=[
            pl.BlockSpec((tm, d), row),
            pl.BlockSpec((1, d), fixed),
            pl.BlockSpec((None, d, RM_W), this_layer),
            pl.BlockSpec((None, T_W, d), this_layer),
            pl.BlockSpec((1, OFF_GATE), fixed),
            pl.BlockSpec((T_W, 1), fixed),
        ],
        out_specs=[pl.BlockSpec((tm, w), row) for w in widths] + [
            pl.BlockSpec((tm, GATE_W), row),
            pl.BlockSpec((tm // A_TILE, A_W, A_TILE), tiles),
            pl.BlockSpec((tm // B_TILE, B_KV_W, B_TILE), tiles),
            pl.BlockSpec((tm // C_TILE, C_W, C_TILE), tiles),
            pl.BlockSpec((tm // A_TILE, A_W, A_TILE), tiles),
        ],
        out_shape=[jax.ShapeDtypeStruct((m, w), BF16) for w in widths] + [
            jax.ShapeDtypeStruct((m, GATE_W), F32),
            jax.ShapeDtypeStruct((m // A_TILE, A_W, A_TILE), BF16),
            jax.ShapeDtypeStruct((m // B_TILE, B_KV_W, B_TILE), BF16),
            jax.ShapeDtypeStruct((m // C_TILE, C_W, C_TILE), BF16),
            jax.ShapeDtypeStruct((m // A_TILE, A_W, A_TILE), BF16),
        ],
        compiler_params=pltpu.CompilerParams(
            dimension_semantics=("parallel",), vmem_limit_bytes=VMEM_LIMIT_BYTES),
        name="inproj",
    )(x2, g, wrm, wt, cs, ts)


def _attn_a_kernel(lq1_ref, lk1_ref, lq2_ref, lk2_ref, sg_ref, qfeat_ref,
                   qt_ref, k_ref, kfeat_ref, vt_ref, gate_ref, o_ref,
                   qz_sc, m_sc, acc_sc, s_sc, smax_sc, *, lambda_init):
    t = A_TILE
    tk = A_KEYS
    qi = pl.program_id(1)
    n_full = qi // 2
    m_sc[...] = jnp.full(m_sc.shape, MASK_VALUE, F32)
    acc_sc[...] = jnp.zeros(acc_sc.shape, F32)
    ones_rows = jnp.ones((A_NORM_ROWS, tk), BF16)

    sub = lax.broadcasted_iota(jnp.int32, (LANES, t), 0)
    for pr in range(2):
        qp = qt_ref[pr * LANES:(pr + 1) * LANES, :]
        for c in range(4):
            hh, mp = divmod(c, 2)
            lo = hh * HEAD_DIM + mp * A_QK_DIM
            real = jnp.where((sub >= lo) & (sub < lo + A_QK_DIM), qp, jnp.zeros_like(qp))
            h = 2 * pr + hh
            feat = jnp.broadcast_to(qfeat_ref[:, h:h + 1], (LANES, t)).astype(BF16)
            qz_sc[pr, :, c * t:(c + 1) * t] = jnp.concatenate([real, feat], axis=0)

    def key_rows(pr, step, nk):
        ks = pl.ds(pl.multiple_of(step * tk, tk), nk)
        return jnp.concatenate([k_ref[ks, pr * LANES:(pr + 1) * LANES], kfeat_ref[ks, :]], axis=1)

    def scores(pr, c, lhs):
        return jnp.dot(lhs, qz_sc[pr, :, c * t:(c + 1) * t], preferred_element_type=F32)

    def softmax_block(pr, c, nk):
        cols = slice(c * t, (c + 1) * t)
        m_old = m_sc[pr, :, cols]
        m_new = jnp.maximum(m_old, smax_sc[pr, :, cols])
        m_sc[pr, :, cols] = m_new
        return jnp.exp2(m_old - m_new), jnp.exp2((s_sc[pr, :nk, cols] - m_new).astype(BF16))

    def value_block(pr, c, step, nk, alpha, p):
        hh, mp = divmod(c, 2)
        h = 2 * pr + hh
        rows = slice(h * HEAD_DIM, (h + 1) * HEAD_DIM)
        ocols = slice(mp * t, (mp + 1) * t)
        vt = jnp.concatenate([vt_ref[2 * step + blk, rows, :] for blk in range(nk // t)], axis=1)
        vt1 = jnp.concatenate([vt, ones_rows[:, :nk]], axis=0)
        pv = jnp.dot(vt1, p, preferred_element_type=F32)
        acc_sc[h, :, ocols] = alpha * acc_sc[h, :, ocols] + pv

    def causal_mask(nk):
        kmq = (lax.broadcasted_iota(jnp.int32, (nk, t), 0)
               - lax.broadcasted_iota(jnp.int32, (nk, t), 1))
        return kmq <= qi * t - n_full * tk

    def stage_scores(pr, c, s):
        cols = slice(c * t, (c + 1) * t)
        s_sc[pr, :s.shape[0], cols] = s
        smax_sc[pr, :, cols] = jnp.max(s, axis=0, keepdims=True)

    def trip(i, nk_cur=tk, nk_next=tk, masked_next=False):
        lhs = [key_rows(pr, i + 1, nk_next) if nk_next else None for pr in range(2)]
        for c in range(4):
            for pr in range(2):
                if nk_next:
                    sn = scores(pr, c, lhs[pr])
                alpha, p = softmax_block(pr, c, nk_cur)
                value_block(pr, c, i, nk_cur, alpha, p)
                if nk_next:
                    if masked_next:
                        sn = jnp.where(causal_mask(nk_next), sn, MASK_VALUE)
                    stage_scores(pr, c, sn)

    keep = jnp.logical_or(causal_mask(tk), n_full > 0)
    for pr in range(2):
        lhs0 = key_rows(pr, 0, tk)
        for c in range(4):
            stage_scores(pr, c, jnp.where(keep, scores(pr, c, lhs0), MASK_VALUE))

    def body(i, carry):
        trip(i)
        return carry

    lax.fori_loop(0, n_full - 1, body, 0)

    short = jnp.logical_and(qi % 2 == 0, n_full > 0)

    def tail(nk):
        @pl.when(n_full > 0)
        def _():
            trip(n_full - 1, nk_next=nk, masked_next=True)

        trip(n_full, nk_cur=nk, nk_next=0)

    @pl.when(short)
    def _():
        tail(t)

    @pl.when(jnp.logical_not(short))
    def _():
        tail(tk)

    lam = (jnp.exp(jnp.sum(lq1_ref[...] * lk1_ref[...], axis=-1, keepdims=True))
           - jnp.exp(jnp.sum(lq2_ref[...] * lk2_ref[...], axis=-1, keepdims=True))
           + lambda_init)
    outs = []
    for h in range(A_HEADS):
        pr, hh = divmod(h, 2)
        acc = acc_sc[h, :HEAD_DIM, :]
        l = acc_sc[h, HEAD_DIM:HEAD_DIM + 1, :]
        o = acc[:, :t] / l[:, :t] - lam * (acc[:, t:] / l[:, t:])
        ms = jnp.mean(o * o, axis=0, keepdims=True)
        outs.append(o * lax.rsqrt(ms + NORM_EPS))
    o_all = jnp.concatenate(outs, axis=0) * sg_ref[...] * (1.0 - lambda_init)
    o_ref[...] = (o_all.T * _silu(gate_ref[...])).astype(BF16)


def _attn_a(aqt, ak, kfeat, avt, gate, qfeat, lq1, lk1, lq2, lk2, sg4, lambda_init, b, s):
    t = A_TILE
    tk = A_KEYS
    nt = s // t
    small = lambda shape: pl.BlockSpec(shape, lambda bi, qi: (0, 0))
    return pl.pallas_call(
        functools.partial(_attn_a_kernel, lambda_init=lambda_init),
        grid=(b, nt),
        in_specs=[
            small((1, A_QK_DIM)), small((1, A_QK_DIM)), small((1, A_QK_DIM)), small((1, A_QK_DIM)),
            small((A_W, 1)),
            small((LANES, A_HEADS)),
            pl.BlockSpec((None, A_W, t), lambda bi, qi: (bi * nt + qi, 0, 0)),
            pl.BlockSpec((s, A_W), lambda bi, qi: (bi, 0)),
            pl.BlockSpec((s, LANES), lambda bi, qi: (0, 0)),
            pl.BlockSpec((nt, A_W, t), lambda bi, qi: (bi, 0, 0)),
            pl.BlockSpec((t, A_W), lambda bi, qi: (bi * nt + qi, GATE_BLOCK_A)),
        ],
        out_specs=pl.BlockSpec((t, A_W), lambda bi, qi: (bi * nt + qi, 0)),
        out_shape=jax.ShapeDtypeStruct((b * s, A_W), BF16),
        scratch_shapes=[
            pltpu.VMEM((2, 2 * LANES, 4 * t), BF16),
            pltpu.VMEM((2, 1, 4 * t), F32),
            pltpu.VMEM((A_HEADS, HEAD_DIM + A_NORM_ROWS, 2 * t), F32),
            pltpu.VMEM((2, tk, 4 * t), F32),
            pltpu.VMEM((2, 1, 4 * t), F32),
        ],
        compiler_params=pltpu.CompilerParams(
            dimension_semantics=("parallel", "arbitrary"),
            vmem_limit_bytes=VMEM_LIMIT_BYTES),
        name="attn_a",
    )(lq1, lk1, lq2, lk2, sg4, qfeat, aqt, ak, kfeat, avt, gate)


def _attn_b_kernel(q_ref, kp_ref, kc_ref, vtp_ref, vtc_ref, gate_ref, bias_ref, sink_ref, o_ref):
    t = B_TILE
    qi = pl.program_id(1)
    lane = lax.broadcasted_iota(jnp.int32, (t, LANES), 1)
    sink = sink_ref[...]
    k_blocks = [kp_ref[...]] + [kc_ref[c * t:(c + 1) * t, :] for c in range(B_BLOCKS)]
    vt_blocks = [vtp_ref[...]] + [vtc_ref[c] for c in range(B_BLOCKS)]
    for c in range(B_BLOCKS):
        rows = slice(c * t, (c + 1) * t)
        k_all = jnp.concatenate(k_blocks[c:c + 2], axis=0)
        vt_all = jnp.concatenate(vt_blocks[c:c + 2], axis=1)
        pieces = [None] * B_HEADS
        for j in range(B_GROUP):
            qp = q_ref[rows, j * LANES:(j + 1) * LANES]
            zero = jnp.zeros_like(qp)
            pieces[j] = jnp.where(lane < HEAD_DIM, qp, zero)
            pieces[B_GROUP + j] = jnp.where(lane >= HEAD_DIM, qp, zero)
        qz = jnp.concatenate(pieces, axis=0)
        bias = bias_ref[jnp.minimum(qi, 1)] if c == 0 else bias_ref[1]
        s = _dot_nt(k_all, qz) + bias
        m = jnp.maximum(jnp.max(s, axis=0, keepdims=True), sink)
        p = jnp.exp2(s - m)
        inv = 1.0 / (jnp.sum(p, axis=0, keepdims=True) + jnp.exp2(sink - m))
        pb = p.astype(BF16)
        outs = []
        for g in range(B_KV_HEADS):
            cols = slice(g * B_GROUP * t, (g + 1) * B_GROUP * t)
            og = jnp.dot(vt_all[g * HEAD_DIM:(g + 1) * HEAD_DIM, :], pb[:, cols],
                         preferred_element_type=F32) * inv[:, cols]
            outs.extend(og[:, j * t:(j + 1) * t] for j in range(B_GROUP))
        o_all = jnp.concatenate(outs, axis=0)
        o_ref[rows, :] = (o_all.T * _silu(gate_ref[rows, :])).astype(BF16)


def _attn_b(bq, bk, bvt, gate, bias, sink_row, b, s):
    t = B_TILE
    tq = B_BLOCKS * t
    nt = s // tq
    cur = lambda bi, qi: (bi * nt + qi, 0)
    prev = lambda bi, qi: (jnp.maximum((bi * nt + qi) * B_BLOCKS - 1, 0), 0)
    cur3 = lambda bi, qi: (bi * nt + qi, 0, 0)
    prev3 = lambda bi, qi: (jnp.maximum((bi * nt + qi) * B_BLOCKS - 1, 0), 0, 0)
    return pl.pallas_call(
        _attn_b_kernel,
        grid=(b, nt),
        in_specs=[
            pl.BlockSpec((tq, B_W), cur),
            pl.BlockSpec((t, B_KV_W), prev),
            pl.BlockSpec((tq, B_KV_W), cur),
            pl.BlockSpec((None, B_KV_W, t), prev3),
            pl.BlockSpec((B_BLOCKS, B_KV_W, t), cur3),
            pl.BlockSpec((tq, B_W), cur),
            pl.BlockSpec((2, 2 * t, B_HEADS * t), lambda bi, qi: (0, 0, 0)),
            pl.BlockSpec((1, B_HEADS * t), lambda bi, qi: (0, 0)),
        ],
        out_specs=pl.BlockSpec((tq, B_W), cur),
        out_shape=jax.ShapeDtypeStruct((b * s, B_W), BF16),
        compiler_params=pltpu.CompilerParams(
            dimension_semantics=("parallel", "arbitrary"),
            vmem_limit_bytes=VMEM_LIMIT_BYTES),
        name="attn_b",
    )(bq, bk, bk, bvt, bvt, gate, bias, sink_row)


def _b_bias(slopes):
    t = B_TILE
    koff = np.arange(2 * t)[:, None] - t
    qoff = np.arange(t)[None, :]
    dist = qoff - koff
    band = (dist >= 0) & (dist < WINDOW)
    out = np.empty((2, 2 * t, B_HEADS * t), np.float32)
    for h in range(B_HEADS):
        term = (-np.float32(slopes[h]) * dist.astype(np.float32) * np.float32(LOG2_E)).astype(np.float32)
        out[1, :, h * t:(h + 1) * t] = np.where(band, term, np.float32(MASK_VALUE))
        out[0, :, h * t:(h + 1) * t] = np.where(band & (koff >= 0), term, np.float32(MASK_VALUE))
    return jnp.asarray(out)


def _attn_c_kernel(q_ref, k_ref, vt_ref, u_ref, gate_ref, o_ref, qz_sc, carry_sc, acc_sc):
    t = C_TILE
    step = pl.program_id(1)
    lane = lax.broadcasted_iota(jnp.int32, (t, C_W), 1)
    for sub in range(C_SUBTILES):
        q = q_ref[sub * t:(sub + 1) * t, :]
        for h in range(C_HEADS):
            own = (lane >= h * HEAD_DIM) & (lane < (h + 1) * HEAD_DIM)
            qz_sc[sub, h * t:(h + 1) * t, :] = jnp.where(own, q, jnp.zeros_like(q))
    u2 = u_ref[...]

    def tile(sub, kj, masked, carry_in):
        ks = pl.ds(pl.multiple_of(kj * t, t), t)
        z = _dot_nt(k_ref[ks, :], qz_sc[sub])
        lse = jnp.log2(1.0 + jnp.exp2(jnp.minimum(z, -z)))
        log_beta = jnp.minimum(z, 0.0) - lse
        l1m = log_beta - z
        if masked:
            krow = lax.broadcasted_iota(jnp.int32, (t, C_HEADS * t), 0)
            qcol = lax.broadcasted_iota(jnp.int32, (t, C_HEADS * t), 1) & (t - 1)
            past = krow < qcol
            l1m = jnp.where(past, l1m, 0.0)
        hi = l1m.astype(BF16)
        lo = (l1m - hi.astype(F32)).astype(BF16)
        tail = jnp.dot(u2, jnp.concatenate([hi, lo], axis=0),
                       preferred_element_type=F32)
        a = jnp.exp2(log_beta + tail + carry_in)
        if masked:
            a = jnp.where(past, a, 0.0)
        a = a.astype(BF16)
        parts = [jnp.dot(vt_ref[kj, h * HEAD_DIM:(h + 1) * HEAD_DIM, :], a[:, h * t:(h + 1) * t],
                         preferred_element_type=F32) for h in range(C_HEADS)]
        return jnp.concatenate(parts, axis=0), carry_in + jnp.sum(l1m, axis=0, keepdims=True)

    tiles = [C_SUBTILES * step + sub for sub in range(C_SUBTILES)]
    alive = []
    for sub in range(C_SUBTILES):
        acc, carry = tile(sub, tiles[sub], True, jnp.zeros((1, C_HEADS * t), F32))
        acc_sc[sub] = acc
        carry_sc[sub] = carry
        alive.append(jnp.logical_and(tiles[sub] > 0, jnp.max(carry) > C_ZERO_WEIGHT_LOG2))

    def cond(state):
        return functools.reduce(jnp.logical_or, state[1:])

    def body(state):
        r = state[0]
        nxt = []
        for sub in range(C_SUBTILES):
            has_keys = tiles[sub] - 1 - r >= 0
            acc, carry = tile(sub, jnp.maximum(tiles[sub] - 1 - r, 0), False, carry_sc[sub])
            acc_sc[sub] += jnp.where(has_keys, acc, 0.0)
            carry_sc[sub] = jnp.where(has_keys, carry, carry_sc[sub])
            more_keys = tiles[sub] - 2 - r >= 0
            nxt.append(jnp.logical_and(more_keys, jnp.max(carry) > C_ZERO_WEIGHT_LOG2))
        return (r + 1, *nxt)

    lax.while_loop(cond, body, (jnp.int32(0), *alive))
    for sub in range(C_SUBTILES):
        rows = slice(sub * t, (sub + 1) * t)
        o_ref[rows, :] = (acc_sc[sub].T * _silu(gate_ref[rows, :])).astype(BF16)


def _attn_c(cq, ck, cvt, u2, gate, b, s):
    t = C_TILE
    tq = C_SUBTILES * t
    nt = s // t
    nq = s // tq
    return pl.pallas_call(
        _attn_c_kernel,
        grid=(b, nq),
        in_specs=[
            pl.BlockSpec((tq, C_W), lambda bi, qi: (bi * nq + qi, 0)),
            pl.BlockSpec((s, C_W), lambda bi, qi: (bi, 0)),
            pl.BlockSpec((nt, C_W, t), lambda bi, qi: (bi, 0, 0)),
            pl.BlockSpec((t, 2 * t), lambda bi, qi: (0, 0)),
            pl.BlockSpec((tq, C_W), lambda bi, qi: (bi * nq + qi, GATE_BLOCK_C)),
        ],
        out_specs=pl.BlockSpec((tq, C_W), lambda bi, qi: (bi * nq + qi, 0)),
        out_shape=jax.ShapeDtypeStruct((b * s, C_W), BF16),
        scratch_shapes=[
            pltpu.VMEM((C_SUBTILES, C_HEADS * t, C_W), BF16),
            pltpu.VMEM((C_SUBTILES, 1, C_HEADS * t), F32),
            pltpu.VMEM((C_SUBTILES, C_W, t), F32),
        ],
        compiler_params=pltpu.CompilerParams(
            dimension_semantics=("parallel", "arbitrary"),
            vmem_limit_bytes=VMEM_LIMIT_BYTES),
        name="attn_c",
    )(cq, ck, cvt, u2, gate)


def _outproj_kernel(ya_ref, yb_ref, yc_ref, x_ref, w_ref, fg_ref, o_ref, *, final_norm):
    y = jnp.concatenate([ya_ref[...], yb_ref[...], yc_ref[...]], axis=1)
    r = x_ref[...] + jnp.dot(y, w_ref[...], preferred_element_type=F32)
    if final_norm:
        ms = jnp.mean(r * r, axis=-1, keepdims=True)
        r = r * lax.rsqrt(ms + NORM_EPS) * fg_ref[...]
    o_ref[...] = r


def _outproj(ya, yb, yc, x2, w, fg, layer, final_norm):
    m, d = x2.shape
    row = lambda i: (i, 0)
    fixed = lambda i: (0, 0)
    return pl.pallas_call(
        functools.partial(_outproj_kernel, final_norm=final_norm),
        grid=(m // OUT_ROW_TILE,),
        in_specs=[
            pl.BlockSpec((OUT_ROW_TILE, A_W), row),
            pl.BlockSpec((OUT_ROW_TILE, B_W), row),
            pl.BlockSpec((OUT_ROW_TILE, C_W), row),
            pl.BlockSpec((OUT_ROW_TILE, d), row),
            pl.BlockSpec((None, GATE_W, d), lambda i: (layer, 0, 0)),
            pl.BlockSpec((1, d), fixed),
        ],
        out_specs=pl.BlockSpec((OUT_ROW_TILE, d), row),
        out_shape=jax.ShapeDtypeStruct((m, d), F32),
        compiler_params=pltpu.CompilerParams(
            dimension_semantics=("parallel",), vmem_limit_bytes=VMEM_LIMIT_BYTES),
        name="outproj",
    )(ya, yb, yc, x2, w, fg)


def _split_w_in(w):
    splits = (A_W, A_W, A_W, A_W, B_W, B_KV_W, B_KV_W, B_W, C_W, C_W, C_W, C_W)
    pts = np.cumsum(splits)[:-1]
    aq, ak, av, ag, bq, bk, bv, bg, cq, ck, cv, cg = jnp.split(w, pts, axis=-1)
    depth, d = w.shape[:2]
    bq = bq.reshape(depth, d, B_KV_HEADS, B_GROUP, HEAD_DIM).transpose(0, 1, 3, 2, 4)
    bq = bq.reshape(depth, d, B_W)
    wrm = jnp.concatenate([ak, bq, bk, cq, ck, bg, ag, cg], axis=-1)
    wt = jnp.swapaxes(jnp.concatenate([av, bv, cv, aq], axis=-1), 1, 2)
    return wrm.astype(BF16), wt.astype(BF16)


def _q_scales():
    cs = np.ones((1, OFF_GATE), np.float32)
    cs[0, OFF_BQ:OFF_BK] = HEAD_DIM ** -0.5 * LOG2_E
    cs[0, OFF_CQ:OFF_CK] = HEAD_DIM ** -0.5 * LOG2_E
    ts = np.ones((T_W, 1), np.float32)
    ts[TOFF_AQ:, 0] = A_QK_DIM ** -0.5 * LOG2_E
    return jnp.asarray(cs), jnp.asarray(ts)


def kernel(x, norm_g, w_in, lambda_q1, lambda_k1, lambda_q2, lambda_k2, subln_g, sinks, w_out, final_g):
    b, s, d = x.shape
    depth = w_in.shape[0]
    m = b * s
    slopes_a, slopes_b = _alibi_slopes()

    pos = jnp.arange(s, dtype=jnp.int32)
    p_hi = (pos // 256).astype(BF16)
    p_lo = (pos % 256).astype(BF16)
    kfeat = jnp.stack([p_hi] * 3 + [p_lo] * 3, axis=-1)
    kfeat = jnp.pad(kfeat, ((0, 0), (0, LANES - 6)))
    qfeat_np = np.zeros((LANES, A_HEADS), np.float32)
    for h, sl in enumerate(slopes_a):
        sl2 = np.float32(np.float32(sl) * np.float32(LOG2_E))
        qfeat_np[:6, h] = _split3_bf16(np.float32(256.0) * sl2) + _split3_bf16(sl2)
    qfeat = jnp.asarray(qfeat_np)

    tri = (lax.broadcasted_iota(jnp.int32, (C_TILE, C_TILE), 1)
           > lax.broadcasted_iota(jnp.int32, (C_TILE, C_TILE), 0)).astype(BF16)
    tri2 = jnp.concatenate([tri, tri], axis=1)
    b_bias = _b_bias(slopes_b)
    cs, ts = _q_scales()

    x2 = x.reshape(m, d)
    wrm_all, wt_all = _split_w_in(w_in)
    w_out_all = w_out.astype(BF16)
    sg_all = jnp.tile(subln_g, (1, A_HEADS)).reshape(depth, -1, 1)
    sink_rows = jnp.repeat(sinks * LOG2_E, B_TILE, axis=1).reshape(depth, 1, -1)
    for l in range(depth):
        lambda_init = 0.8 - 0.6 * math.exp(-0.3 * l)
        ak, bq, bk, cq, ck, gate, avt, bvt, cvt, aqt = _inproj(
            x2, norm_g[l].reshape(1, d), wrm_all, wt_all, cs, ts, l)
        ya = _attn_a(aqt, ak, kfeat, avt, gate, qfeat,
                     lambda_q1[l].reshape(1, -1), lambda_k1[l].reshape(1, -1),
                     lambda_q2[l].reshape(1, -1), lambda_k2[l].reshape(1, -1),
                     sg_all[l], lambda_init, b, s)
        yb = _attn_b(bq, bk, bvt, gate, b_bias, sink_rows[l], b, s)
        yc = _attn_c(cq, ck, cvt, tri2, gate, b, s)
        x2 = _outproj(ya, yb, yc, x2, w_out_all, final_g.reshape(1, d), l,
                      final_norm=(l == depth - 1))
    return x2.reshape(b, s, d)
```

```python
import functools
import math

import numpy as np
import jax
import jax.numpy as jnp
from jax import lax
from jax.experimental import pallas as pl
from jax.experimental.pallas import tpu as pltpu

F32 = jnp.float32
BF16 = jnp.bfloat16

HEAD_DIM = 64
NORM_EPS = 1e-5
A_HEADS = 4
A_QK_DIM = HEAD_DIM // 2
B_HEADS = 8
B_KV_HEADS = 2
B_GROUP = B_HEADS // B_KV_HEADS
C_HEADS = 4
WINDOW = 128
A_W = A_HEADS * HEAD_DIM
B_W = B_HEADS * HEAD_DIM
B_KV_W = B_KV_HEADS * HEAD_DIM
C_W = C_HEADS * HEAD_DIM
GATE_W = A_W + B_W + C_W
LANES = 128
MASK_VALUE = -1e30
LOG2_E = math.log2(math.e)
C_ZERO_WEIGHT_LOG2 = -105.0 * LOG2_E
VMEM_LIMIT_BYTES = 48 * 1024 * 1024

ROW_TILE = 512
OUT_ROW_TILE = 1024
A_TILE = 256
A_KEYS = 2 * A_TILE
A_NORM_ROWS = 16
C_TILE = 256
C_SUBTILES = 2
B_TILE = WINDOW
B_BLOCKS = 8

OFF_AK = 0
OFF_BQ = OFF_AK + A_W
OFF_BK = OFF_BQ + B_W
OFF_CQ = OFF_BK + B_KV_W
OFF_CK = OFF_CQ + C_W
OFF_GATE = OFF_CK + C_W
RM_W = OFF_GATE + GATE_W
TOFF_AV = 0
TOFF_BV = TOFF_AV + A_W
TOFF_CV = TOFF_BV + B_KV_W
TOFF_AQ = TOFF_CV + C_W
T_W = TOFF_AQ + A_W
GATE_BLOCK_A = B_W // A_W
GATE_BLOCK_C = (B_W + A_W) // C_W


def _alibi_slopes():
    n = A_HEADS + B_HEADS
    s = (2.0 ** (-8.0 * np.arange(1, n + 1) / n)).astype(np.float32)
    return s[B_HEADS:], s[:B_HEADS]


def _split3_bf16(v):
    out = []
    r = np.float32(v)
    for _ in range(3):
        p = np.asarray(r, np.float32).astype(BF16).astype(np.float32)
        out.append(float(p))
        r = np.float32(r - p)
    return out


def _dot_nt(a, b):
    return lax.dot_general(a, b, (((1,), (1,)), ((), ())), preferred_element_type=F32)


def _silu(g):
    return g * jax.nn.sigmoid(g)


def _inproj_kernel(x_ref, g_ref, wrm_ref, wt_ref, cs_ref, ts_ref,
                   ak_ref, bq_ref, bk_ref, cq_ref, ck_ref, gate_ref,
                   avt_ref, bvt_ref, cvt_ref, aqt_ref):
    x = x_ref[...]
    ms = jnp.mean(x * x, axis=-1, keepdims=True)
    h = (x * lax.rsqrt(ms + NORM_EPS) * g_ref[...]).astype(BF16)
    qk = jnp.dot(h, wrm_ref[:, :OFF_GATE], preferred_element_type=F32) * cs_ref[...]
    ak_ref[...] = qk[:, OFF_AK:OFF_BQ].astype(BF16)
    bq_ref[...] = qk[:, OFF_BQ:OFF_BK].astype(BF16)
    bk_ref[...] = qk[:, OFF_BK:OFF_CQ].astype(BF16)
    cq_ref[...] = qk[:, OFF_CQ:OFF_CK].astype(BF16)
    ck_ref[...] = qk[:, OFF_CK:OFF_GATE].astype(BF16)
    gate_ref[...] = jnp.dot(h, wrm_ref[:, OFF_GATE:], preferred_element_type=F32)
    tr = (_dot_nt(wt_ref[...], h) * ts_ref[...]).astype(BF16)
    for c in range(ROW_TILE // A_TILE):
        avt_ref[c] = tr[TOFF_AV:TOFF_BV, c * A_TILE:(c + 1) * A_TILE]
        aqt_ref[c] = tr[TOFF_AQ:, c * A_TILE:(c + 1) * A_TILE]
    for c in range(ROW_TILE // B_TILE):
        bvt_ref[c] = tr[TOFF_BV:TOFF_CV, c * B_TILE:(c + 1) * B_TILE]
    for c in range(ROW_TILE // C_TILE):
        cvt_ref[c] = tr[TOFF_CV:TOFF_AQ, c * C_TILE:(c + 1) * C_TILE]


def _inproj(x2, g, wrm, wt, cs, ts, layer):
    m, d = x2.shape
    tm = ROW_TILE
    row = lambda i: (i, 0)
    fixed = lambda i: (0, 0)
    tiles = lambda i: (i, 0, 0)
    this_layer = lambda i: (layer, 0, 0)
    widths = (A_W, B_W, B_KV_W, C_W, C_W)
    return pl.pallas_call(
        _inproj_kernel,
        grid=(m // tm,),
        in_specs=[
            pl.BlockSpec((tm, d), row),
            pl.BlockSpec((1, d), fixed),
            pl.BlockSpec((None, d, RM_W), this_layer),
            pl.BlockSpec((None, T_W, d), this_layer),
            pl.BlockSpec((1, OFF_GATE), fixed),
            pl.BlockSpec((T_W, 1), fixed),
        ],
        out_specs=[pl.BlockSpec((tm, w), row) for w in widths] + [
            pl.BlockSpec((tm, GATE_W), row),
            pl.BlockSpec((tm // A_TILE, A_W, A_TILE), tiles),
            pl.BlockSpec((tm // B_TILE, B_KV_W, B_TILE), tiles),
            pl.BlockSpec((tm // C_TILE, C_W, C_TILE), tiles),
            pl.BlockSpec((tm // A_TILE, A_W, A_TILE), tiles),
        ],
        out_shape=[jax.ShapeDtypeStruct((m, w), BF16) for w in widths] + [
            jax.ShapeDtypeStruct((m, GATE_W), F32),
            jax.ShapeDtypeStruct((m // A_TILE, A_W, A_TILE), BF16),
            jax.ShapeDtypeStruct((m // B_TILE, B_KV_W, B_TILE), BF16),
            jax.ShapeDtypeStruct((m // C_TILE, C_W, C_TILE), BF16),
            jax.ShapeDtypeStruct((m // A_TILE, A_W, A_TILE), BF16),
        ],
        compiler_params=pltpu.CompilerParams(
            dimension_semantics=("parallel",), vmem_limit_bytes=VMEM_LIMIT_BYTES),
        name="inproj",
    )(x2, g, wrm, wt, cs, ts)


def _attn_a_kernel(lq1_ref, lk1_ref, lq2_ref, lk2_ref, sg_ref, qfeat_ref,
                   qt_ref, k_ref, kfeat_ref, vt_ref, gate_ref, o_ref,
                   qz_sc, m_sc, acc_sc, s_sc, smax_sc, *, lambda_init):
    t = A_TILE
    tk = A_KEYS
    qi = pl.program_id(1)
    n_full = qi // 2
    m_sc[...] = jnp.full(m_sc.shape, MASK_VALUE, F32)
    acc_sc[...] = jnp.zeros(acc_sc.shape, F32)
    ones_rows = jnp.ones((A_NORM_ROWS, tk), BF16)

    sub = lax.broadcasted_iota(jnp.int32, (LANES, t), 0)
    for pr in range(2):
        qp = qt_ref[pr * LANES:(pr + 1) * LANES, :]
        for c in range(4):
            hh, mp = divmod(c, 2)
            lo = hh * HEAD_DIM + mp * A_QK_DIM
            real = jnp.where((sub >= lo) & (sub < lo + A_QK_DIM), qp, jnp.zeros_like(qp))
            h = 2 * pr + hh
            feat = jnp.broadcast_to(qfeat_ref[:, h:h + 1], (LANES, t)).astype(BF16)
            qz_sc[pr, :, c * t:(c + 1) * t] = jnp.concatenate([real, feat], axis=0)

    def key_rows(pr, step, nk):
        ks = pl.ds(pl.multiple_of(step * tk, tk), nk)
        return jnp.concatenate([k_ref[ks, pr * LANES:(pr + 1) * LANES], kfeat_ref[ks, :]], axis=1)

    def scores(pr, c, lhs):
        return jnp.dot(lhs, qz_sc[pr, :, c * t:(c + 1) * t], preferred_element_type=F32)

    def softmax_block(pr, c, nk):
        cols = slice(c * t, (c + 1) * t)
        m_old = m_sc[pr, :, cols]
        m_new = jnp.maximum(m_old, smax_sc[pr, :, cols])
        m_sc[pr, :, cols] = m_new
        return jnp.exp2(m_old - m_new), jnp.exp2((s_sc[pr, :nk, cols] - m_new).astype(BF16))

    def value_block(pr, c, step, nk, alpha, p):
        hh, mp = divmod(c, 2)
        h = 2 * pr + hh
        rows = slice(h * HEAD_DIM, (h + 1) * HEAD_DIM)
        ocols = slice(mp * t, (mp + 1) * t)
        vt = jnp.concatenate([vt_ref[2 * step + blk, rows, :] for blk in range(nk // t)], axis=1)
        vt1 = jnp.concatenate([vt, ones_rows[:, :nk]], axis=0)
        pv = jnp.dot(vt1, p, preferred_element_type=F32)
        acc_sc[h, :, ocols] = alpha * acc_sc[h, :, ocols] + pv

    def causal_mask(nk):
        kmq = (lax.broadcasted_iota(jnp.int32, (nk, t), 0)
               - lax.broadcasted_iota(jnp.int32, (nk, t), 1))
        return kmq <= qi * t - n_full * tk

    def stage_scores(pr, c, s):
        cols = slice(c * t, (c + 1) * t)
        s_sc[pr, :s.shape[0], cols] = s
        smax_sc[pr, :, cols] = jnp.max(s, axis=0, keepdims=True)

    def trip(i, nk_cur=tk, nk_next=tk, masked_next=False):
        for pr in range(2):
            lhs = key_rows(pr, i + 1, nk_next) if nk_next else None
            for c in range(4):
                if nk_next:
                    sn = scores(pr, c, lhs)
                alpha, p = softmax_block(pr, c, nk_cur)
                value_block(pr, c, i, nk_cur, alpha, p)
                if nk_next:
                    if masked_next:
                        sn = jnp.where(causal_mask(nk_next), sn, MASK_VALUE)
                    stage_scores(pr, c, sn)

    keep = jnp.logical_or(causal_mask(tk), n_full > 0)
    for pr in range(2):
        lhs0 = key_rows(pr, 0, tk)
        for c in range(4):
            stage_scores(pr, c, jnp.where(keep, scores(pr, c, lhs0), MASK_VALUE))

    def body(i, carry):
        trip(i)
        return carry

    lax.fori_loop(0, n_full - 1, body, 0)

    short = jnp.logical_and(qi % 2 == 0, n_full > 0)

    def tail(nk):
        @pl.when(n_full > 0)
        def _():
            trip(n_full - 1, nk_next=nk, masked_next=True)

        trip(n_full, nk_cur=nk, nk_next=0)

    @pl.when(short)
    def _():
        tail(t)

    @pl.when(jnp.logical_not(short))
    def _():
        tail(tk)

    lam = (jnp.exp(jnp.sum(lq1_ref[...] * lk1_ref[...], axis=-1, keepdims=True))
           - jnp.exp(jnp.sum(lq2_ref[...] * lk2_ref[...], axis=-1, keepdims=True))
           + lambda_init)
    outs = []
    for h in range(A_HEADS):
        pr, hh = divmod(h, 2)
        acc = acc_sc[h, :HEAD_DIM, :]
        l = acc_sc[h, HEAD_DIM:HEAD_DIM + 1, :]
        o = acc[:, :t] / l[:, :t] - lam * (acc[:, t:] / l[:, t:])
        ms = jnp.mean(o * o, axis=0, keepdims=True)
        outs.append(o * lax.rsqrt(ms + NORM_EPS))
    o_all = jnp.concatenate(outs, axis=0) * sg_ref[...] * (1.0 - lambda_init)
    o_ref[...] = (o_all.T * _silu(gate_ref[...])).astype(BF16)


def _attn_a(aqt, ak, kfeat, avt, gate, qfeat, lq1, lk1, lq2, lk2, sg4, lambda_init, b, s):
    t = A_TILE
    tk = A_KEYS
    nt = s // t
    small = lambda shape: pl.BlockSpec(shape, lambda bi, qi: (0, 0))
    return pl.pallas_call(
        functools.partial(_attn_a_kernel, lambda_init=lambda_init),
        grid=(b, nt),
        in_specs=[
            small((1, A_QK_DIM)), small((1, A_QK_DIM)), small((1, A_QK_DIM)), small((1, A_QK_DIM)),
            small((A_W, 1)),
            small((LANES, A_HEADS)),
            pl.BlockSpec((None, A_W, t), lambda bi, qi: (bi * nt + qi, 0, 0)),
            pl.BlockSpec((s, A_W), lambda bi, qi: (bi, 0)),
            pl.BlockSpec((s, LANES), lambda bi, qi: (0, 0)),
            pl.BlockSpec((nt, A_W, t), lambda bi, qi: (bi, 0, 0)),
            pl.BlockSpec((t, A_W), lambda bi, qi: (bi * nt + qi, GATE_BLOCK_A)),
        ],
        out_specs=pl.BlockSpec((t, A_W), lambda bi, qi: (bi * nt + qi, 0)),
        out_shape=jax.ShapeDtypeStruct((b * s, A_W), BF16),
        scratch_shapes=[
            pltpu.VMEM((2, 2 * LANES, 4 * t), BF16),
            pltpu.VMEM((2, 1, 4 * t), F32),
            pltpu.VMEM((A_HEADS, HEAD_DIM + A_NORM_ROWS, 2 * t), F32),
            pltpu.VMEM((2, tk, 4 * t), F32),
            pltpu.VMEM((2, 1, 4 * t), F32),
        ],
        compiler_params=pltpu.CompilerParams(
            dimension_semantics=("parallel", "arbitrary"),
            vmem_limit_bytes=VMEM_LIMIT_BYTES),
        name="attn_a",
    )(lq1, lk1, lq2, lk2, sg4, qfeat, aqt, ak, kfeat, avt, gate)


def _attn_b_kernel(q_ref, kp_ref, kc_ref, vtp_ref, vtc_ref, gate_ref, bias_ref, sink_ref, o_ref):
    t = B_TILE
    qi = pl.program_id(1)
    lane = lax.broadcasted_iota(jnp.int32, (t, LANES), 1)
    sink = sink_ref[...]
    k_blocks = [kp_ref[...]] + [kc_ref[c * t:(c + 1) * t, :] for c in range(B_BLOCKS)]
    vt_blocks = [vtp_ref[...]] + [vtc_ref[c] for c in range(B_BLOCKS)]
    for c in range(B_BLOCKS):
        rows = slice(c * t, (c + 1) * t)
        k_all = jnp.concatenate(k_blocks[c:c + 2], axis=0)
        vt_all = jnp.concatenate(vt_blocks[c:c + 2], axis=1)
        pieces = [None] * B_HEADS
        for j in range(B_GROUP):
            qp = q_ref[rows, j * LANES:(j + 1) * LANES]
            zero = jnp.zeros_like(qp)
            pieces[j] = jnp.where(lane < HEAD_DIM, qp, zero)
            pieces[B_GROUP + j] = jnp.where(lane >= HEAD_DIM, qp, zero)
        qz = jnp.concatenate(pieces, axis=0)
        bias = bias_ref[jnp.minimum(qi, 1)] if c == 0 else bias_ref[1]
        s = _dot_nt(k_all, qz) + bias
        m = jnp.maximum(jnp.max(s, axis=0, keepdims=True), sink)
        p = jnp.exp2(s - m)
        inv = 1.0 / (jnp.sum(p, axis=0, keepdims=True) + jnp.exp2(sink - m))
        pb = p.astype(BF16)
        outs = []
        for g in range(B_KV_HEADS):
            cols = slice(g * B_GROUP * t, (g + 1) * B_GROUP * t)
            og = jnp.dot(vt_all[g * HEAD_DIM:(g + 1) * HEAD_DIM, :], pb[:, cols],
                         preferred_element_type=F32) * inv[:, cols]
            outs.extend(og[:, j * t:(j + 1) * t] for j in range(B_GROUP))
        o_all = jnp.concatenate(outs, axis=0)
        o_ref[rows, :] = (o_all.T * _silu(gate_ref[rows, :])).astype(BF16)


def _attn_b(bq, bk, bvt, gate, bias, sink_row, b, s):
    t = B_TILE
    tq = B_BLOCKS * t
    nt = s // tq
    cur = lambda bi, qi: (bi * nt + qi, 0)
    prev = lambda bi, qi: (jnp.maximum((bi * nt + qi) * B_BLOCKS - 1, 0), 0)
    cur3 = lambda bi, qi: (bi * nt + qi, 0, 0)
    prev3 = lambda bi, qi: (jnp.maximum((bi * nt + qi) * B_BLOCKS - 1, 0), 0, 0)
    return pl.pallas_call(
        _attn_b_kernel,
        grid=(b, nt),
        in_specs=[
            pl.BlockSpec((tq, B_W), cur),
            pl.BlockSpec((t, B_KV_W), prev),
            pl.BlockSpec((tq, B_KV_W), cur),
            pl.BlockSpec((None, B_KV_W, t), prev3),
            pl.BlockSpec((B_BLOCKS, B_KV_W, t), cur3),
            pl.BlockSpec((tq, B_W), cur),
            pl.BlockSpec((2, 2 * t, B_HEADS * t), lambda bi, qi: (0, 0, 0)),
            pl.BlockSpec((1, B_HEADS * t), lambda bi, qi: (0, 0)),
        ],
        out_specs=pl.BlockSpec((tq, B_W), cur),
        out_shape=jax.ShapeDtypeStruct((b * s, B_W), BF16),
        compiler_params=pltpu.CompilerParams(
            dimension_semantics=("parallel", "arbitrary"),
            vmem_limit_bytes=VMEM_LIMIT_BYTES),
        name="attn_b",
    )(bq, bk, bk, bvt, bvt, gate, bias, sink_row)


def _b_bias(slopes):
    t = B_TILE
    koff = np.arange(2 * t)[:, None] - t
    qoff = np.arange(t)[None, :]
    dist = qoff - koff
    band = (dist >= 0) & (dist < WINDOW)
    out = np.empty((2, 2 * t, B_HEADS * t), np.float32)
    for h in range(B_HEADS):
        term = (-np.float32(slopes[h]) * dist.astype(np.float32) * np.float32(LOG2_E)).astype(np.float32)
        out[1, :, h * t:(h + 1) * t] = np.where(band, term, np.float32(MASK_VALUE))
        out[0, :, h * t:(h + 1) * t] = np.where(band & (koff >= 0), term, np.float32(MASK_VALUE))
    return jnp.asarray(out)


def _attn_c_kernel(q_ref, k_ref, vt_ref, u_ref, gate_ref, o_ref, qz_sc, carry_sc, acc_sc):
    t = C_TILE
    step = pl.program_id(1)
    lane = lax.broadcasted_iota(jnp.int32, (t, C_W), 1)
    for sub in range(C_SUBTILES):
        q = q_ref[sub * t:(sub + 1) * t, :]
        for h in range(C_HEADS):
            own = (lane >= h * HEAD_DIM) & (lane < (h + 1) * HEAD_DIM)
            qz_sc[sub, h * t:(h + 1) * t, :] = jnp.where(own, q, jnp.zeros_like(q))
    u2 = u_ref[...]

    def tile(sub, kj, masked, carry_in):
        ks = pl.ds(pl.multiple_of(kj * t, t), t)
        z = _dot_nt(k_ref[ks, :], qz_sc[sub])
        lse = jnp.log2(1.0 + jnp.exp2(jnp.minimum(z, -z)))
        log_beta = jnp.minimum(z, 0.0) - lse
        l1m = log_beta - z
        if masked:
            krow = lax.broadcasted_iota(jnp.int32, (t, C_HEADS * t), 0)
            qcol = lax.broadcasted_iota(jnp.int32, (t, C_HEADS * t), 1) & (t - 1)
            past = krow < qcol
            l1m = jnp.where(past, l1m, 0.0)
        hi = l1m.astype(BF16)
        lo = (l1m - hi.astype(F32)).astype(BF16)
        tail = jnp.dot(u2, jnp.concatenate([hi, lo], axis=0),
                       preferred_element_type=F32)
        a = jnp.exp2(log_beta + tail + carry_in)
        if masked:
            a = jnp.where(past, a, 0.0)
        a = a.astype(BF16)
        parts = [jnp.dot(vt_ref[kj, h * HEAD_DIM:(h + 1) * HEAD_DIM, :], a[:, h * t:(h + 1) * t],
                         preferred_element_type=F32) for h in range(C_HEADS)]
        return jnp.concatenate(parts, axis=0), carry_in + jnp.sum(l1m, axis=0, keepdims=True)

    tiles = [C_SUBTILES * step + sub for sub in range(C_SUBTILES)]
    alive = []
    for sub in range(C_SUBTILES):
        acc, carry = tile(sub, tiles[sub], True, jnp.zeros((1, C_HEADS * t), F32))
        acc_sc[sub] = acc
        carry_sc[sub] = carry
        alive.append(jnp.logical_and(tiles[sub] > 0, jnp.max(carry) > C_ZERO_WEIGHT_LOG2))

    def cond(state):
        return functools.reduce(jnp.logical_or, state[1:])

    def body(state):
        r = state[0]
        nxt = []
        for sub in range(C_SUBTILES):
            has_keys = tiles[sub] - 1 - r >= 0
            acc, carry = tile(sub, jnp.maximum(tiles[sub] - 1 - r, 0), False, carry_sc[sub])
            acc_sc[sub] += jnp.where(has_keys, acc, 0.0)
            carry_sc[sub] = jnp.where(has_keys, carry, carry_sc[sub])
            more_keys = tiles[sub] - 2 - r >= 0
            nxt.append(jnp.logical_and(more_keys, jnp.max(carry) > C_ZERO_WEIGHT_LOG2))
        return (r + 1, *nxt)

    lax.while_loop(cond, body, (jnp.int32(0), *alive))
    for sub in range(C_SUBTILES):
        rows = slice(sub * t, (sub + 1) * t)
        o_ref[rows, :] = (acc_sc[sub].T * _silu(gate_ref[rows, :])).astype(BF16)


def _attn_c(cq, ck, cvt, u2, gate, b, s):
    t = C_TILE
    tq = C_SUBTILES * t
    nt = s // t
    nq = s // tq
    return pl.pallas_call(
        _attn_c_kernel,
        grid=(b, nq),
        in_specs=[
            pl.BlockSpec((tq, C_W), lambda bi, qi: (bi * nq + qi, 0)),
            pl.BlockSpec((s, C_W), lambda bi, qi: (bi, 0)),
            pl.BlockSpec((nt, C_W, t), lambda bi, qi: (bi, 0, 0)),
            pl.BlockSpec((t, 2 * t), lambda bi, qi: (0, 0)),
            pl.BlockSpec((tq, C_W), lambda bi, qi: (bi * nq + qi, GATE_BLOCK_C)),
        ],
        out_specs=pl.BlockSpec((tq, C_W), lambda bi, qi: (bi * nq + qi, 0)),
        out_shape=jax.ShapeDtypeStruct((b * s, C_W), BF16),
        scratch_shapes=[
            pltpu.VMEM((C_SUBTILES, C_HEADS * t, C_W), BF16),
            pltpu.VMEM((C_SUBTILES, 1, C_HEADS * t), F32),
            pltpu.VMEM((C_SUBTILES, C_W, t), F32),
        ],
        compiler_params=pltpu.CompilerParams(
            dimension_semantics=("parallel", "arbitrary"),
            vmem_limit_bytes=VMEM_LIMIT_BYTES),
        name="attn_c",
    )(cq, ck, cvt, u2, gate)


def _outproj_kernel(ya_ref, yb_ref, yc_ref, x_ref, w_ref, fg_ref, o_ref, *, final_norm):
    y = jnp.concatenate([ya_ref[...], yb_ref[...], yc_ref[...]], axis=1)
    r = x_ref[...] + jnp.dot(y, w_ref[...], preferred_element_type=F32)
    if final_norm:
        ms = jnp.mean(r * r, axis=-1, keepdims=True)
        r = r * lax.rsqrt(ms + NORM_EPS) * fg_ref[...]
    o_ref[...] = r


def _outproj(ya, yb, yc, x2, w, fg, layer, final_norm):
    m, d = x2.shape
    row = lambda i: (i, 0)
    fixed = lambda i: (0, 0)
    return pl.pallas_call(
        functools.partial(_outproj_kernel, final_norm=final_norm),
        grid=(m // OUT_ROW_TILE,),
        in_specs=[
            pl.BlockSpec((OUT_ROW_TILE, A_W), row),
            pl.BlockSpec((OUT_ROW_TILE, B_W), row),
            pl.BlockSpec((OUT_ROW_TILE, C_W), row),
            pl.BlockSpec((OUT_ROW_TILE, d), row),
            pl.BlockSpec((None, GATE_W, d), lambda i: (layer, 0, 0)),
            pl.BlockSpec((1, d), fixed),
        ],
        out_specs=pl.BlockSpec((OUT_ROW_TILE, d), row),
        out_shape=jax.ShapeDtypeStruct((m, d), F32),
        compiler_params=pltpu.CompilerParams(
            dimension_semantics=("parallel",), vmem_limit_bytes=VMEM_LIMIT_BYTES),
        name="outproj",
    )(ya, yb, yc, x2, w, fg)


def _split_w_in(w):
    splits = (A_W, A_W, A_W, A_W, B_W, B_KV_W, B_KV_W, B_W, C_W, C_W, C_W, C_W)
    pts = np.cumsum(splits)[:-1]
    aq, ak, av, ag, bq, bk, bv, bg, cq, ck, cv, cg = jnp.split(w, pts, axis=-1)
    depth, d = w.shape[:2]
    bq = bq.reshape(depth, d, B_KV_HEADS, B_GROUP, HEAD_DIM).transpose(0, 1, 3, 2, 4)
    bq = bq.reshape(depth, d, B_W)
    wrm = jnp.concatenate([ak, bq, bk, cq, ck, bg, ag, cg], axis=-1)
    wt = jnp.swapaxes(jnp.concatenate([av, bv, cv, aq], axis=-1), 1, 2)
    return wrm.astype(BF16), wt.astype(BF16)


def _q_scales():
    cs = np.ones((1, OFF_GATE), np.float32)
    cs[0, OFF_BQ:OFF_BK] = HEAD_DIM ** -0.5 * LOG2_E
    cs[0, OFF_CQ:OFF_CK] = HEAD_DIM ** -0.5 * LOG2_E
    ts = np.ones((T_W, 1), np.float32)
    ts[TOFF_AQ:, 0] = A_QK_DIM ** -0.5 * LOG2_E
    return jnp.asarray(cs), jnp.asarray(ts)


def kernel(x, norm_g, w_in, lambda_q1, lambda_k1, lambda_q2, lambda_k2, subln_g, sinks, w_out, final_g):
    b, s, d = x.shape
    depth = w_in.shape[0]
    m = b * s
    slopes_a, slopes_b = _alibi_slopes()

    pos = jnp.arange(s, dtype=jnp.int32)
    p_hi = (pos // 256).astype(BF16)
    p_lo = (pos % 256).astype(BF16)
    kfeat = jnp.stack([p_hi] * 3 + [p_lo] * 3, axis=-1)
    kfeat = jnp.pad(kfeat, ((0, 0), (0, LANES - 6)))
    qfeat_np = np.zeros((LANES, A_HEADS), np.float32)
    for h, sl in enumerate(slopes_a):
        sl2 = np.float32(np.float32(sl) * np.float32(LOG2_E))
        qfeat_np[:6, h] = _split3_bf16(np.float32(256.0) * sl2) + _split3_bf16(sl2)
    qfeat = jnp.asarray(qfeat_np)

    tri = (lax.broadcasted_iota(jnp.int32, (C_TILE, C_TILE), 1)
           > lax.broadcasted_iota(jnp.int32, (C_TILE, C_TILE), 0)).astype(BF16)
    tri2 = jnp.concatenate([tri, tri], axis=1)
    b_bias = _b_bias(slopes_b)
    cs, ts = _q_scales()

    x2 = x.reshape(m, d)
    wrm_all, wt_all = _split_w_in(w_in)
    w_out_all = w_out.astype(BF16)
    sg_all = jnp.tile(subln_g, (1, A_HEADS)).reshape(depth, -1, 1)
    sink_rows = jnp.repeat(sinks * LOG2_E, B_TILE, axis=1).reshape(depth, 1, -1)
    for l in range(depth):
        lambda_init = 0.8 - 0.6 * math.exp(-0.3 * l)
        ak, bq, bk, cq, ck, gate, avt, bvt, cvt, aqt = _inproj(
            x2, norm_g[l].reshape(1, d), wrm_all, wt_all, cs, ts, l)
        ya = _attn_a(aqt, ak, kfeat, avt, gate, qfeat,
                     lambda_q1[l].reshape(1, -1), lambda_k1[l].reshape(1, -1),
                     lambda_q2[l].reshape(1, -1), lambda_k2[l].reshape(1, -1),
                     sg_all[l], lambda_init, b, s)
        yb = _attn_b(bq, bk, bvt, gate, b_bias, sink_rows[l], b, s)
        yc = _attn_c(cq, ck, cvt, tri2, gate, b, s)
        x2 = _outproj(ya, yb, yc, x2, w_out_all, final_g.reshape(1, d), l,
                      final_norm=(l == depth - 1))
    return x2.reshape(b, s, d)
```

```python
import functools
import math

import numpy as np
import jax
import jax.numpy as jnp
from jax import lax
from jax.experimental import pallas as pl
from jax.experimental.pallas import tpu as pltpu

F32 = jnp.float32
BF16 = jnp.bfloat16

HEAD_DIM = 64
NORM_EPS = 1e-5
A_HEADS = 4
A_QK_DIM = HEAD_DIM // 2
B_HEADS = 8
B_KV_HEADS = 2
B_GROUP = B_HEADS // B_KV_HEADS
C_HEADS = 4
WINDOW = 128
A_W = A_HEADS * HEAD_DIM
B_W = B_HEADS * HEAD_DIM
B_KV_W = B_KV_HEADS * HEAD_DIM
C_W = C_HEADS * HEAD_DIM
GATE_W = A_W + B_W + C_W
LANES = 128
MASK_VALUE = -1e30
LOG2_E = math.log2(math.e)
C_ZERO_WEIGHT_LOG2 = -105.0 * LOG2_E
VMEM_LIMIT_BYTES = 48 * 1024 * 1024

ROW_TILE = 512
OUT_ROW_TILE = 1024
A_TILE = 256
A_KEYS = 2 * A_TILE
A_NORM_ROWS = 16
C_TILE = 256
C_SUBTILES = 2
B_TILE = WINDOW
B_BLOCKS = 8

OFF_AK = 0
OFF_BQ = OFF_AK + A_W
OFF_BK = OFF_BQ + B_W
OFF_CQ = OFF_BK + B_KV_W
OFF_CK = OFF_CQ + C_W
OFF_GATE = OFF_CK + C_W
RM_W = OFF_GATE + GATE_W
TOFF_AV = 0
TOFF_BV = TOFF_AV + A_W
TOFF_CV = TOFF_BV + B_KV_W
TOFF_AQ = TOFF_CV + C_W
T_W = TOFF_AQ + A_W
GATE_BLOCK_A = B_W // A_W
GATE_BLOCK_C = (B_W + A_W) // C_W


def _alibi_slopes():
    n = A_HEADS + B_HEADS
    s = (2.0 ** (-8.0 * np.arange(1, n + 1) / n)).astype(np.float32)
    return s[B_HEADS:], s[:B_HEADS]


def _split3_bf16(v):
    out = []
    r = np.float32(v)
    for _ in range(3):
        p = np.asarray(r, np.float32).astype(BF16).astype(np.float32)
        out.append(float(p))
        r = np.float32(r - p)
    return out


def _dot_nt(a, b):
    return lax.dot_general(a, b, (((1,), (1,)), ((), ())), preferred_element_type=F32)


def _silu(g):
    return g * jax.nn.sigmoid(g)


def _inproj_kernel(x_ref, g_ref, wrm_ref, wt_ref, cs_ref, ts_ref,
                   ak_ref, bq_ref, bk_ref, cq_ref, ck_ref, gate_ref,
                   avt_ref, bvt_ref, cvt_ref, aqt_ref):
    x = x_ref[...]
    ms = jnp.mean(x * x, axis=-1, keepdims=True)
    h = (x * lax.rsqrt(ms + NORM_EPS) * g_ref[...]).astype(BF16)
    qk = jnp.dot(h, wrm_ref[:, :OFF_GATE], preferred_element_type=F32) * cs_ref[...]
    ak_ref[...] = qk[:, OFF_AK:OFF_BQ].astype(BF16)
    bq_ref[...] = qk[:, OFF_BQ:OFF_BK].astype(BF16)
    bk_ref[...] = qk[:, OFF_BK:OFF_CQ].astype(BF16)
    cq_ref[...] = qk[:, OFF_CQ:OFF_CK].astype(BF16)
    ck_ref[...] = qk[:, OFF_CK:OFF_GATE].astype(BF16)
    gate_ref[...] = jnp.dot(h, wrm_ref[:, OFF_GATE:], preferred_element_type=F32)
    tr = (_dot_nt(wt_ref[...], h) * ts_ref[...]).astype(BF16)
    for c in range(ROW_TILE // A_TILE):
        avt_ref[c] = tr[TOFF_AV:TOFF_BV, c * A_TILE:(c + 1) * A_TILE]
        aqt_ref[c] = tr[TOFF_AQ:, c * A_TILE:(c + 1) * A_TILE]
    for c in range(ROW_TILE // B_TILE):
        bvt_ref[c] = tr[TOFF_BV:TOFF_CV, c * B_TILE:(c + 1) * B_TILE]
    for c in range(ROW_TILE // C_TILE):
        cvt_ref[c] = tr[TOFF_CV:TOFF_AQ, c * C_TILE:(c + 1) * C_TILE]


def _inproj(x2, g, wrm, wt, cs, ts, layer):
    m, d = x2.shape
    tm = ROW_TILE
    row = lambda i: (i, 0)
    fixed = lambda i: (0, 0)
    tiles = lambda i: (i, 0, 0)
    this_layer = lambda i: (layer, 0, 0)
    widths = (A_W, B_W, B_KV_W, C_W, C_W)
    return pl.pallas_call(
        _inproj_kernel,
        grid=(m // tm,),
        in_specs=[
            pl.BlockSpec((tm, d), row),
            pl.BlockSpec((1, d), fixed),
            pl.BlockSpec((None, d, RM_W), this_layer),
            pl.BlockSpec((None, T_W, d), this_layer),
            pl.BlockSpec((1, OFF_GATE), fixed),
            pl.BlockSpec((T_W, 1), fixed),
        ],
        out_specs=[pl.BlockSpec((tm, w), row) for w in widths] + [
            pl.BlockSpec((tm, GATE_W), row),
            pl.BlockSpec((tm // A_TILE, A_W, A_TILE), tiles),
            pl.BlockSpec((tm // B_TILE, B_KV_W, B_TILE), tiles),
            pl.BlockSpec((tm // C_TILE, C_W, C_TILE), tiles),
            pl.BlockSpec((tm // A_TILE, A_W, A_TILE), tiles),
        ],
        out_shape=[jax.ShapeDtypeStruct((m, w), BF16) for w in widths] + [
            jax.ShapeDtypeStruct((m, GATE_W), F32),
            jax.ShapeDtypeStruct((m // A_TILE, A_W, A_TILE), BF16),
            jax.ShapeDtypeStruct((m // B_TILE, B_KV_W, B_TILE), BF16),
            jax.ShapeDtypeStruct((m // C_TILE, C_W, C_TILE), BF16),
            jax.ShapeDtypeStruct((m // A_TILE, A_W, A_TILE), BF16),
        ],
        compiler_params=pltpu.CompilerParams(
            dimension_semantics=("parallel",), vmem_limit_bytes=VMEM_LIMIT_BYTES),
        name="inproj",
    )(x2, g, wrm, wt, cs, ts)


def _attn_a_kernel(lq1_ref, lk1_ref, lq2_ref, lk2_ref, sg_ref, qfeat_ref,
                   qt_ref, k_ref, kfeat_ref, vt_ref, gate_ref, o_ref,
                   qz_sc, m_sc, acc_sc, s_sc, smax_sc, *, lambda_init):
    t = A_TILE
    tk = A_KEYS
    qi = pl.program_id(1)
    n_full = qi // 2
    m_sc[...] = jnp.full(m_sc.shape, MASK_VALUE, F32)
    acc_sc[...] = jnp.zeros(acc_sc.shape, F32)
    ones_rows = jnp.ones((A_NORM_ROWS, tk), BF16)

    sub = lax.broadcasted_iota(jnp.int32, (LANES, t), 0)
    for pr in range(2):
        qp = qt_ref[pr * LANES:(pr + 1) * LANES, :]
        for c in range(4):
            hh, mp = divmod(c, 2)
            lo = hh * HEAD_DIM + mp * A_QK_DIM
            real = jnp.where((sub >= lo) & (sub < lo + A_QK_DIM), qp, jnp.zeros_like(qp))
            h = 2 * pr + hh
            feat = jnp.broadcast_to(qfeat_ref[:, h:h + 1], (LANES, t)).astype(BF16)
            qz_sc[pr, c] = jnp.concatenate([real, feat], axis=0)

    def key_rows(pr, step, nk):
        ks = pl.ds(pl.multiple_of(step * tk, tk), nk)
        return jnp.concatenate([k_ref[ks, pr * LANES:(pr + 1) * LANES], kfeat_ref[ks, :]], axis=1)

    def scores(pr, c, lhs):
        return jnp.dot(lhs, qz_sc[pr, c], preferred_element_type=F32)

    def softmax_block(pr, c, nk):
        m_old = m_sc[pr, c]
        m_new = jnp.maximum(m_old, smax_sc[pr, c])
        m_sc[pr, c] = m_new
        return jnp.exp2(m_old - m_new), jnp.exp2((s_sc[pr, c, :nk, :] - m_new).astype(BF16))

    def value_block(pr, c, step, nk, alpha, p):
        hh, mp = divmod(c, 2)
        h = 2 * pr + hh
        rows = slice(h * HEAD_DIM, (h + 1) * HEAD_DIM)
        vt = jnp.concatenate([vt_ref[2 * step + blk, rows, :] for blk in range(nk // t)], axis=1)
        vt1 = jnp.concatenate([vt, ones_rows[:, :nk]], axis=0)
        pv = jnp.dot(vt1, p, preferred_element_type=F32)
        acc_sc[h, mp] = alpha * acc_sc[h, mp] + pv

    def causal_mask(nk):
        kmq = (lax.broadcasted_iota(jnp.int32, (nk, t), 0)
               - lax.broadcasted_iota(jnp.int32, (nk, t), 1))
        return kmq <= qi * t - n_full * tk

    def stage_scores(pr, c, s):
        s_sc[pr, c, :s.shape[0], :] = s
        smax_sc[pr, c] = jnp.max(s, axis=0, keepdims=True)

    def trip(i, nk_cur=tk, nk_next=tk, masked_next=False):
        for pr in range(2):
            lhs = key_rows(pr, i + 1, nk_next) if nk_next else None
            for c in range(4):
                if nk_next:
                    sn = scores(pr, c, lhs)
                alpha, p = softmax_block(pr, c, nk_cur)
                value_block(pr, c, i, nk_cur, alpha, p)
                if nk_next:
                    if masked_next:
                        sn = jnp.where(causal_mask(nk_next), sn, MASK_VALUE)
                    stage_scores(pr, c, sn)

    keep = jnp.logical_or(causal_mask(tk), n_full > 0)
    for pr in range(2):
        lhs0 = key_rows(pr, 0, tk)
        for c in range(4):
            stage_scores(pr, c, jnp.where(keep, scores(pr, c, lhs0), MASK_VALUE))

    def body(i, carry):
        trip(i)
        return carry

    lax.fori_loop(0, n_full - 1, body, 0)

    short = jnp.logical_and(qi % 2 == 0, n_full > 0)

    def tail(nk):
        @pl.when(n_full > 0)
        def _():
            trip(n_full - 1, nk_next=nk, masked_next=True)

        trip(n_full, nk_cur=nk, nk_next=0)

    @pl.when(short)
    def _():
        tail(t)

    @pl.when(jnp.logical_not(short))
    def _():
        tail(tk)

    lam = (jnp.exp(jnp.sum(lq1_ref[...] * lk1_ref[...], axis=-1, keepdims=True))
           - jnp.exp(jnp.sum(lq2_ref[...] * lk2_ref[...], axis=-1, keepdims=True))
           + lambda_init)
    outs = []
    for h in range(A_HEADS):
        o0, o1 = (acc_sc[h, mp, :HEAD_DIM, :] / acc_sc[h, mp, HEAD_DIM:HEAD_DIM + 1, :]
                  for mp in range(2))
        o = o0 - lam * o1
        ms = jnp.mean(o * o, axis=0, keepdims=True)
        outs.append(o * lax.rsqrt(ms + NORM_EPS))
    o_all = jnp.concatenate(outs, axis=0) * sg_ref[...] * (1.0 - lambda_init)
    o_ref[...] = (o_all.T * _silu(gate_ref[...])).astype(BF16)


def _attn_a(aqt, ak, kfeat, avt, gate, qfeat, lq1, lk1, lq2, lk2, sg4, lambda_init, b, s):
    t = A_TILE
    tk = A_KEYS
    nt = s // t
    small = lambda shape: pl.BlockSpec(shape, lambda bi, qi: (0, 0))
    return pl.pallas_call(
        functools.partial(_attn_a_kernel, lambda_init=lambda_init),
        grid=(b, nt),
        in_specs=[
            small((1, A_QK_DIM)), small((1, A_QK_DIM)), small((1, A_QK_DIM)), small((1, A_QK_DIM)),
            small((A_W, 1)),
            small((LANES, A_HEADS)),
            pl.BlockSpec((None, A_W, t), lambda bi, qi: (bi * nt + qi, 0, 0)),
            pl.BlockSpec((s, A_W), lambda bi, qi: (bi, 0)),
            pl.BlockSpec((s, LANES), lambda bi, qi: (0, 0)),
            pl.BlockSpec((nt, A_W, t), lambda bi, qi: (bi, 0, 0)),
            pl.BlockSpec((t, A_W), lambda bi, qi: (bi * nt + qi, GATE_BLOCK_A)),
        ],
        out_specs=pl.BlockSpec((t, A_W), lambda bi, qi: (bi * nt + qi, 0)),
        out_shape=jax.ShapeDtypeStruct((b * s, A_W), BF16),
        scratch_shapes=[
            pltpu.VMEM((2, 4, 2 * LANES, t), BF16),
            pltpu.VMEM((2, 4, 1, t), F32),
            pltpu.VMEM((A_HEADS, 2, HEAD_DIM + A_NORM_ROWS, t), F32),
            pltpu.VMEM((2, 4, tk, t), F32),
            pltpu.VMEM((2, 4, 1, t), F32),
        ],
        compiler_params=pltpu.CompilerParams(
            dimension_semantics=("parallel", "arbitrary"),
            vmem_limit_bytes=VMEM_LIMIT_BYTES),
        name="attn_a",
    )(lq1, lk1, lq2, lk2, sg4, qfeat, aqt, ak, kfeat, avt, gate)


def _attn_b_kernel(q_ref, kp_ref, kc_ref, vtp_ref, vtc_ref, gate_ref, bias_ref, sink_ref, o_ref):
    t = B_TILE
    qi = pl.program_id(1)
    lane = lax.broadcasted_iota(jnp.int32, (t, LANES), 1)
    sink = sink_ref[...]
    k_blocks = [kp_ref[...]] + [kc_ref[c * t:(c + 1) * t, :] for c in range(B_BLOCKS)]
    vt_blocks = [vtp_ref[...]] + [vtc_ref[c] for c in range(B_BLOCKS)]
    for c in range(B_BLOCKS):
        rows = slice(c * t, (c + 1) * t)
        k_all = jnp.concatenate(k_blocks[c:c + 2], axis=0)
        vt_all = jnp.concatenate(vt_blocks[c:c + 2], axis=1)
        pieces = [None] * B_HEADS
        for j in range(B_GROUP):
            qp = q_ref[rows, j * LANES:(j + 1) * LANES]
            zero = jnp.zeros_like(qp)
            pieces[j] = jnp.where(lane < HEAD_DIM, qp, zero)
            pieces[B_GROUP + j] = jnp.where(lane >= HEAD_DIM, qp, zero)
        qz = jnp.concatenate(pieces, axis=0)
        bias = bias_ref[jnp.minimum(qi, 1)] if c == 0 else bias_ref[1]
        s = _dot_nt(k_all, qz) + bias
        m = jnp.maximum(jnp.max(s, axis=0, keepdims=True), sink)
        p = jnp.exp2(s - m)
        inv = 1.0 / (jnp.sum(p, axis=0, keepdims=True) + jnp.exp2(sink - m))
        pb = p.astype(BF16)
        outs = []
        for g in range(B_KV_HEADS):
            cols = slice(g * B_GROUP * t, (g + 1) * B_GROUP * t)
            og = jnp.dot(vt_all[g * HEAD_DIM:(g + 1) * HEAD_DIM, :], pb[:, cols],
                         preferred_element_type=F32) * inv[:, cols]
            outs.extend(og[:, j * t:(j + 1) * t] for j in range(B_GROUP))
        o_all = jnp.concatenate(outs, axis=0)
        o_ref[rows, :] = (o_all.T * _silu(gate_ref[rows, :])).astype(BF16)


def _attn_b(bq, bk, bvt, gate, bias, sink_row, b, s):
    t = B_TILE
    tq = B_BLOCKS * t
    nt = s // tq
    cur = lambda bi, qi: (bi * nt + qi, 0)
    prev = lambda bi, qi: (jnp.maximum((bi * nt + qi) * B_BLOCKS - 1, 0), 0)
    cur3 = lambda bi, qi: (bi * nt + qi, 0, 0)
    prev3 = lambda bi, qi: (jnp.maximum((bi * nt + qi) * B_BLOCKS - 1, 0), 0, 0)
    return pl.pallas_call(
        _attn_b_kernel,
        grid=(b, nt),
        in_specs=[
            pl.BlockSpec((tq, B_W), cur),
            pl.BlockSpec((t, B_KV_W), prev),
            pl.BlockSpec((tq, B_KV_W), cur),
            pl.BlockSpec((None, B_KV_W, t), prev3),
            pl.BlockSpec((B_BLOCKS, B_KV_W, t), cur3),
            pl.BlockSpec((tq, B_W), cur),
            pl.BlockSpec((2, 2 * t, B_HEADS * t), lambda bi, qi: (0, 0, 0)),
            pl.BlockSpec((1, B_HEADS * t), lambda bi, qi: (0, 0)),
        ],
        out_specs=pl.BlockSpec((tq, B_W), cur),
        out_shape=jax.ShapeDtypeStruct((b * s, B_W), BF16),
        compiler_params=pltpu.CompilerParams(
            dimension_semantics=("parallel", "arbitrary"),
            vmem_limit_bytes=VMEM_LIMIT_BYTES),
        name="attn_b",
    )(bq, bk, bk, bvt, bvt, gate, bias, sink_row)


def _b_bias(slopes):
    t = B_TILE
    koff = np.arange(2 * t)[:, None] - t
    qoff = np.arange(t)[None, :]
    dist = qoff - koff
    band = (dist >= 0) & (dist < WINDOW)
    out = np.empty((2, 2 * t, B_HEADS * t), np.float32)
    for h in range(B_HEADS):
        term = (-np.float32(slopes[h]) * dist.astype(np.float32) * np.float32(LOG2_E)).astype(np.float32)
        out[1, :, h * t:(h + 1) * t] = np.where(band, term, np.float32(MASK_VALUE))
        out[0, :, h * t:(h + 1) * t] = np.where(band & (koff >= 0), term, np.float32(MASK_VALUE))
    return jnp.asarray(out)


def _attn_c_kernel(q_ref, k_ref, vt_ref, u_ref, gate_ref, o_ref, qz_sc, carry_sc, acc_sc):
    t = C_TILE
    step = pl.program_id(1)
    lane = lax.broadcasted_iota(jnp.int32, (t, C_W), 1)
    for sub in range(C_SUBTILES):
        q = q_ref[sub * t:(sub + 1) * t, :]
        for h in range(C_HEADS):
            own = (lane >= h * HEAD_DIM) & (lane < (h + 1) * HEAD_DIM)
            qz_sc[sub, h * t:(h + 1) * t, :] = jnp.where(own, q, jnp.zeros_like(q))
    u2 = u_ref[...]

    def tile(sub, kj, masked, carry_in):
        ks = pl.ds(pl.multiple_of(kj * t, t), t)
        z = _dot_nt(k_ref[ks, :], qz_sc[sub])
        lse = jnp.log2(1.0 + jnp.exp2(jnp.minimum(z, -z)))
        log_beta = jnp.minimum(z, 0.0) - lse
        l1m = log_beta - z
        if masked:
            krow = lax.broadcasted_iota(jnp.int32, (t, C_HEADS * t), 0)
            qcol = lax.broadcasted_iota(jnp.int32, (t, C_HEADS * t), 1) & (t - 1)
            past = krow < qcol
            l1m = jnp.where(past, l1m, 0.0)
        hi = l1m.astype(BF16)
        lo = (l1m - hi.astype(F32)).astype(BF16)
        tail = jnp.dot(u2, jnp.concatenate([hi, lo], axis=0),
                       preferred_element_type=F32)
        a = jnp.exp2(log_beta + tail + carry_in)
        if masked:
            a = jnp.where(past, a, 0.0)
        a = a.astype(BF16)
        parts = [jnp.dot(vt_ref[kj, h * HEAD_DIM:(h + 1) * HEAD_DIM, :], a[:, h * t:(h + 1) * t],
                         preferred_element_type=F32) for h in range(C_HEADS)]
        return jnp.concatenate(parts, axis=0), carry_in + jnp.sum(l1m, axis=0, keepdims=True)

    tiles = [C_SUBTILES * step + sub for sub in range(C_SUBTILES)]
    alive = []
    for sub in range(C_SUBTILES):
        acc, carry = tile(sub, tiles[sub], True, jnp.zeros((1, C_HEADS * t), F32))
        acc_sc[sub] = acc
        carry_sc[sub] = carry
        alive.append(jnp.logical_and(tiles[sub] > 0, jnp.max(carry) > C_ZERO_WEIGHT_LOG2))

    def cond(state):
        return functools.reduce(jnp.logical_or, state[1:])

    def body(state):
        r = state[0]
        nxt = []
        for sub in range(C_SUBTILES):
            has_keys = tiles[sub] - 1 - r >= 0
            acc, carry = tile(sub, jnp.maximum(tiles[sub] - 1 - r, 0), False, carry_sc[sub])
            acc_sc[sub] += jnp.where(has_keys, acc, 0.0)
            carry_sc[sub] = jnp.where(has_keys, carry, carry_sc[sub])
            more_keys = tiles[sub] - 2 - r >= 0
            nxt.append(jnp.logical_and(more_keys, jnp.max(carry) > C_ZERO_WEIGHT_LOG2))
        return (r + 1, *nxt)

    lax.while_loop(cond, body, (jnp.int32(0), *alive))
    for sub in range(C_SUBTILES):
        rows = slice(sub * t, (sub + 1) * t)
        o_ref[rows, :] = (acc_sc[sub].T * _silu(gate_ref[rows, :])).astype(BF16)


def _attn_c(cq, ck, cvt, u2, gate, b, s):
    t = C_TILE
    tq = C_SUBTILES * t
    nt = s // t
    nq = s // tq
    return pl.pallas_call(
        _attn_c_kernel,
        grid=(b, nq),
        in_specs=[
            pl.BlockSpec((tq, C_W), lambda bi, qi: (bi * nq + qi, 0)),
            pl.BlockSpec((s, C_W), lambda bi, qi: (bi, 0)),
            pl.BlockSpec((nt, C_W, t), lambda bi, qi: (bi, 0, 0)),
            pl.BlockSpec((t, 2 * t), lambda bi, qi: (0, 0)),
            pl.BlockSpec((tq, C_W), lambda bi, qi: (bi * nq + qi, GATE_BLOCK_C)),
        ],
        out_specs=pl.BlockSpec((tq, C_W), lambda bi, qi: (bi * nq + qi, 0)),
        out_shape=jax.ShapeDtypeStruct((b * s, C_W), BF16),
        scratch_shapes=[
            pltpu.VMEM((C_SUBTILES, C_HEADS * t, C_W), BF16),
            pltpu.VMEM((C_SUBTILES, 1, C_HEADS * t), F32),
            pltpu.VMEM((C_SUBTILES, C_W, t), F32),
        ],
        compiler_params=pltpu.CompilerParams(
            dimension_semantics=("parallel", "arbitrary"),
            vmem_limit_bytes=VMEM_LIMIT_BYTES),
        name="attn_c",
    )(cq, ck, cvt, u2, gate)


def _outproj_kernel(ya_ref, yb_ref, yc_ref, x_ref, w_ref, fg_ref, o_ref, *, final_norm):
    y = jnp.concatenate([ya_ref[...], yb_ref[...], yc_ref[...]], axis=1)
    r = x_ref[...] + jnp.dot(y, w_ref[...], preferred_element_type=F32)
    if final_norm:
        ms = jnp.mean(r * r, axis=-1, keepdims=True)
        r = r * lax.rsqrt(ms + NORM_EPS) * fg_ref[...]
    o_ref[...] = r


def _outproj(ya, yb, yc, x2, w, fg, layer, final_norm):
    m, d = x2.shape
    row = lambda i: (i, 0)
    fixed = lambda i: (0, 0)
    return pl.pallas_call(
        functools.partial(_outproj_kernel, final_norm=final_norm),
        grid=(m // OUT_ROW_TILE,),
        in_specs=[
            pl.BlockSpec((OUT_ROW_TILE, A_W), row),
            pl.BlockSpec((OUT_ROW_TILE, B_W), row),
            pl.BlockSpec((OUT_ROW_TILE, C_W), row),
            pl.BlockSpec((OUT_ROW_TILE, d), row),
            pl.BlockSpec((None, GATE_W, d), lambda i: (layer, 0, 0)),
            pl.BlockSpec((1, d), fixed),
        ],
        out_specs=pl.BlockSpec((OUT_ROW_TILE, d), row),
        out_shape=jax.ShapeDtypeStruct((m, d), F32),
        compiler_params=pltpu.CompilerParams(
            dimension_semantics=("parallel",), vmem_limit_bytes=VMEM_LIMIT_BYTES),
        name="outproj",
    )(ya, yb, yc, x2, w, fg)


def _split_w_in(w):
    splits = (A_W, A_W, A_W, A_W, B_W, B_KV_W, B_KV_W, B_W, C_W, C_W, C_W, C_W)
    pts = np.cumsum(splits)[:-1]
    aq, ak, av, ag, bq, bk, bv, bg, cq, ck, cv, cg = jnp.split(w, pts, axis=-1)
    depth, d = w.shape[:2]
    bq = bq.reshape(depth, d, B_KV_HEADS, B_GROUP, HEAD_DIM).transpose(0, 1, 3, 2, 4)
    bq = bq.reshape(depth, d, B_W)
    wrm = jnp.concatenate([ak, bq, bk, cq, ck, bg, ag, cg], axis=-1)
    wt = jnp.swapaxes(jnp.concatenate([av, bv, cv, aq], axis=-1), 1, 2)
    return wrm.astype(BF16), wt.astype(BF16)


def _q_scales():
    cs = np.ones((1, OFF_GATE), np.float32)
    cs[0, OFF_BQ:OFF_BK] = HEAD_DIM ** -0.5 * LOG2_E
    cs[0, OFF_CQ:OFF_CK] = HEAD_DIM ** -0.5 * LOG2_E
    ts = np.ones((T_W, 1), np.float32)
    ts[TOFF_AQ:, 0] = A_QK_DIM ** -0.5 * LOG2_E
    return jnp.asarray(cs), jnp.asarray(ts)


def kernel(x, norm_g, w_in, lambda_q1, lambda_k1, lambda_q2, lambda_k2, subln_g, sinks, w_out, final_g):
    b, s, d = x.shape
    depth = w_in.shape[0]
    m = b * s
    slopes_a, slopes_b = _alibi_slopes()

    pos = jnp.arange(s, dtype=jnp.int32)
    p_hi = (pos // 256).astype(BF16)
    p_lo = (pos % 256).astype(BF16)
    kfeat = jnp.stack([p_hi] * 3 + [p_lo] * 3, axis=-1)
    kfeat = jnp.pad(kfeat, ((0, 0), (0, LANES - 6)))
    qfeat_np = np.zeros((LANES, A_HEADS), np.float32)
    for h, sl in enumerate(slopes_a):
        sl2 = np.float32(np.float32(sl) * np.float32(LOG2_E))
        qfeat_np[:6, h] = _split3_bf16(np.float32(256.0) * sl2) + _split3_bf16(sl2)
    qfeat = jnp.asarray(qfeat_np)

    tri = (lax.broadcasted_iota(jnp.int32, (C_TILE, C_TILE), 1)
           > lax.broadcasted_iota(jnp.int32, (C_TILE, C_TILE), 0)).astype(BF16)
    tri2 = jnp.concatenate([tri, tri], axis=1)
    b_bias = _b_bias(slopes_b)
    cs, ts = _q_scales()

    x2 = x.reshape(m, d)
    wrm_all, wt_all = _split_w_in(w_in)
    w_out_all = w_out.astype(BF16)
    sg_all = jnp.tile(subln_g, (1, A_HEADS)).reshape(depth, -1, 1)
    sink_rows = jnp.repeat(sinks * LOG2_E, B_TILE, axis=1).reshape(depth, 1, -1)
    for l in range(depth):
        lambda_init = 0.8 - 0.6 * math.exp(-0.3 * l)
        ak, bq, bk, cq, ck, gate, avt, bvt, cvt, aqt = _inproj(
            x2, norm_g[l].reshape(1, d), wrm_all, wt_all, cs, ts, l)
        ya = _attn_a(aqt, ak, kfeat, avt, gate, qfeat,
                     lambda_q1[l].reshape(1, -1), lambda_k1[l].reshape(1, -1),
                     lambda_q2[l].reshape(1, -1), lambda_k2[l].reshape(1, -1),
                     sg_all[l], lambda_init, b, s)
        yb = _attn_b(bq, bk, bvt, gate, b_bias, sink_rows[l], b, s)
        yc = _attn_c(cq, ck, cvt, tri2, gate, b, s)
        x2 = _outproj(ya, yb, yc, x2, w_out_all, final_g.reshape(1, d), l,
                      final_norm=(l == depth - 1))
    return x2.reshape(b, s, d)
```

```python
import functools
import math

import numpy as np
import jax
import jax.numpy as jnp
from jax import lax
from jax.experimental import pallas as pl
from jax.experimental.pallas import tpu as pltpu

F32 = jnp.float32
BF16 = jnp.bfloat16

HEAD_DIM = 64
NORM_EPS = 1e-5
A_HEADS = 4
A_QK_DIM = HEAD_DIM // 2
B_HEADS = 8
B_KV_HEADS = 2
B_GROUP = B_HEADS // B_KV_HEADS
C_HEADS = 4
WINDOW = 128
A_W = A_HEADS * HEAD_DIM
B_W = B_HEADS * HEAD_DIM
B_KV_W = B_KV_HEADS * HEAD_DIM
C_W = C_HEADS * HEAD_DIM
GATE_W = A_W + B_W + C_W
LANES = 128
MASK_VALUE = -1e30
LOG2_E = math.log2(math.e)
C_ZERO_WEIGHT_LOG2 = -105.0 * LOG2_E
VMEM_LIMIT_BYTES = 48 * 1024 * 1024

ROW_TILE = 512
OUT_ROW_TILE = 1024
A_TILE = 256
A_KEYS = 2 * A_TILE
A_NORM_ROWS = 16
C_TILE = 256
C_SUBTILES = 2
B_TILE = WINDOW
B_BLOCKS = 8

OFF_AK = 0
OFF_BQ = OFF_AK + A_W
OFF_BK = OFF_BQ + B_W
OFF_CQ = OFF_BK + B_KV_W
OFF_CK = OFF_CQ + C_W
OFF_GATE = OFF_CK + C_W
RM_W = OFF_GATE + GATE_W
TOFF_AV = 0
TOFF_BV = TOFF_AV + A_W
TOFF_CV = TOFF_BV + B_KV_W
TOFF_AQ = TOFF_CV + C_W
T_W = TOFF_AQ + A_W
GATE_BLOCK_A = B_W // A_W
GATE_BLOCK_C = (B_W + A_W) // C_W


def _alibi_slopes():
    n = A_HEADS + B_HEADS
    s = (2.0 ** (-8.0 * np.arange(1, n + 1) / n)).astype(np.float32)
    return s[B_HEADS:], s[:B_HEADS]


def _split3_bf16(v):
    out = []
    r = np.float32(v)
    for _ in range(3):
        p = np.asarray(r, np.float32).astype(BF16).astype(np.float32)
        out.append(float(p))
        r = np.float32(r - p)
    return out


def _dot_nt(a, b):
    return lax.dot_general(a, b, (((1,), (1,)), ((), ())), preferred_element_type=F32)


def _silu(g):
    return g * jax.nn.sigmoid(g)


def _inproj_kernel(x_ref, g_ref, wrm_ref, wt_ref, cs_ref, ts_ref,
                   ak_ref, bq_ref, bk_ref, cq_ref, ck_ref, gate_ref,
                   avt_ref, bvt_ref, cvt_ref, aqt_ref):
    x = x_ref[...]
    ms = jnp.mean(x * x, axis=-1, keepdims=True)
    h = (x * lax.rsqrt(ms + NORM_EPS) * g_ref[...]).astype(BF16)
    qk = jnp.dot(h, wrm_ref[:, :OFF_GATE], preferred_element_type=F32) * cs_ref[...]
    ak_ref[...] = qk[:, OFF_AK:OFF_BQ].astype(BF16)
    bq_ref[...] = qk[:, OFF_BQ:OFF_BK].astype(BF16)
    bk_ref[...] = qk[:, OFF_BK:OFF_CQ].astype(BF16)
    cq_ref[...] = qk[:, OFF_CQ:OFF_CK].astype(BF16)
    ck_ref[...] = qk[:, OFF_CK:OFF_GATE].astype(BF16)
    gate_ref[...] = jnp.dot(h, wrm_ref[:, OFF_GATE:], preferred_element_type=F32)
    tr = (_dot_nt(wt_ref[...], h) * ts_ref[...]).astype(BF16)
    for c in range(ROW_TILE // A_TILE):
        avt_ref[c] = tr[TOFF_AV:TOFF_BV, c * A_TILE:(c + 1) * A_TILE]
        aqt_ref[c] = tr[TOFF_AQ:, c * A_TILE:(c + 1) * A_TILE]
    for c in range(ROW_TILE // B_TILE):
        bvt_ref[c] = tr[TOFF_BV:TOFF_CV, c * B_TILE:(c + 1) * B_TILE]
    for c in range(ROW_TILE // C_TILE):
        cvt_ref[c] = tr[TOFF_CV:TOFF_AQ, c * C_TILE:(c + 1) * C_TILE]


def _inproj(x2, g, wrm, wt, cs, ts, layer):
    m, d = x2.shape
    tm = ROW_TILE
    row = lambda i: (i, 0)
    fixed = lambda i: (0, 0)
    tiles = lambda i: (i, 0, 0)
    this_layer = lambda i: (layer, 0, 0)
    widths = (A_W, B_W, B_KV_W, C_W, C_W)
    return pl.pallas_call(
        _inproj_kernel,
        grid=(m // tm,),
        in_specs=[
            pl.BlockSpec((tm, d), row),
            pl.BlockSpec((1, d), fixed),
            pl.BlockSpec((None, d, RM_W), this_layer),
            pl.BlockSpec((None, T_W, d), this_layer),
            pl.BlockSpec((1, OFF_GATE), fixed),
            pl.BlockSpec((T_W, 1), fixed),
        ],
        out_specs=[pl.BlockSpec((tm, w), row) for w in widths] + [
            pl.BlockSpec((tm, GATE_W), row),
            pl.BlockSpec((tm // A_TILE, A_W, A_TILE), tiles),
            pl.BlockSpec((tm // B_TILE, B_KV_W, B_TILE), tiles),
            pl.BlockSpec((tm // C_TILE, C_W, C_TILE), tiles),
            pl.BlockSpec((tm // A_TILE, A_W, A_TILE), tiles),
        ],
        out_shape=[jax.ShapeDtypeStruct((m, w), BF16) for w in widths] + [
            jax.ShapeDtypeStruct((m, GATE_W), F32),
            jax.ShapeDtypeStruct((m // A_TILE, A_W, A_TILE), BF16),
            jax.ShapeDtypeStruct((m // B_TILE, B_KV_W, B_TILE), BF16),
            jax.ShapeDtypeStruct((m // C_TILE, C_W, C_TILE), BF16),
            jax.ShapeDtypeStruct((m // A_TILE, A_W, A_TILE), BF16),
        ],
        compiler_params=pltpu.CompilerParams(
            dimension_semantics=("parallel",), vmem_limit_bytes=VMEM_LIMIT_BYTES),
        name="inproj",
    )(x2, g, wrm, wt, cs, ts)


def _attn_a_kernel(lq1_ref, lk1_ref, lq2_ref, lk2_ref, sg_ref, qfeat_ref,
                   qt_ref, k_ref, kfeat_ref, vt_ref, gate_ref, o_ref,
                   qz_sc, m_sc, acc_sc, s_sc, smax_sc, ref_sc, *, lambda_init, alibi_step):
    t = A_TILE
    tk = A_KEYS
    qi = pl.program_id(1)
    n_full = qi // 2
    m_sc[...] = jnp.full(m_sc.shape, MASK_VALUE, F32)
    acc_sc[...] = jnp.zeros(acc_sc.shape, F32)
    ones_rows = jnp.ones((A_NORM_ROWS, tk), BF16)

    sub = lax.broadcasted_iota(jnp.int32, (LANES, t), 0)
    for pr in range(2):
        qp = qt_ref[pr * LANES:(pr + 1) * LANES, :]
        for c in range(4):
            hh, mp = divmod(c, 2)
            lo = hh * HEAD_DIM + mp * A_QK_DIM
            real = jnp.where((sub >= lo) & (sub < lo + A_QK_DIM), qp, jnp.zeros_like(qp))
            h = 2 * pr + hh
            feat = jnp.broadcast_to(qfeat_ref[:, h:h + 1], (LANES, t)).astype(BF16)
            qz_sc[pr, c] = jnp.concatenate([real, feat], axis=0)

    def key_rows(pr, step, nk):
        ks = pl.ds(pl.multiple_of(step * tk, tk), nk)
        return jnp.concatenate([k_ref[ks, pr * LANES:(pr + 1) * LANES], kfeat_ref[ks, :]], axis=1)

    def scores(pr, c, lhs):
        return jnp.dot(lhs, qz_sc[pr, c], preferred_element_type=F32)

    def softmax_block(pr, c, nk):
        m_old = m_sc[pr, c]
        m_new = jnp.maximum(m_old, smax_sc[pr, c])
        m_sc[pr, c] = m_new
        shift = (m_new - ref_sc[pr, c]).astype(BF16)
        return jnp.exp2(m_old - m_new), jnp.exp2(s_sc[pr, c, :nk, :] - shift)

    def value_block(pr, c, step, nk, alpha, p):
        hh, mp = divmod(c, 2)
        h = 2 * pr + hh
        rows = slice(h * HEAD_DIM, (h + 1) * HEAD_DIM)
        vt = jnp.concatenate([vt_ref[2 * step + blk, rows, :] for blk in range(nk // t)], axis=1)
        vt1 = jnp.concatenate([vt, ones_rows[:, :nk]], axis=0)
        pv = jnp.dot(vt1, p, preferred_element_type=F32)
        acc_sc[h, mp] = alpha * acc_sc[h, mp] + pv

    def causal_mask(nk):
        kmq = (lax.broadcasted_iota(jnp.int32, (nk, t), 0)
               - lax.broadcasted_iota(jnp.int32, (nk, t), 1))
        return kmq <= qi * t - n_full * tk

    def stage_scores(pr, c, s, ref):
        s_sc[pr, c, :s.shape[0], :] = (s - ref).astype(BF16)
        ref_sc[pr, c] = ref
        smax_sc[pr, c] = jnp.max(s, axis=0, keepdims=True)

    def trip(i, nk_cur=tk, nk_next=tk, masked_next=False):
        for pr in range(2):
            lhs = key_rows(pr, i + 1, nk_next) if nk_next else None
            for c in range(4):
                if nk_next:
                    sn = scores(pr, c, lhs)
                alpha, p = softmax_block(pr, c, nk_cur)
                value_block(pr, c, i, nk_cur, alpha, p)
                if nk_next:
                    if masked_next:
                        sn = jnp.where(causal_mask(nk_next), sn, MASK_VALUE)
                    stage_scores(pr, c, sn, m_sc[pr, c] + alibi_step[2 * pr + c // 2])

    keep = jnp.logical_or(causal_mask(tk), n_full > 0)
    for pr in range(2):
        lhs0 = key_rows(pr, 0, tk)
        for c in range(4):
            s0 = jnp.where(keep, scores(pr, c, lhs0), MASK_VALUE)
            stage_scores(pr, c, s0, jnp.max(s0, axis=0, keepdims=True))

    def body(i, carry):
        trip(i)
        return carry

    lax.fori_loop(0, n_full - 1, body, 0)

    short = jnp.logical_and(qi % 2 == 0, n_full > 0)

    def tail(nk):
        @pl.when(n_full > 0)
        def _():
            trip(n_full - 1, nk_next=nk, masked_next=True)

        trip(n_full, nk_cur=nk, nk_next=0)

    @pl.when(short)
    def _():
        tail(t)

    @pl.when(jnp.logical_not(short))
    def _():
        tail(tk)

    lam = (jnp.exp(jnp.sum(lq1_ref[...] * lk1_ref[...], axis=-1, keepdims=True))
           - jnp.exp(jnp.sum(lq2_ref[...] * lk2_ref[...], axis=-1, keepdims=True))
           + lambda_init)
    outs = []
    for h in range(A_HEADS):
        o0, o1 = (acc_sc[h, mp, :HEAD_DIM, :] / acc_sc[h, mp, HEAD_DIM:HEAD_DIM + 1, :]
                  for mp in range(2))
        o = o0 - lam * o1
        ms = jnp.mean(o * o, axis=0, keepdims=True)
        outs.append(o * lax.rsqrt(ms + NORM_EPS))
    o_all = jnp.concatenate(outs, axis=0) * sg_ref[...] * (1.0 - lambda_init)
    o_ref[...] = (o_all.T * _silu(gate_ref[...])).astype(BF16)


def _attn_a(aqt, ak, kfeat, avt, gate, qfeat, lq1, lk1, lq2, lk2, sg4, lambda_init, alibi_step, b, s):
    t = A_TILE
    tk = A_KEYS
    nt = s // t
    small = lambda shape: pl.BlockSpec(shape, lambda bi, qi: (0, 0))
    return pl.pallas_call(
        functools.partial(_attn_a_kernel, lambda_init=lambda_init, alibi_step=alibi_step),
        grid=(b, nt),
        in_specs=[
            small((1, A_QK_DIM)), small((1, A_QK_DIM)), small((1, A_QK_DIM)), small((1, A_QK_DIM)),
            small((A_W, 1)),
            small((LANES, A_HEADS)),
            pl.BlockSpec((None, A_W, t), lambda bi, qi: (bi * nt + qi, 0, 0)),
            pl.BlockSpec((s, A_W), lambda bi, qi: (bi, 0)),
            pl.BlockSpec((s, LANES), lambda bi, qi: (0, 0)),
            pl.BlockSpec((nt, A_W, t), lambda bi, qi: (bi, 0, 0)),
            pl.BlockSpec((t, A_W), lambda bi, qi: (bi * nt + qi, GATE_BLOCK_A)),
        ],
        out_specs=pl.BlockSpec((t, A_W), lambda bi, qi: (bi * nt + qi, 0)),
        out_shape=jax.ShapeDtypeStruct((b * s, A_W), BF16),
        scratch_shapes=[
            pltpu.VMEM((2, 4, 2 * LANES, t), BF16),
            pltpu.VMEM((2, 4, 1, t), F32),
            pltpu.VMEM((A_HEADS, 2, HEAD_DIM + A_NORM_ROWS, t), F32),
            pltpu.VMEM((2, 4, tk, t), BF16),
            pltpu.VMEM((2, 4, 1, t), F32),
            pltpu.VMEM((2, 4, 1, t), F32),
        ],
        compiler_params=pltpu.CompilerParams(
            dimension_semantics=("parallel", "arbitrary"),
            vmem_limit_bytes=VMEM_LIMIT_BYTES),
        name="attn_a",
    )(lq1, lk1, lq2, lk2, sg4, qfeat, aqt, ak, kfeat, avt, gate)


def _attn_b_kernel(q_ref, kp_ref, kc_ref, vtp_ref, vtc_ref, gate_ref, bias_ref, sink_ref, o_ref):
    t = B_TILE
    qi = pl.program_id(1)
    lane = lax.broadcasted_iota(jnp.int32, (t, LANES), 1)
    sink = sink_ref[...]
    k_blocks = [kp_ref[...]] + [kc_ref[c * t:(c + 1) * t, :] for c in range(B_BLOCKS)]
    vt_blocks = [vtp_ref[...]] + [vtc_ref[c] for c in range(B_BLOCKS)]
    for c in range(B_BLOCKS):
        rows = slice(c * t, (c + 1) * t)
        k_all = jnp.concatenate(k_blocks[c:c + 2], axis=0)
        vt_all = jnp.concatenate(vt_blocks[c:c + 2], axis=1)
        pieces = [None] * B_HEADS
        for j in range(B_GROUP):
            qp = q_ref[rows, j * LANES:(j + 1) * LANES]
            zero = jnp.zeros_like(qp)
            pieces[j] = jnp.where(lane < HEAD_DIM, qp, zero)
            pieces[B_GROUP + j] = jnp.where(lane >= HEAD_DIM, qp, zero)
        qz = jnp.concatenate(pieces, axis=0)
        bias = bias_ref[jnp.minimum(qi, 1)] if c == 0 else bias_ref[1]
        s = _dot_nt(k_all, qz) + bias
        m = jnp.maximum(jnp.max(s, axis=0, keepdims=True), sink)
        p = jnp.exp2(s - m)
        inv = 1.0 / (jnp.sum(p, axis=0, keepdims=True) + jnp.exp2(sink - m))
        pb = p.astype(BF16)
        outs = []
        for g in range(B_KV_HEADS):
            cols = slice(g * B_GROUP * t, (g + 1) * B_GROUP * t)
            og = jnp.dot(vt_all[g * HEAD_DIM:(g + 1) * HEAD_DIM, :], pb[:, cols],
                         preferred_element_type=F32) * inv[:, cols]
            outs.extend(og[:, j * t:(j + 1) * t] for j in range(B_GROUP))
        o_all = jnp.concatenate(outs, axis=0)
        o_ref[rows, :] = (o_all.T * _silu(gate_ref[rows, :])).astype(BF16)


def _attn_b(bq, bk, bvt, gate, bias, sink_row, b, s):
    t = B_TILE
    tq = B_BLOCKS * t
    nt = s // tq
    cur = lambda bi, qi: (bi * nt + qi, 0)
    prev = lambda bi, qi: (jnp.maximum((bi * nt + qi) * B_BLOCKS - 1, 0), 0)
    cur3 = lambda bi, qi: (bi * nt + qi, 0, 0)
    prev3 = lambda bi, qi: (jnp.maximum((bi * nt + qi) * B_BLOCKS - 1, 0), 0, 0)
    return pl.pallas_call(
        _attn_b_kernel,
        grid=(b, nt),
        in_specs=[
            pl.BlockSpec((tq, B_W), cur),
            pl.BlockSpec((t, B_KV_W), prev),
            pl.BlockSpec((tq, B_KV_W), cur),
            pl.BlockSpec((None, B_KV_W, t), prev3),
            pl.BlockSpec((B_BLOCKS, B_KV_W, t), cur3),
            pl.BlockSpec((tq, B_W), cur),
            pl.BlockSpec((2, 2 * t, B_HEADS * t), lambda bi, qi: (0, 0, 0)),
            pl.BlockSpec((1, B_HEADS * t), lambda bi, qi: (0, 0)),
        ],
        out_specs=pl.BlockSpec((tq, B_W), cur),
        out_shape=jax.ShapeDtypeStruct((b * s, B_W), BF16),
        compiler_params=pltpu.CompilerParams(
            dimension_semantics=("parallel", "arbitrary"),
            vmem_limit_bytes=VMEM_LIMIT_BYTES),
        name="attn_b",
    )(bq, bk, bk, bvt, bvt, gate, bias, sink_row)


def _b_bias(slopes):
    t = B_TILE
    koff = np.arange(2 * t)[:, None] - t
    qoff = np.arange(t)[None, :]
    dist = qoff - koff
    band = (dist >= 0) & (dist < WINDOW)
    out = np.empty((2, 2 * t, B_HEADS * t), np.float32)
    for h in range(B_HEADS):
        term = (-np.float32(slopes[h]) * dist.astype(np.float32) * np.float32(LOG2_E)).astype(np.float32)
        out[1, :, h * t:(h + 1) * t] = np.where(band, term, np.float32(MASK_VALUE))
        out[0, :, h * t:(h + 1) * t] = np.where(band & (koff >= 0), term, np.float32(MASK_VALUE))
    return jnp.asarray(out)


def _attn_c_kernel(q_ref, k_ref, vt_ref, u_ref, gate_ref, o_ref, qz_sc, carry_sc, acc_sc):
    t = C_TILE
    step = pl.program_id(1)
    lane = lax.broadcasted_iota(jnp.int32, (t, C_W), 1)
    for sub in range(C_SUBTILES):
        q = q_ref[sub * t:(sub + 1) * t, :]
        for h in range(C_HEADS):
            own = (lane >= h * HEAD_DIM) & (lane < (h + 1) * HEAD_DIM)
            qz_sc[sub, h * t:(h + 1) * t, :] = jnp.where(own, q, jnp.zeros_like(q))
    u2 = u_ref[...]

    def tile(sub, kj, masked, carry_in):
        ks = pl.ds(pl.multiple_of(kj * t, t), t)
        z = _dot_nt(k_ref[ks, :], qz_sc[sub])
        lse = jnp.log2(1.0 + jnp.exp2(jnp.minimum(z, -z)))
        log_beta = jnp.minimum(z, 0.0) - lse
        l1m = log_beta - z
        if masked:
            krow = lax.broadcasted_iota(jnp.int32, (t, C_HEADS * t), 0)
            qcol = lax.broadcasted_iota(jnp.int32, (t, C_HEADS * t), 1) & (t - 1)
            past = krow < qcol
            l1m = jnp.where(past, l1m, 0.0)
        hi = l1m.astype(BF16)
        lo = (l1m - hi.astype(F32)).astype(BF16)
        tail = jnp.dot(u2, jnp.concatenate([hi, lo], axis=0),
                       preferred_element_type=F32)
        a = jnp.exp2(log_beta + tail + carry_in)
        if masked:
            a = jnp.where(past, a, 0.0)
        a = a.astype(BF16)
        parts = [jnp.dot(vt_ref[kj, h * HEAD_DIM:(h + 1) * HEAD_DIM, :], a[:, h * t:(h + 1) * t],
                         preferred_element_type=F32) for h in range(C_HEADS)]
        return jnp.concatenate(parts, axis=0), carry_in + jnp.sum(l1m, axis=0, keepdims=True)

    tiles = [C_SUBTILES * step + sub for sub in range(C_SUBTILES)]
    alive = []
    for sub in range(C_SUBTILES):
        acc, carry = tile(sub, tiles[sub], True, jnp.zeros((1, C_HEADS * t), F32))
        acc_sc[sub] = acc
        carry_sc[sub] = carry
        alive.append(jnp.logical_and(tiles[sub] > 0, jnp.max(carry) > C_ZERO_WEIGHT_LOG2))

    def cond(state):
        return functools.reduce(jnp.logical_or, state[1:])

    def body(state):
        r = state[0]
        nxt = []
        for sub in range(C_SUBTILES):
            has_keys = tiles[sub] - 1 - r >= 0
            acc, carry = tile(sub, jnp.maximum(tiles[sub] - 1 - r, 0), False, carry_sc[sub])
            acc_sc[sub] += jnp.where(has_keys, acc, 0.0)
            carry_sc[sub] = jnp.where(has_keys, carry, carry_sc[sub])
            more_keys = tiles[sub] - 2 - r >= 0
            nxt.append(jnp.logical_and(more_keys, jnp.max(carry) > C_ZERO_WEIGHT_LOG2))
        return (r + 1, *nxt)

    lax.while_loop(cond, body, (jnp.int32(0), *alive))
    for sub in range(C_SUBTILES):
        rows = slice(sub * t, (sub + 1) * t)
        o_ref[rows, :] = (acc_sc[sub].T * _silu(gate_ref[rows, :])).astype(BF16)


def _attn_c(cq, ck, cvt, u2, gate, b, s):
    t = C_TILE
    tq = C_SUBTILES * t
    nt = s // t
    nq = s // tq
    return pl.pallas_call(
        _attn_c_kernel,
        grid=(b, nq),
        in_specs=[
            pl.BlockSpec((tq, C_W), lambda bi, qi: (bi * nq + qi, 0)),
            pl.BlockSpec((s, C_W), lambda bi, qi: (bi, 0)),
            pl.BlockSpec((nt, C_W, t), lambda bi, qi: (bi, 0, 0)),
            pl.BlockSpec((t, 2 * t), lambda bi, qi: (0, 0)),
            pl.BlockSpec((tq, C_W), lambda bi, qi: (bi * nq + qi, GATE_BLOCK_C)),
        ],
        out_specs=pl.BlockSpec((tq, C_W), lambda bi, qi: (bi * nq + qi, 0)),
        out_shape=jax.ShapeDtypeStruct((b * s, C_W), BF16),
        scratch_shapes=[
            pltpu.VMEM((C_SUBTILES, C_HEADS * t, C_W), BF16),
            pltpu.VMEM((C_SUBTILES, 1, C_HEADS * t), F32),
            pltpu.VMEM((C_SUBTILES, C_W, t), F32),
        ],
        compiler_params=pltpu.CompilerParams(
            dimension_semantics=("parallel", "arbitrary"),
            vmem_limit_bytes=VMEM_LIMIT_BYTES),
        name="attn_c",
    )(cq, ck, cvt, u2, gate)


def _outproj_kernel(ya_ref, yb_ref, yc_ref, x_ref, w_ref, fg_ref, o_ref, *, final_norm):
    y = jnp.concatenate([ya_ref[...], yb_ref[...], yc_ref[...]], axis=1)
    r = x_ref[...] + jnp.dot(y, w_ref[...], preferred_element_type=F32)
    if final_norm:
        ms = jnp.mean(r * r, axis=-1, keepdims=True)
        r = r * lax.rsqrt(ms + NORM_EPS) * fg_ref[...]
    o_ref[...] = r


def _outproj(ya, yb, yc, x2, w, fg, layer, final_norm):
    m, d = x2.shape
    row = lambda i: (i, 0)
    fixed = lambda i: (0, 0)
    return pl.pallas_call(
        functools.partial(_outproj_kernel, final_norm=final_norm),
        grid=(m // OUT_ROW_TILE,),
        in_specs=[
            pl.BlockSpec((OUT_ROW_TILE, A_W), row),
            pl.BlockSpec((OUT_ROW_TILE, B_W), row),
            pl.BlockSpec((OUT_ROW_TILE, C_W), row),
            pl.BlockSpec((OUT_ROW_TILE, d), row),
            pl.BlockSpec((None, GATE_W, d), lambda i: (layer, 0, 0)),
            pl.BlockSpec((1, d), fixed),
        ],
        out_specs=pl.BlockSpec((OUT_ROW_TILE, d), row),
        out_shape=jax.ShapeDtypeStruct((m, d), F32),
        compiler_params=pltpu.CompilerParams(
            dimension_semantics=("parallel",), vmem_limit_bytes=VMEM_LIMIT_BYTES),
        name="outproj",
    )(ya, yb, yc, x2, w, fg)


def _split_w_in(w):
    splits = (A_W, A_W, A_W, A_W, B_W, B_KV_W, B_KV_W, B_W, C_W, C_W, C_W, C_W)
    pts = np.cumsum(splits)[:-1]
    aq, ak, av, ag, bq, bk, bv, bg, cq, ck, cv, cg = jnp.split(w, pts, axis=-1)
    depth, d = w.shape[:2]
    bq = bq.reshape(depth, d, B_KV_HEADS, B_GROUP, HEAD_DIM).transpose(0, 1, 3, 2, 4)
    bq = bq.reshape(depth, d, B_W)
    wrm = jnp.concatenate([ak, bq, bk, cq, ck, bg, ag, cg], axis=-1)
    wt = jnp.swapaxes(jnp.concatenate([av, bv, cv, aq], axis=-1), 1, 2)
    return wrm.astype(BF16), wt.astype(BF16)


def _q_scales():
    cs = np.ones((1, OFF_GATE), np.float32)
    cs[0, OFF_BQ:OFF_BK] = HEAD_DIM ** -0.5 * LOG2_E
    cs[0, OFF_CQ:OFF_CK] = HEAD_DIM ** -0.5 * LOG2_E
    ts = np.ones((T_W, 1), np.float32)
    ts[TOFF_AQ:, 0] = A_QK_DIM ** -0.5 * LOG2_E
    return jnp.asarray(cs), jnp.asarray(ts)


def kernel(x, norm_g, w_in, lambda_q1, lambda_k1, lambda_q2, lambda_k2, subln_g, sinks, w_out, final_g):
    b, s, d = x.shape
    depth = w_in.shape[0]
    m = b * s
    slopes_a, slopes_b = _alibi_slopes()

    pos = jnp.arange(s, dtype=jnp.int32)
    p_hi = (pos // 256).astype(BF16)
    p_lo = (pos % 256).astype(BF16)
    kfeat = jnp.stack([p_hi] * 3 + [p_lo] * 3, axis=-1)
    kfeat = jnp.pad(kfeat, ((0, 0), (0, LANES - 6)))
    qfeat_np = np.zeros((LANES, A_HEADS), np.float32)
    for h, sl in enumerate(slopes_a):
        sl2 = np.float32(np.float32(sl) * np.float32(LOG2_E))
        qfeat_np[:6, h] = _split3_bf16(np.float32(256.0) * sl2) + _split3_bf16(sl2)
    qfeat = jnp.asarray(qfeat_np)
    alibi_step = tuple(float(sl) * LOG2_E * A_KEYS for sl in slopes_a)

    tri = (lax.broadcasted_iota(jnp.int32, (C_TILE, C_TILE), 1)
           > lax.broadcasted_iota(jnp.int32, (C_TILE, C_TILE), 0)).astype(BF16)
    tri2 = jnp.concatenate([tri, tri], axis=1)
    b_bias = _b_bias(slopes_b)
    cs, ts = _q_scales()

    x2 = x.reshape(m, d)
    wrm_all, wt_all = _split_w_in(w_in)
    w_out_all = w_out.astype(BF16)
    sg_all = jnp.tile(subln_g, (1, A_HEADS)).reshape(depth, -1, 1)
    sink_rows = jnp.repeat(sinks * LOG2_E, B_TILE, axis=1).reshape(depth, 1, -1)
    for l in range(depth):
        lambda_init = 0.8 - 0.6 * math.exp(-0.3 * l)
        ak, bq, bk, cq, ck, gate, avt, bvt, cvt, aqt = _inproj(
            x2, norm_g[l].reshape(1, d), wrm_all, wt_all, cs, ts, l)
        ya = _attn_a(aqt, ak, kfeat, avt, gate, qfeat,
                     lambda_q1[l].reshape(1, -1), lambda_k1[l].reshape(1, -1),
                     lambda_q2[l].reshape(1, -1), lambda_k2[l].reshape(1, -1),
                     sg_all[l], lambda_init, alibi_step, b, s)
        yb = _attn_b(bq, bk, bvt, gate, b_bias, sink_rows[l], b, s)
        yc = _attn_c(cq, ck, cvt, tri2, gate, b, s)
        x2 = _outproj(ya, yb, yc, x2, w_out_all, final_g.reshape(1, d), l,
                      final_norm=(l == depth - 1))
    return x2.reshape(b, s, d)
```

```python
import functools
import math

import numpy as np
import jax
import jax.numpy as jnp
from jax import lax
from jax.experimental import pallas as pl
from jax.experimental.pallas import tpu as pltpu

F32 = jnp.float32
BF16 = jnp.bfloat16

HEAD_DIM = 64
NORM_EPS = 1e-5
A_HEADS = 4
A_QK_DIM = HEAD_DIM // 2
B_HEADS = 8
B_KV_HEADS = 2
B_GROUP = B_HEADS // B_KV_HEADS
C_HEADS = 4
WINDOW = 128
A_W = A_HEADS * HEAD_DIM
B_W = B_HEADS * HEAD_DIM
B_KV_W = B_KV_HEADS * HEAD_DIM
C_W = C_HEADS * HEAD_DIM
GATE_W = A_W + B_W + C_W
LANES = 128
MASK_VALUE = -1e30
LOG2_E = math.log2(math.e)
C_ZERO_WEIGHT_LOG2 = -105.0 * LOG2_E
VMEM_LIMIT_BYTES = 48 * 1024 * 1024

ROW_TILE = 512
OUT_ROW_TILE = 1024
A_TILE = 256
A_KEYS = 2 * A_TILE
A_NORM_ROWS = 16
C_TILE = 256
C_SUBTILES = 4
B_TILE = WINDOW
B_BLOCKS = 8

OFF_AK = 0
OFF_BQ = OFF_AK + A_W
OFF_BK = OFF_BQ + B_W
OFF_CQ = OFF_BK + B_KV_W
OFF_CK = OFF_CQ + C_W
OFF_GATE = OFF_CK + C_W
RM_W = OFF_GATE + GATE_W
TOFF_AV = 0
TOFF_BV = TOFF_AV + A_W
TOFF_CV = TOFF_BV + B_KV_W
TOFF_AQ = TOFF_CV + C_W
T_W = TOFF_AQ + A_W
GATE_BLOCK_A = B_W // A_W
GATE_BLOCK_C = (B_W + A_W) // C_W


def _alibi_slopes():
    n = A_HEADS + B_HEADS
    s = (2.0 ** (-8.0 * np.arange(1, n + 1) / n)).astype(np.float32)
    return s[B_HEADS:], s[:B_HEADS]


def _split3_bf16(v):
    out = []
    r = np.float32(v)
    for _ in range(3):
        p = np.asarray(r, np.float32).astype(BF16).astype(np.float32)
        out.append(float(p))
        r = np.float32(r - p)
    return out


def _dot_nt(a, b):
    return lax.dot_general(a, b, (((1,), (1,)), ((), ())), preferred_element_type=F32)


def _silu(g):
    return g * jax.nn.sigmoid(g)


def _inproj_kernel(x_ref, g_ref, wrm_ref, wt_ref, cs_ref, ts_ref,
                   ak_ref, bq_ref, bk_ref, cq_ref, ck_ref, gate_ref,
                   avt_ref, bvt_ref, cvt_ref, aqt_ref):
    x = x_ref[...]
    ms = jnp.mean(x * x, axis=-1, keepdims=True)
    h = (x * lax.rsqrt(ms + NORM_EPS) * g_ref[...]).astype(BF16)
    qk = jnp.dot(h, wrm_ref[:, :OFF_GATE], preferred_element_type=F32) * cs_ref[...]
    ak_ref[...] = qk[:, OFF_AK:OFF_BQ].astype(BF16)
    bq_ref[...] = qk[:, OFF_BQ:OFF_BK].astype(BF16)
    bk_ref[...] = qk[:, OFF_BK:OFF_CQ].astype(BF16)
    cq_ref[...] = qk[:, OFF_CQ:OFF_CK].astype(BF16)
    ck_ref[...] = qk[:, OFF_CK:OFF_GATE].astype(BF16)
    gate_ref[...] = jnp.dot(h, wrm_ref[:, OFF_GATE:], preferred_element_type=F32)
    tr = (_dot_nt(wt_ref[...], h) * ts_ref[...]).astype(BF16)
    for c in range(ROW_TILE // A_TILE):
        avt_ref[c] = tr[TOFF_AV:TOFF_BV, c * A_TILE:(c + 1) * A_TILE]
        aqt_ref[c] = tr[TOFF_AQ:, c * A_TILE:(c + 1) * A_TILE]
    for c in range(ROW_TILE // B_TILE):
        bvt_ref[c] = tr[TOFF_BV:TOFF_CV, c * B_TILE:(c + 1) * B_TILE]
    for c in range(ROW_TILE // C_TILE):
        cvt_ref[c] = tr[TOFF_CV:TOFF_AQ, c * C_TILE:(c + 1) * C_TILE]


def _inproj(x2, g, wrm, wt, cs, ts, layer):
    m, d = x2.shape
    tm = ROW_TILE
    row = lambda i: (i, 0)
    fixed = lambda i: (0, 0)
    tiles = lambda i: (i, 0, 0)
    this_layer = lambda i: (layer, 0, 0)
    widths = (A_W, B_W, B_KV_W, C_W, C_W)
    return pl.pallas_call(
        _inproj_kernel,
        grid=(m // tm,),
        in_specs=[
            pl.BlockSpec((tm, d), row),
            pl.BlockSpec((1, d), fixed),
            pl.BlockSpec((None, d, RM_W), this_layer),
            pl.BlockSpec((None, T_W, d), this_layer),
            pl.BlockSpec((1, OFF_GATE), fixed),
            pl.BlockSpec((T_W, 1), fixed),
        ],
        out_specs=[pl.BlockSpec((tm, w), row) for w in widths] + [
            pl.BlockSpec((tm, GATE_W), row),
            pl.BlockSpec((tm // A_TILE, A_W, A_TILE), tiles),
            pl.BlockSpec((tm // B_TILE, B_KV_W, B_TILE), tiles),
            pl.BlockSpec((tm // C_TILE, C_W, C_TILE), tiles),
            pl.BlockSpec((tm // A_TILE, A_W, A_TILE), tiles),
        ],
        out_shape=[jax.ShapeDtypeStruct((m, w), BF16) for w in widths] + [
            jax.ShapeDtypeStruct((m, GATE_W), F32),
            jax.ShapeDtypeStruct((m // A_TILE, A_W, A_TILE), BF16),
            jax.ShapeDtypeStruct((m // B_TILE, B_KV_W, B_TILE), BF16),
            jax.ShapeDtypeStruct((m // C_TILE, C_W, C_TILE), BF16),
            jax.ShapeDtypeStruct((m // A_TILE, A_W, A_TILE), BF16),
        ],
        compiler_params=pltpu.CompilerParams(
            dimension_semantics=("parallel",), vmem_limit_bytes=VMEM_LIMIT_BYTES),
        name="inproj",
    )(x2, g, wrm, wt, cs, ts)


def _attn_a_kernel(lq1_ref, lk1_ref, lq2_ref, lk2_ref, sg_ref, qfeat_ref,
                   qt_ref, k_ref, kfeat_ref, vt_ref, gate_ref, o_ref,
                   qz_sc, m_sc, acc_sc, s_sc, smax_sc, *, lambda_init):
    t = A_TILE
    tk = A_KEYS
    qi = pl.program_id(1)
    n_full = qi // 2
    m_sc[...] = jnp.full(m_sc.shape, MASK_VALUE, F32)
    acc_sc[...] = jnp.zeros(acc_sc.shape, F32)
    ones_rows = jnp.ones((A_NORM_ROWS, tk), BF16)

    sub = lax.broadcasted_iota(jnp.int32, (LANES, t), 0)
    for pr in range(2):
        qp = qt_ref[pr * LANES:(pr + 1) * LANES, :]
        for c in range(4):
            hh, mp = divmod(c, 2)
            lo = hh * HEAD_DIM + mp * A_QK_DIM
            real = jnp.where((sub >= lo) & (sub < lo + A_QK_DIM), qp, jnp.zeros_like(qp))
            h = 2 * pr + hh
            feat = jnp.broadcast_to(qfeat_ref[:, h:h + 1], (LANES, t)).astype(BF16)
            qz_sc[pr, :, c * t:(c + 1) * t] = jnp.concatenate([real, feat], axis=0)

    def key_rows(pr, step, nk):
        ks = pl.ds(pl.multiple_of(step * tk, tk), nk)
        return jnp.concatenate([k_ref[ks, pr * LANES:(pr + 1) * LANES], kfeat_ref[ks, :]], axis=1)

    def scores(pr, c, lhs):
        return jnp.dot(lhs, qz_sc[pr, :, c * t:(c + 1) * t], preferred_element_type=F32)

    def softmax_block(pr, c, nk):
        cols = slice(c * t, (c + 1) * t)
        m_old = m_sc[pr, :, cols]
        m_new = jnp.maximum(m_old, smax_sc[pr, :, cols])
        m_sc[pr, :, cols] = m_new
        return jnp.exp2(m_old - m_new), jnp.exp2((s_sc[pr, :nk, cols] - m_new).astype(BF16))

    def value_block(pr, c, step, nk, alpha, p):
        hh, mp = divmod(c, 2)
        h = 2 * pr + hh
        rows = slice(h * HEAD_DIM, (h + 1) * HEAD_DIM)
        ocols = slice(mp * t, (mp + 1) * t)
        vt = jnp.concatenate([vt_ref[2 * step + blk, rows, :] for blk in range(nk // t)], axis=1)
        vt1 = jnp.concatenate([vt, ones_rows[:, :nk]], axis=0)
        pv = jnp.dot(vt1, p, preferred_element_type=F32)
        acc_sc[h, :, ocols] = alpha * acc_sc[h, :, ocols] + pv

    def causal_mask(nk):
        kmq = (lax.broadcasted_iota(jnp.int32, (nk, t), 0)
               - lax.broadcasted_iota(jnp.int32, (nk, t), 1))
        return kmq <= qi * t - n_full * tk

    def stage_scores(pr, c, s):
        cols = slice(c * t, (c + 1) * t)
        s_sc[pr, :s.shape[0], cols] = s
        smax_sc[pr, :, cols] = jnp.max(s, axis=0, keepdims=True)

    def trip(i, nk_cur=tk, nk_next=tk, masked_next=False):
        for pr in range(2):
            lhs = key_rows(pr, i + 1, nk_next) if nk_next else None
            for c in range(4):
                if nk_next:
                    sn = scores(pr, c, lhs)
                alpha, p = softmax_block(pr, c, nk_cur)
                value_block(pr, c, i, nk_cur, alpha, p)
                if nk_next:
                    if masked_next:
                        sn = jnp.where(causal_mask(nk_next), sn, MASK_VALUE)
                    stage_scores(pr, c, sn)

    keep =jnp.logical_or(causal_mask(tk), n_full > 0)
    for pr in range(2):
        lhs0 = key_rows(pr, 0, tk)
        for c in range(4):
            stage_scores(pr, c, jnp.where(keep, scores(pr, c, lhs0), MASK_VALUE))

    def body(i, carry):
        trip(i)
        return carry

    lax.fori_loop(0, n_full - 1, body, 0)

    short = jnp.logical_and(qi % 2 == 0, n_full > 0)

    def tail(nk):
        @pl.when(n_full > 0)
        def _():
            trip(n_full - 1, nk_next=nk, masked_next=True)

        trip(n_full, nk_cur=nk, nk_next=0)

    @pl.when(short)
    def _():
        tail(t)

    @pl.when(jnp.logical_not(short))
    def _():
        tail(tk)

    lam = (jnp.exp(jnp.sum(lq1_ref[...] * lk1_ref[...], axis=-1, keepdims=True))
           - jnp.exp(jnp.sum(lq2_ref[...] * lk2_ref[...], axis=-1, keepdims=True))
           + lambda_init)
    outs = []
    for h in range(A_HEADS):
        pr, hh = divmod(h, 2)
        acc = acc_sc[h, :HEAD_DIM, :]
        l = acc_sc[h, HEAD_DIM:HEAD_DIM + 1, :]
        o = acc[:, :t] / l[:, :t] - lam * (acc[:, t:] / l[:, t:])
        ms = jnp.mean(o * o, axis=0, keepdims=True)
        outs.append(o * lax.rsqrt(ms + NORM_EPS))
    o_all = jnp.concatenate(outs, axis=0) * sg_ref[...] * (1.0 - lambda_init)
    o_ref[...] = (o_all.T * _silu(gate_ref[...])).astype(BF16)


def _attn_a(aqt, ak, kfeat, avt, gate, qfeat, lq1, lk1, lq2, lk2, sg4, lambda_init, b, s):
    t = A_TILE
    tk = A_KEYS
    nt = s // t
    small = lambda shape: pl.BlockSpec(shape, lambda bi, qi: (0, 0))
    return pl.pallas_call(
        functools.partial(_attn_a_kernel, lambda_init=lambda_init),
        grid=(b, nt),
        in_specs=[
            small((1, A_QK_DIM)), small((1, A_QK_DIM)), small((1, A_QK_DIM)), small((1, A_QK_DIM)),
            small((A_W, 1)),
            small((LANES, A_HEADS)),
            pl.BlockSpec((None, A_W, t), lambda bi, qi: (bi * nt + qi, 0, 0)),
            pl.BlockSpec((s, A_W), lambda bi, qi: (bi, 0)),
            pl.BlockSpec((s, LANES), lambda bi, qi: (0, 0)),
            pl.BlockSpec((nt, A_W, t), lambda bi, qi: (bi, 0, 0)),
            pl.BlockSpec((t, A_W), lambda bi, qi: (bi * nt + qi, GATE_BLOCK_A)),
        ],
        out_specs=pl.BlockSpec((t, A_W), lambda bi, qi: (bi * nt + qi, 0)),
        out_shape=jax.ShapeDtypeStruct((b * s, A_W), BF16),
        scratch_shapes=[
            pltpu.VMEM((2, 2 * LANES, 4 * t), BF16),
            pltpu.VMEM((2, 1, 4 * t), F32),
            pltpu.VMEM((A_HEADS, HEAD_DIM + A_NORM_ROWS, 2 * t), F32),
            pltpu.VMEM((2, tk, 4 * t), F32),
            pltpu.VMEM((2, 1, 4 * t), F32),
        ],
        compiler_params=pltpu.CompilerParams(
            dimension_semantics=("parallel", "arbitrary"),
            vmem_limit_bytes=VMEM_LIMIT_BYTES),
        name="attn_a",
    )(lq1, lk1, lq2, lk2, sg4, qfeat, aqt, ak, kfeat, avt, gate)


def _attn_b_kernel(q_ref, kp_ref, kc_ref, vtp_ref, vtc_ref, gate_ref, bias_ref, sink_ref, o_ref):
    t = B_TILE
    qi = pl.program_id(1)
    lane = lax.broadcasted_iota(jnp.int32, (t, LANES), 1)
    sink = sink_ref[...]
    k_blocks = [kp_ref[...]] + [kc_ref[c * t:(c + 1) * t, :] for c in range(B_BLOCKS)]
    vt_blocks = [vtp_ref[...]] + [vtc_ref[c] for c in range(B_BLOCKS)]
    for c in range(B_BLOCKS):
        rows = slice(c * t, (c + 1) * t)
        k_all = jnp.concatenate(k_blocks[c:c + 2], axis=0)
        vt_all = jnp.concatenate(vt_blocks[c:c + 2], axis=1)
        pieces = [None] * B_HEADS
        for j in range(B_GROUP):
            qp = q_ref[rows, j * LANES:(j + 1) * LANES]
            zero = jnp.zeros_like(qp)
            pieces[j] = jnp.where(lane < HEAD_DIM, qp, zero)
            pieces[B_GROUP + j] = jnp.where(lane >= HEAD_DIM, qp, zero)
        qz = jnp.concatenate(pieces, axis=0)
        bias = bias_ref[jnp.minimum(qi, 1)] if c == 0 else bias_ref[1]
        s = _dot_nt(k_all, qz) + bias
        m = jnp.maximum(jnp.max(s, axis=0, keepdims=True), sink)
        p = jnp.exp2(s - m)
        inv = 1.0 / (jnp.sum(p, axis=0, keepdims=True) + jnp.exp2(sink - m))
        pb = p.astype(BF16)
        outs = []
        for g in range(B_KV_HEADS):
            cols = slice(g * B_GROUP * t, (g + 1) * B_GROUP * t)
            og = jnp.dot(vt_all[g * HEAD_DIM:(g + 1) * HEAD_DIM, :], pb[:, cols],
                         preferred_element_type=F32) * inv[:, cols]
            outs.extend(og[:, j * t:(j + 1) * t] for j in range(B_GROUP))
        o_all = jnp.concatenate(outs, axis=0)
        o_ref[rows, :] = (o_all.T * _silu(gate_ref[rows, :])).astype(BF16)


def _attn_b(bq, bk, bvt, gate, bias, sink_row, b, s):
    t = B_TILE
    tq = B_BLOCKS * t
    nt = s // tq
    cur = lambda bi, qi: (bi * nt + qi, 0)
    prev = lambda bi, qi: (jnp.maximum((bi * nt + qi) * B_BLOCKS - 1, 0), 0)
    cur3 = lambda bi, qi: (bi * nt + qi, 0, 0)
    prev3 = lambda bi, qi: (jnp.maximum((bi * nt + qi) * B_BLOCKS - 1, 0), 0, 0)
    return pl.pallas_call(
        _attn_b_kernel,
        grid=(b, nt),
        in_specs=[
            pl.BlockSpec((tq, B_W), cur),
            pl.BlockSpec((t, B_KV_W), prev),
            pl.BlockSpec((tq, B_KV_W), cur),
            pl.BlockSpec((None, B_KV_W, t), prev3),
            pl.BlockSpec((B_BLOCKS, B_KV_W, t), cur3),
            pl.BlockSpec((tq, B_W), cur),
            pl.BlockSpec((2, 2 * t, B_HEADS * t), lambda bi, qi: (0, 0, 0)),
            pl.BlockSpec((1, B_HEADS * t), lambda bi, qi: (0, 0)),
        ],
        out_specs=pl.BlockSpec((tq, B_W), cur),
        out_shape=jax.ShapeDtypeStruct((b * s, B_W), BF16),
        compiler_params=pltpu.CompilerParams(
            dimension_semantics=("parallel", "arbitrary"),
            vmem_limit_bytes=VMEM_LIMIT_BYTES),
        name="attn_b",
    )(bq, bk, bk, bvt, bvt, gate, bias, sink_row)


def _b_bias(slopes):
    t = B_TILE
    koff = np.arange(2 * t)[:, None] - t
    qoff = np.arange(t)[None, :]
    dist = qoff - koff
    band = (dist >= 0) & (dist < WINDOW)
    out = np.empty((2, 2 * t, B_HEADS * t), np.float32)
    for h in range(B_HEADS):
        term = (-np.float32(slopes[h]) * dist.astype(np.float32) * np.float32(LOG2_E)).astype(np.float32)
        out[1, :, h * t:(h + 1) * t] = np.where(band, term, np.float32(MASK_VALUE))
        out[0, :, h * t:(h + 1) * t] = np.where(band & (koff >= 0), term, np.float32(MASK_VALUE))
    return jnp.asarray(out)


def _attn_c_kernel(q_ref, k_ref, vt_ref, u_ref, gate_ref, o_ref, qz_sc, carry_sc, acc_sc):
    t = C_TILE
    step = pl.program_id(1)
    lane = lax.broadcasted_iota(jnp.int32, (t, C_W), 1)
    for sub in range(C_SUBTILES):
        q = q_ref[sub * t:(sub + 1) * t, :]
        for h in range(C_HEADS):
            own = (lane >= h * HEAD_DIM) & (lane < (h + 1) * HEAD_DIM)
            qz_sc[sub, h * t:(h + 1) * t, :] = jnp.where(own, q, jnp.zeros_like(q))
    u2 = u_ref[...]

    def tile(sub, kj, masked, carry_in):
        ks = pl.ds(pl.multiple_of(kj * t, t), t)
        z = _dot_nt(k_ref[ks, :], qz_sc[sub])
        lse = jnp.log2(1.0 + jnp.exp2(jnp.minimum(z, -z)))
        log_beta = jnp.minimum(z, 0.0) - lse
        l1m = log_beta - z
        if masked:
            krow = lax.broadcasted_iota(jnp.int32, (t, C_HEADS * t), 0)
            qcol = lax.broadcasted_iota(jnp.int32, (t, C_HEADS * t), 1) & (t - 1)
            past = krow < qcol
            l1m = jnp.where(past, l1m, 0.0)
        hi = l1m.astype(BF16)
        lo = (l1m - hi.astype(F32)).astype(BF16)
        tail = jnp.dot(u2, jnp.concatenate([hi, lo], axis=0),
                       preferred_element_type=F32)
        a = jnp.exp2(log_beta + tail + carry_in)
        if masked:
            a = jnp.where(past, a, 0.0)
        a = a.astype(BF16)
        parts = [jnp.dot(vt_ref[kj, h * HEAD_DIM:(h + 1) * HEAD_DIM, :], a[:, h * t:(h + 1) * t],
                         preferred_element_type=F32) for h in range(C_HEADS)]
        return jnp.concatenate(parts, axis=0), carry_in + jnp.sum(l1m, axis=0, keepdims=True)

    tiles = [C_SUBTILES * step + sub for sub in range(C_SUBTILES)]
    alive = []
    for sub in range(C_SUBTILES):
        acc, carry = tile(sub, tiles[sub], True, jnp.zeros((1, C_HEADS * t), F32))
        acc_sc[sub] = acc
        carry_sc[sub] = carry
        alive.append(jnp.logical_and(tiles[sub] > 0, jnp.max(carry) > C_ZERO_WEIGHT_LOG2))

    def cond(state):
        return functools.reduce(jnp.logical_or, state[1:])

    def body(state):
        r = state[0]
        nxt = []
        for sub in range(C_SUBTILES):
            has_keys = tiles[sub] - 1 - r >= 0
            acc, carry = tile(sub, jnp.maximum(tiles[sub] - 1 - r, 0), False, carry_sc[sub])
            acc_sc[sub] += jnp.where(has_keys, acc, 0.0)
            carry_sc[sub] = jnp.where(has_keys, carry, carry_sc[sub])
            more_keys = tiles[sub] - 2 - r >= 0
            nxt.append(jnp.logical_and(more_keys, jnp.max(carry) > C_ZERO_WEIGHT_LOG2))
        return (r + 1, *nxt)

    lax.while_loop(cond, body, (jnp.int32(0), *alive))
    for sub in range(C_SUBTILES):
        rows = slice(sub * t, (sub + 1) * t)
        o_ref[rows, :] = (acc_sc[sub].T * _silu(gate_ref[rows, :])).astype(BF16)


def _attn_c(cq, ck, cvt, u2, gate, b, s):
    t = C_TILE
    tq = C_SUBTILES * t
    nt = s // t
    nq = s // tq
    return pl.pallas_call(
        _attn_c_kernel,
        grid=(b, nq),
        in_specs=[
            pl.BlockSpec((tq, C_W), lambda bi, qi: (bi * nq + qi, 0)),
            pl.BlockSpec((s, C_W), lambda bi, qi: (bi, 0)),
            pl.BlockSpec((nt, C_W, t), lambda bi, qi: (bi, 0, 0)),
            pl.BlockSpec((t, 2 * t), lambda bi, qi: (0, 0)),
            pl.BlockSpec((tq, C_W), lambda bi, qi: (bi * nq + qi, GATE_BLOCK_C)),
        ],
        out_specs=pl.BlockSpec((tq, C_W), lambda bi, qi: (bi * nq + qi, 0)),
        out_shape=jax.ShapeDtypeStruct((b * s, C_W), BF16),
        scratch_shapes=[
            pltpu.VMEM((C_SUBTILES, C_HEADS * t, C_W), BF16),
            pltpu.VMEM((C_SUBTILES, 1, C_HEADS * t), F32),
            pltpu.VMEM((C_SUBTILES, C_W, t), F32),
        ],
        compiler_params=pltpu.CompilerParams(
            dimension_semantics=("parallel", "arbitrary"),
            vmem_limit_bytes=VMEM_LIMIT_BYTES),
        name="attn_c",
    )(cq, ck, cvt, u2, gate)


def _outproj_kernel(ya_ref, yb_ref, yc_ref, x_ref, w_ref, fg_ref, o_ref, *, final_norm):
    y = jnp.concatenate([ya_ref[...], yb_ref[...], yc_ref[...]], axis=1)
    r = x_ref[...] + jnp.dot(y, w_ref[...], preferred_element_type=F32)
    if final_norm:
        ms = jnp.mean(r * r, axis=-1, keepdims=True)
        r = r * lax.rsqrt(ms + NORM_EPS) * fg_ref[...]
    o_ref[...] = r


def _outproj(ya, yb, yc, x2, w, fg, layer, final_norm):
    m, d = x2.shape
    row = lambda i: (i, 0)
    fixed = lambda i: (0, 0)
    return pl.pallas_call(
        functools.partial(_outproj_kernel, final_norm=final_norm),
        grid=(m // OUT_ROW_TILE,),
        in_specs=[
            pl.BlockSpec((OUT_ROW_TILE, A_W), row),
            pl.BlockSpec((OUT_ROW_TILE, B_W), row),
            pl.BlockSpec((OUT_ROW_TILE, C_W), row),
            pl.BlockSpec((OUT_ROW_TILE, d), row),
            pl.BlockSpec((None, GATE_W, d), lambda i: (layer, 0, 0)),
            pl.BlockSpec((1, d), fixed),
        ],
        out_specs=pl.BlockSpec((OUT_ROW_TILE, d), row),
        out_shape=jax.ShapeDtypeStruct((m, d), F32),
        compiler_params=pltpu.CompilerParams(
            dimension_semantics=("parallel",), vmem_limit_bytes=VMEM_LIMIT_BYTES),
        name="outproj",
    )(ya, yb, yc, x2, w, fg)


def _split_w_in(w):
    splits = (A_W, A_W, A_W, A_W, B_W, B_KV_W, B_KV_W, B_W, C_W, C_W, C_W, C_W)
    pts = np.cumsum(splits)[:-1]
    aq, ak, av, ag, bq, bk, bv, bg, cq, ck, cv, cg = jnp.split(w, pts, axis=-1)
    depth, d = w.shape[:2]
    bq = bq.reshape(depth, d, B_KV_HEADS, B_GROUP, HEAD_DIM).transpose(0, 1, 3, 2, 4)
    bq = bq.reshape(depth, d, B_W)
    wrm = jnp.concatenate([ak, bq, bk, cq, ck, bg, ag, cg], axis=-1)
    wt = jnp.swapaxes(jnp.concatenate([av, bv, cv, aq], axis=-1), 1, 2)
    return wrm.astype(BF16), wt.astype(BF16)


def _q_scales():
    cs = np.ones((1, OFF_GATE), np.float32)
    cs[0, OFF_BQ:OFF_BK] = HEAD_DIM ** -0.5 * LOG2_E
    cs[0, OFF_CQ:OFF_CK] = HEAD_DIM ** -0.5 * LOG2_E
    ts = np.ones((T_W, 1), np.float32)
    ts[TOFF_AQ:, 0] = A_QK_DIM ** -0.5 * LOG2_E
    return jnp.asarray(cs), jnp.asarray(ts)


def kernel(x, norm_g, w_in, lambda_q1, lambda_k1, lambda_q2, lambda_k2, subln_g, sinks, w_out, final_g):
    b, s, d = x.shape
    depth = w_in.shape[0]
    m = b * s
    slopes_a, slopes_b = _alibi_slopes()

    pos = jnp.arange(s, dtype=jnp.int32)
    p_hi = (pos // 256).astype(BF16)
    p_lo = (pos % 256).astype(BF16)
    kfeat = jnp.stack([p_hi] * 3 + [p_lo] * 3, axis=-1)
    kfeat = jnp.pad(kfeat, ((0, 0), (0, LANES - 6)))
    qfeat_np = np.zeros((LANES, A_HEADS), np.float32)
    for h, sl in enumerate(slopes_a):
        sl2 = np.float32(np.float32(sl) * np.float32(LOG2_E))
        qfeat_np[:6, h] = _split3_bf16(np.float32(256.0) * sl2) + _split3_bf16(sl2)
    qfeat = jnp.asarray(qfeat_np)

    tri = (lax.broadcasted_iota(jnp.int32, (C_TILE, C_TILE), 1)
           > lax.broadcasted_iota(jnp.int32, (C_TILE, C_TILE), 0)).astype(BF16)
    tri2 = jnp.concatenate([tri, tri], axis=1)
    b_bias = _b_bias(slopes_b)
    cs, ts = _q_scales()

    x2 = x.reshape(m, d)
    wrm_all, wt_all = _split_w_in(w_in)
    w_out_all = w_out.astype(BF16)
    sg_all = jnp.tile(subln_g, (1, A_HEADS)).reshape(depth, -1, 1)
    sink_rows = jnp.repeat(sinks * LOG2_E, B_TILE, axis=1).reshape(depth, 1, -1)
    for l in range(depth):
        lambda_init = 0.8 - 0.6 * math.exp(-0.3 * l)
        ak, bq, bk, cq, ck, gate, avt, bvt, cvt, aqt = _inproj(
            x2, norm_g[l].reshape(1, d), wrm_all, wt_all, cs, ts, l)
        ya = _attn_a(aqt, ak, kfeat, avt, gate, qfeat,
                     lambda_q1[l].reshape(1, -1), lambda_k1[l].reshape(1, -1),
                     lambda_q2[l].reshape(1, -1), lambda_k2[l].reshape(1, -1),
                     sg_all[l], lambda_init, b, s)
        yb = _attn_b(bq, bk, bvt, gate, b_bias, sink_rows[l], b, s)
        yc = _attn_c(cq, ck, cvt, tri2, gate, b, s)
        x2 = _outproj(ya, yb, yc, x2, w_out_all, final_g.reshape(1, d), l,
                      final_norm=(l == depth - 1))
    return x2.reshape(b, s, d)
```

```python
import functools
import math

import numpy as np
import jax
import jax.numpy as jnp
from jax import lax
from jax.experimental import pallas as pl
from jax.experimental.pallas import tpu as pltpu

F32 = jnp.float32
BF16 = jnp.bfloat16

HEAD_DIM = 64
NORM_EPS = 1e-5
A_HEADS = 4
A_QK_DIM = HEAD_DIM // 2
B_HEADS = 8
B_KV_HEADS = 2
B_GROUP = B_HEADS // B_KV_HEADS
C_HEADS = 4
WINDOW = 128
A_W = A_HEADS * HEAD_DIM
B_W = B_HEADS * HEAD_DIM
B_KV_W = B_KV_HEADS * HEAD_DIM
C_W = C_HEADS * HEAD_DIM
GATE_W = A_W + B_W + C_W
LANES = 128
MASK_VALUE = -1e30
LOG2_E = math.log2(math.e)
C_ZERO_WEIGHT_LOG2 = -105.0 * LOG2_E
VMEM_LIMIT_BYTES = 48 * 1024 * 1024

ROW_TILE = 512
OUT_ROW_TILE = 1024
A_TILE = 256
A_KEYS = 2 * A_TILE
A_NORM_ROWS = 16
C_TILE = 256
C_SUBTILES = 8
B_TILE = WINDOW
B_BLOCKS = 16

OFF_AK = 0
OFF_BQ = OFF_AK + A_W
OFF_BK = OFF_BQ + B_W
OFF_CQ = OFF_BK + B_KV_W
OFF_CK = OFF_CQ + C_W
OFF_GATE = OFF_CK + C_W
RM_W = OFF_GATE + GATE_W
TOFF_AV = 0
TOFF_BV = TOFF_AV + A_W
TOFF_CV = TOFF_BV + B_KV_W
TOFF_AQ = TOFF_CV + C_W
T_W = TOFF_AQ + A_W
GATE_BLOCK_A = B_W // A_W
GATE_BLOCK_C = (B_W + A_W) // C_W


def _alibi_slopes():
    n = A_HEADS + B_HEADS
    s = (2.0 ** (-8.0 * np.arange(1, n + 1) / n)).astype(np.float32)
    return s[B_HEADS:], s[:B_HEADS]


def _split3_bf16(v):
    out = []
    r = np.float32(v)
    for _ in range(3):
        p = np.asarray(r, np.float32).astype(BF16).astype(np.float32)
        out.append(float(p))
        r = np.float32(r - p)
    return out


def _dot_nt(a, b):
    return lax.dot_general(a, b, (((1,), (1,)), ((), ())), preferred_element_type=F32)


def _silu(g):
    return g * jax.nn.sigmoid(g)


def _inproj_kernel(x_ref, g_ref, wrm_ref, wt_ref, cs_ref, ts_ref,
                   ak_ref, bq_ref, bk_ref, cq_ref, ck_ref, gate_ref,
                   avt_ref, bvt_ref, cvt_ref, aqt_ref):
    x = x_ref[...]
    ms = jnp.mean(x * x, axis=-1, keepdims=True)
    h = (x * lax.rsqrt(ms + NORM_EPS) * g_ref[...]).astype(BF16)
    qk = jnp.dot(h, wrm_ref[:, :OFF_GATE], preferred_element_type=F32) * cs_ref[...]
    ak_ref[...] = qk[:, OFF_AK:OFF_BQ].astype(BF16)
    bq_ref[...] = qk[:, OFF_BQ:OFF_BK].astype(BF16)
    bk_ref[...] = qk[:, OFF_BK:OFF_CQ].astype(BF16)
    cq_ref[...] = qk[:, OFF_CQ:OFF_CK].astype(BF16)
    ck_ref[...] = qk[:, OFF_CK:OFF_GATE].astype(BF16)
    gate_ref[...] = jnp.dot(h, wrm_ref[:, OFF_GATE:], preferred_element_type=F32)
    tr = (_dot_nt(wt_ref[...], h) * ts_ref[...]).astype(BF16)
    for c in range(ROW_TILE // A_TILE):
        avt_ref[c] = tr[TOFF_AV:TOFF_BV, c * A_TILE:(c + 1) * A_TILE]
        aqt_ref[c] = tr[TOFF_AQ:, c * A_TILE:(c + 1) * A_TILE]
    for c in range(ROW_TILE // B_TILE):
        bvt_ref[c] = tr[TOFF_BV:TOFF_CV, c * B_TILE:(c + 1) * B_TILE]
    for c in range(ROW_TILE // C_TILE):
        cvt_ref[c] = tr[TOFF_CV:TOFF_AQ, c * C_TILE:(c + 1) * C_TILE]


def _inproj(x2, g, wrm, wt, cs, ts, layer):
    m, d = x2.shape
    tm = ROW_TILE
    row = lambda i: (i, 0)
    fixed = lambda i: (0, 0)
    tiles = lambda i: (i, 0, 0)
    this_layer = lambda i: (layer, 0, 0)
    widths = (A_W, B_W, B_KV_W, C_W, C_W)
    return pl.pallas_call(
        _inproj_kernel,
        grid=(m // tm,),
        in_specs=[
            pl.BlockSpec((tm, d), row),
            pl.BlockSpec((1, d), fixed),
            pl.BlockSpec((None, d, RM_W), this_layer),
            pl.BlockSpec((None, T_W, d), this_layer),
            pl.BlockSpec((1, OFF_GATE), fixed),
            pl.BlockSpec((T_W, 1), fixed),
        ],
        out_specs=[pl.BlockSpec((tm, w), row) for w in widths] + [
            pl.BlockSpec((tm, GATE_W), row),
            pl.BlockSpec((tm // A_TILE, A_W, A_TILE), tiles),
            pl.BlockSpec((tm // B_TILE, B_KV_W, B_TILE), tiles),
            pl.BlockSpec((tm // C_TILE, C_W, C_TILE), tiles),
            pl.BlockSpec((tm // A_TILE, A_W, A_TILE), tiles),
        ],
        out_shape=[jax.ShapeDtypeStruct((m, w), BF16) for w in widths] + [
            jax.ShapeDtypeStruct((m, GATE_W), F32),
            jax.ShapeDtypeStruct((m // A_TILE, A_W, A_TILE), BF16),
            jax.ShapeDtypeStruct((m // B_TILE, B_KV_W, B_TILE), BF16),
            jax.ShapeDtypeStruct((m // C_TILE, C_W, C_TILE), BF16),
            jax.ShapeDtypeStruct((m // A_TILE, A_W, A_TILE), BF16),
        ],
        compiler_params=pltpu.CompilerParams(
            dimension_semantics=("parallel",), vmem_limit_bytes=VMEM_LIMIT_BYTES),
        name="inproj",
    )(x2, g, wrm, wt, cs, ts)


def _attn_a_kernel(lq1_ref, lk1_ref, lq2_ref, lk2_ref, sg_ref, qfeat_ref,
                   qt_ref, k_ref, kfeat_ref, vt_ref, gate_ref, o_ref,
                   qz_sc, m_sc, acc_sc, s_sc, smax_sc, *, lambda_init):
    t = A_TILE
    tk = A_KEYS
    qi = pl.program_id(1)
    n_full = qi // 2
    m_sc[...] = jnp.full(m_sc.shape, MASK_VALUE, F32)
    acc_sc[...] = jnp.zeros(acc_sc.shape, F32)
    ones_rows = jnp.ones((A_NORM_ROWS, tk), BF16)

    sub = lax.broadcasted_iota(jnp.int32, (LANES, t), 0)
    for pr in range(2):
        qp = qt_ref[pr * LANES:(pr + 1) * LANES, :]
        for c in range(4):
            hh, mp = divmod(c, 2)
            lo = hh * HEAD_DIM + mp * A_QK_DIM
            real = jnp.where((sub >= lo) & (sub < lo + A_QK_DIM), qp, jnp.zeros_like(qp))
            h = 2 * pr + hh
            feat = jnp.broadcast_to(qfeat_ref[:, h:h + 1], (LANES, t)).astype(BF16)
            qz_sc[pr, :, c * t:(c + 1) * t] = jnp.concatenate([real, feat], axis=0)

    def key_rows(pr, step, nk):
        ks = pl.ds(pl.multiple_of(step * tk, tk), nk)
        return jnp.concatenate([k_ref[ks, pr * LANES:(pr + 1) * LANES], kfeat_ref[ks, :]], axis=1)

    def scores(pr, c, lhs):
        return jnp.dot(lhs, qz_sc[pr, :, c * t:(c + 1) * t], preferred_element_type=F32)

    def softmax_block(pr, c, nk):
        cols = slice(c * t, (c + 1) * t)
        m_old = m_sc[pr, :, cols]
        m_new = jnp.maximum(m_old, smax_sc[pr, :, cols])
        m_sc[pr, :, cols] = m_new
        return jnp.exp2(m_old - m_new), jnp.exp2((s_sc[pr, :nk, cols] - m_new).astype(BF16))

    def value_block(pr, c, step, nk, alpha, p):
        hh, mp = divmod(c, 2)
        h = 2 * pr + hh
        rows = slice(h * HEAD_DIM, (h + 1) * HEAD_DIM)
        ocols = slice(mp * t, (mp + 1) * t)
        vt = jnp.concatenate([vt_ref[2 * step + blk, rows, :] for blk in range(nk // t)], axis=1)
        vt1 = jnp.concatenate([vt, ones_rows[:, :nk]], axis=0)
        pv = jnp.dot(vt1, p, preferred_element_type=F32)
        acc_sc[h, :, ocols] = alpha * acc_sc[h, :, ocols] + pv

    def causal_mask(nk):
        kmq = (lax.broadcasted_iota(jnp.int32, (nk, t), 0)
               - lax.broadcasted_iota(jnp.int32, (nk, t), 1))
        return kmq <= qi * t - n_full * tk

    def stage_scores(pr, c, s):
        cols = slice(c * t, (c + 1) * t)
        s_sc[pr, :s.shape[0], cols] = s
        smax_sc[pr, :, cols] = jnp.max(s, axis=0, keepdims=True)

    def trip(i, nk_cur=tk, nk_next=tk, masked_next=False):
        for pr in range(2):
            lhs = key_rows(pr, i + 1, nk_next) if nk_next else None
            for c in range(4):
                if nk_next:
                    sn = scores(pr, c, lhs)
                alpha, p = softmax_block(pr, c, nk_cur)
                value_block(pr, c, i, nk_cur, alpha, p)
                if nk_next:
                    if masked_next:
                        sn = jnp.where(causal_mask(nk_next), sn, MASK_VALUE)
                    stage_scores(pr, c, sn)

    keep =jnp.logical_or(causal_mask(tk), n_full > 0)
    for pr in range(2):
        lhs0 = key_rows(pr, 0, tk)
        for c in range(4):
            stage_scores(pr, c, jnp.where(keep, scores(pr, c, lhs0), MASK_VALUE))

    def body(i, carry):
        trip(i)
        return carry

    lax.fori_loop(0, n_full - 1, body, 0)

    short = jnp.logical_and(qi % 2 == 0, n_full > 0)

    def tail(nk):
        @pl.when(n_full > 0)
        def _():
            trip(n_full - 1, nk_next=nk, masked_next=True)

        trip(n_full, nk_cur=nk, nk_next=0)

    @pl.when(short)
    def _():
        tail(t)

    @pl.when(jnp.logical_not(short))
    def _():
        tail(tk)

    lam = (jnp.exp(jnp.sum(lq1_ref[...] * lk1_ref[...], axis=-1, keepdims=True))
           - jnp.exp(jnp.sum(lq2_ref[...] * lk2_ref[...], axis=-1, keepdims=True))
           + lambda_init)
    outs = []
    for h in range(A_HEADS):
        pr, hh = divmod(h, 2)
        acc = acc_sc[h, :HEAD_DIM, :]
        l = acc_sc[h, HEAD_DIM:HEAD_DIM + 1, :]
        o = acc[:, :t] / l[:, :t] - lam * (acc[:, t:] / l[:, t:])
        ms = jnp.mean(o * o, axis=0, keepdims=True)
        outs.append(o * lax.rsqrt(ms + NORM_EPS))
    o_all = jnp.concatenate(outs, axis=0) * sg_ref[...] * (1.0 - lambda_init)
    o_ref[...] = (o_all.T * _silu(gate_ref[...])).astype(BF16)


def _attn_a(aqt, ak, kfeat, avt, gate, qfeat, lq1, lk1, lq2, lk2, sg4, lambda_init, b, s):
    t = A_TILE
    tk = A_KEYS
    nt = s // t
    small = lambda shape: pl.BlockSpec(shape, lambda bi, qi: (0, 0))
    return pl.pallas_call(
        functools.partial(_attn_a_kernel, lambda_init=lambda_init),
        grid=(b, nt),
        in_specs=[
            small((1, A_QK_DIM)), small((1, A_QK_DIM)), small((1, A_QK_DIM)), small((1, A_QK_DIM)),
            small((A_W, 1)),
            small((LANES, A_HEADS)),
            pl.BlockSpec((None, A_W, t), lambda bi, qi: (bi * nt + qi, 0, 0)),
            pl.BlockSpec((s, A_W), lambda bi, qi: (bi, 0)),
            pl.BlockSpec((s, LANES), lambda bi, qi: (0, 0)),
            pl.BlockSpec((nt, A_W, t), lambda bi, qi: (bi, 0, 0)),
            pl.BlockSpec((t, A_W), lambda bi, qi: (bi * nt + qi, GATE_BLOCK_A)),
        ],
        out_specs=pl.BlockSpec((t, A_W), lambda bi, qi: (bi * nt + qi, 0)),
        out_shape=jax.ShapeDtypeStruct((b * s, A_W), BF16),
        scratch_shapes=[
            pltpu.VMEM((2, 2 * LANES, 4 * t), BF16),
            pltpu.VMEM((2, 1, 4 * t), F32),
            pltpu.VMEM((A_HEADS, HEAD_DIM + A_NORM_ROWS, 2 * t), F32),
            pltpu.VMEM((2, tk, 4 * t), F32),
            pltpu.VMEM((2, 1, 4 * t), F32),
        ],
        compiler_params=pltpu.CompilerParams(
            dimension_semantics=("parallel", "arbitrary"),
            vmem_limit_bytes=VMEM_LIMIT_BYTES),
        name="attn_a",
    )(lq1, lk1, lq2, lk2, sg4, qfeat, aqt, ak, kfeat, avt, gate)


def _attn_b_kernel(q_ref, kp_ref, kc_ref, vtp_ref, vtc_ref, gate_ref, bias_ref, sink_ref, o_ref):
    t = B_TILE
    qi = pl.program_id(1)
    lane = lax.broadcasted_iota(jnp.int32, (t, LANES), 1)
    sink = sink_ref[...]
    k_blocks = [kp_ref[...]] + [kc_ref[c * t:(c + 1) * t, :] for c in range(B_BLOCKS)]
    vt_blocks = [vtp_ref[...]] + [vtc_ref[c] for c in range(B_BLOCKS)]
    for c in range(B_BLOCKS):
        rows = slice(c * t, (c + 1) * t)
        k_all = jnp.concatenate(k_blocks[c:c + 2], axis=0)
        vt_all = jnp.concatenate(vt_blocks[c:c + 2], axis=1)
        pieces = [None] * B_HEADS
        for j in range(B_GROUP):
            qp = q_ref[rows, j * LANES:(j + 1) * LANES]
            zero = jnp.zeros_like(qp)
            pieces[j] = jnp.where(lane < HEAD_DIM, qp, zero)
            pieces[B_GROUP + j] = jnp.where(lane >= HEAD_DIM, qp, zero)
        qz = jnp.concatenate(pieces, axis=0)
        bias = bias_ref[jnp.minimum(qi, 1)] if c == 0 else bias_ref[1]
        s = _dot_nt(k_all, qz) + bias
        m = jnp.maximum(jnp.max(s, axis=0, keepdims=True), sink)
        p = jnp.exp2(s - m)
        inv = 1.0 / (jnp.sum(p, axis=0, keepdims=True) + jnp.exp2(sink - m))
        pb = p.astype(BF16)
        outs = []
        for g in range(B_KV_HEADS):
            cols = slice(g * B_GROUP * t, (g + 1) * B_GROUP * t)
            og = jnp.dot(vt_all[g * HEAD_DIM:(g + 1) * HEAD_DIM, :], pb[:, cols],
                         preferred_element_type=F32) * inv[:, cols]
            outs.extend(og[:, j * t:(j + 1) * t] for j in range(B_GROUP))
        o_all = jnp.concatenate(outs, axis=0)
        o_ref[rows, :] = (o_all.T * _silu(gate_ref[rows, :])).astype(BF16)


def _attn_b(bq, bk, bvt, gate, bias, sink_row, b, s):
    t = B_TILE
    tq = B_BLOCKS * t
    nt = s // tq
    cur = lambda bi, qi: (bi * nt + qi, 0)
    prev = lambda bi, qi: (jnp.maximum((bi * nt + qi) * B_BLOCKS - 1, 0), 0)
    cur3 = lambda bi, qi: (bi * nt + qi, 0, 0)
    prev3 = lambda bi, qi: (jnp.maximum((bi * nt + qi) * B_BLOCKS - 1, 0), 0, 0)
    return pl.pallas_call(
        _attn_b_kernel,
        grid=(b, nt),
        in_specs=[
            pl.BlockSpec((tq, B_W), cur),
            pl.BlockSpec((t, B_KV_W), prev),
            pl.BlockSpec((tq, B_KV_W), cur),
            pl.BlockSpec((None, B_KV_W, t), prev3),
            pl.BlockSpec((B_BLOCKS, B_KV_W, t), cur3),
            pl.BlockSpec((tq, B_W), cur),
            pl.BlockSpec((2, 2 * t, B_HEADS * t), lambda bi, qi: (0, 0, 0)),
            pl.BlockSpec((1, B_HEADS * t), lambda bi, qi: (0, 0)),
        ],
        out_specs=pl.BlockSpec((tq, B_W), cur),
        out_shape=jax.ShapeDtypeStruct((b * s, B_W), BF16),
        compiler_params=pltpu.CompilerParams(
            dimension_semantics=("parallel", "arbitrary"),
            vmem_limit_bytes=VMEM_LIMIT_BYTES),
        name="attn_b",
    )(bq, bk, bk, bvt, bvt, gate, bias, sink_row)


def _b_bias(slopes):
    t = B_TILE
    koff = np.arange(2 * t)[:, None] - t
    qoff = np.arange(t)[None, :]
    dist = qoff - koff
    band = (dist >= 0) & (dist < WINDOW)
    out = np.empty((2, 2 * t, B_HEADS * t), np.float32)
    for h in range(B_HEADS):
        term = (-np.float32(slopes[h]) * dist.astype(np.float32) * np.float32(LOG2_E)).astype(np.float32)
        out[1, :, h * t:(h + 1) * t] = np.where(band, term, np.float32(MASK_VALUE))
        out[0, :, h * t:(h + 1) * t] = np.where(band & (koff >= 0), term, np.float32(MASK_VALUE))
    return jnp.asarray(out)


def _attn_c_kernel(q_ref, k_ref, vt_ref, u_ref, gate_ref, o_ref, qz_sc, carry_sc, acc_sc):
    t = C_TILE
    step = pl.program_id(1)
    lane = lax.broadcasted_iota(jnp.int32, (t, C_W), 1)
    for sub in range(C_SUBTILES):
        q = q_ref[sub * t:(sub + 1) * t, :]
        for h in range(C_HEADS):
            own = (lane >= h * HEAD_DIM) & (lane < (h + 1) * HEAD_DIM)
            qz_sc[sub, h * t:(h + 1) * t, :] = jnp.where(own, q, jnp.zeros_like(q))
    u2 = u_ref[...]

    def tile(sub, kj, masked, carry_in):
        ks = pl.ds(pl.multiple_of(kj * t, t), t)
        z = _dot_nt(k_ref[ks, :], qz_sc[sub])
        lse = jnp.log2(1.0 + jnp.exp2(jnp.minimum(z, -z)))
        log_beta = jnp.minimum(z, 0.0) - lse
        l1m = log_beta - z
        if masked:
            krow = lax.broadcasted_iota(jnp.int32, (t, C_HEADS * t), 0)
            qcol = lax.broadcasted_iota(jnp.int32, (t, C_HEADS * t), 1) & (t - 1)
            past = krow < qcol
            l1m = jnp.where(past, l1m, 0.0)
        hi = l1m.astype(BF16)
        lo = (l1m - hi.astype(F32)).astype(BF16)
        tail = jnp.dot(u2, jnp.concatenate([hi, lo], axis=0),
                       preferred_element_type=F32)
        a = jnp.exp2(log_beta + tail + carry_in)
        if masked:
            a = jnp.where(past, a, 0.0)
        a = a.astype(BF16)
        parts = [jnp.dot(vt_ref[kj, h * HEAD_DIM:(h + 1) * HEAD_DIM, :], a[:, h * t:(h + 1) * t],
                         preferred_element_type=F32) for h in range(C_HEADS)]
        return jnp.concatenate(parts, axis=0), carry_in + jnp.sum(l1m, axis=0, keepdims=True)

    tiles = [C_SUBTILES * step + sub for sub in range(C_SUBTILES)]
    alive = []
    for sub in range(C_SUBTILES):
        acc, carry = tile(sub, tiles[sub], True, jnp.zeros((1, C_HEADS * t), F32))
        acc_sc[sub] = acc
        carry_sc[sub] = carry
        alive.append(jnp.logical_and(tiles[sub] > 0, jnp.max(carry) > C_ZERO_WEIGHT_LOG2))

    def cond(state):
        return functools.reduce(jnp.logical_or, state[1:])

    def body(state):
        r = state[0]
        nxt = []
        for sub in range(C_SUBTILES):
            has_keys = tiles[sub] - 1 - r >= 0
            acc, carry = tile(sub, jnp.maximum(tiles[sub] - 1 - r, 0), False, carry_sc[sub])
            acc_sc[sub] += jnp.where(has_keys, acc, 0.0)
            carry_sc[sub] = jnp.where(has_keys, carry, carry_sc[sub])
            more_keys = tiles[sub] - 2 - r >= 0
            nxt.append(jnp.logical_and(more_keys, jnp.max(carry) > C_ZERO_WEIGHT_LOG2))
        return (r + 1, *nxt)

    lax.while_loop(cond, body, (jnp.int32(0), *alive))
    for sub in range(C_SUBTILES):
        rows = slice(sub * t, (sub + 1) * t)
        o_ref[rows, :] = (acc_sc[sub].T * _silu(gate_ref[rows, :])).astype(BF16)


def _attn_c(cq, ck, cvt, u2, gate, b, s):
    t = C_TILE
    tq = C_SUBTILES * t
    nt = s // t
    nq = s // tq
    return pl.pallas_call(
        _attn_c_kernel,
        grid=(b, nq),
        in_specs=[
            pl.BlockSpec((tq, C_W), lambda bi, qi: (bi * nq + qi, 0)),
            pl.BlockSpec((s, C_W), lambda bi, qi: (bi, 0)),
            pl.BlockSpec((nt, C_W, t), lambda bi, qi: (bi, 0, 0)),
            pl.BlockSpec((t, 2 * t), lambda bi, qi: (0, 0)),
            pl.BlockSpec((tq, C_W), lambda bi, qi: (bi * nq + qi, GATE_BLOCK_C)),
        ],
        out_specs=pl.BlockSpec((tq, C_W), lambda bi, qi: (bi * nq + qi, 0)),
        out_shape=jax.ShapeDtypeStruct((b * s, C_W), BF16),
        scratch_shapes=[
            pltpu.VMEM((C_SUBTILES, C_HEADS * t, C_W), BF16),
            pltpu.VMEM((C_SUBTILES, 1, C_HEADS * t), F32),
            pltpu.VMEM((C_SUBTILES, C_W, t), F32),
        ],
        compiler_params=pltpu.CompilerParams(
            dimension_semantics=("parallel", "arbitrary"),
            vmem_limit_bytes=VMEM_LIMIT_BYTES),
        name="attn_c",
    )(cq, ck, cvt, u2, gate)


def _outproj_kernel(ya_ref, yb_ref, yc_ref, x_ref, w_ref, fg_ref, o_ref, *, final_norm):
    y = jnp.concatenate([ya_ref[...], yb_ref[...], yc_ref[...]], axis=1)
    r = x_ref[...] + jnp.dot(y, w_ref[...], preferred_element_type=F32)
    if final_norm:
        ms = jnp.mean(r * r, axis=-1, keepdims=True)
        r = r * lax.rsqrt(ms + NORM_EPS) * fg_ref[...]
    o_ref[...] = r


def _outproj(ya, yb, yc, x2, w, fg, layer, final_norm):
    m, d = x2.shape
    row = lambda i: (i, 0)
    fixed = lambda i: (0, 0)
    return pl.pallas_call(
        functools.partial(_outproj_kernel, final_norm=final_norm),
        grid=(m // OUT_ROW_TILE,),
        in_specs=[
            pl.BlockSpec((OUT_ROW_TILE, A_W), row),
            pl.BlockSpec((OUT_ROW_TILE, B_W), row),
            pl.BlockSpec((OUT_ROW_TILE, C_W), row),
            pl.BlockSpec((OUT_ROW_TILE, d), row),
            pl.BlockSpec((None, GATE_W, d), lambda i: (layer, 0, 0)),
            pl.BlockSpec((1, d), fixed),
        ],
        out_specs=pl.BlockSpec((OUT_ROW_TILE, d), row),
        out_shape=jax.ShapeDtypeStruct((m, d), F32),
        compiler_params=pltpu.CompilerParams(
            dimension_semantics=("parallel",), vmem_limit_bytes=VMEM_LIMIT_BYTES),
        name="outproj",
    )(ya, yb, yc, x2, w, fg)


def _split_w_in(w):
    splits = (A_W, A_W, A_W, A_W, B_W, B_KV_W, B_KV_W, B_W, C_W, C_W, C_W, C_W)
    pts = np.cumsum(splits)[:-1]
    aq, ak, av, ag, bq, bk, bv, bg, cq, ck, cv, cg = jnp.split(w, pts, axis=-1)
    depth, d = w.shape[:2]
    bq = bq.reshape(depth, d, B_KV_HEADS, B_GROUP, HEAD_DIM).transpose(0, 1, 3, 2, 4)
    bq = bq.reshape(depth, d, B_W)
    wrm = jnp.concatenate([ak, bq, bk, cq, ck, bg, ag, cg], axis=-1)
    wt = jnp.swapaxes(jnp.concatenate([av, bv, cv, aq], axis=-1), 1, 2)
    return wrm.astype(BF16), wt.astype(BF16)


def _q_scales():
    cs = np.ones((1, OFF_GATE), np.float32)
    cs[0, OFF_BQ:OFF_BK] = HEAD_DIM ** -0.5 * LOG2_E
    cs[0, OFF_CQ:OFF_CK] = HEAD_DIM ** -0.5 * LOG2_E
    ts = np.ones((T_W, 1), np.float32)
    ts[TOFF_AQ:, 0] = A_QK_DIM ** -0.5 * LOG2_E
    return jnp.asarray(cs), jnp.asarray(ts)


def kernel(x, norm_g, w_in, lambda_q1, lambda_k1, lambda_q2, lambda_k2, subln_g, sinks, w_out, final_g):
    b, s, d = x.shape
    depth = w_in.shape[0]
    m = b * s
    slopes_a, slopes_b = _alibi_slopes()

    pos = jnp.arange(s, dtype=jnp.int32)
    p_hi = (pos // 256).astype(BF16)
    p_lo = (pos % 256).astype(BF16)
    kfeat = jnp.stack([p_hi] * 3 + [p_lo] * 3, axis=-1)
    kfeat = jnp.pad(kfeat, ((0, 0), (0, LANES - 6)))
    qfeat_np = np.zeros((LANES, A_HEADS), np.float32)
    for h, sl in enumerate(slopes_a):
        sl2 = np.float32(np.float32(sl) * np.float32(LOG2_E))
        qfeat_np[:6, h] = _split3_bf16(np.float32(256.0) * sl2) + _split3_bf16(sl2)
    qfeat = jnp.asarray(qfeat_np)

    tri = (lax.broadcasted_iota(jnp.int32, (C_TILE, C_TILE), 1)
           > lax.broadcasted_iota(jnp.int32, (C_TILE, C_TILE), 0)).astype(BF16)
    tri2 = jnp.concatenate([tri, tri], axis=1)
    b_bias = _b_bias(slopes_b)
    cs, ts = _q_scales()

    x2 = x.reshape(m, d)
    wrm_all, wt_all = _split_w_in(w_in)
    w_out_all = w_out.astype(BF16)
    sg_all = jnp.tile(subln_g, (1, A_HEADS)).reshape(depth, -1, 1)
    sink_rows = jnp.repeat(sinks * LOG2_E, B_TILE, axis=1).reshape(depth, 1, -1)
    for l in range(depth):
        lambda_init = 0.8 - 0.6 * math.exp(-0.3 * l)
        ak, bq, bk, cq, ck, gate, avt, bvt, cvt, aqt = _inproj(
            x2, norm_g[l].reshape(1, d), wrm_all, wt_all, cs, ts, l)
        ya = _attn_a(aqt, ak, kfeat, avt, gate, qfeat,
                     lambda_q1[l].reshape(1, -1), lambda_k1[l].reshape(1, -1),
                     lambda_q2[l].reshape(1, -1), lambda_k2[l].reshape(1, -1),
                     sg_all[l], lambda_init, b, s)
        yb = _attn_b(bq, bk, bvt, gate, b_bias, sink_rows[l], b, s)
        yc = _attn_c(cq, ck, cvt, tri2, gate, b, s)
        x2 = _outproj(ya, yb, yc, x2, w_out_all, final_g.reshape(1, d), l,
                      final_norm=(l == depth - 1))
    return x2.reshape(b, s, d)
```
